```python
import jax, jax.numpy as jnp
from jax import lax
import numpy as np

D_MODEL = 1024
BATCH = 2
SEQ = 8192
DEPTH = 4
DEC_BATCH = 128
DEC_SEQ = 4
PAST_LEN = 8192
PAGE_SIZE = 128

HEAD_DIM = 64
N_HEADS = D_MODEL // HEAD_DIM
Q_DIM = N_HEADS * HEAD_DIM
ROPE_THETA = 10000.0
NORM_EPS = 1e-6
ATTN_BLOCK = 128
N_MIXERS = 4
D_FF = -(-(8 * D_MODEL) // (3 * 256)) * 256
NEG_INF = -1e30

NSA_KV_HEADS = 2
NSA_CMP_BLOCK = 32
NSA_SEL_BLOCK = 64
NSA_N_SEL = 16
NSA_WINDOW = 512
NSA_FORCE_BONUS = 1e4
NSA_IN = Q_DIM + 6 * NSA_KV_HEADS * HEAD_DIM + 3 * N_HEADS

DIL_KV_HEADS = 4
DIL_GROUPS = ((128, 1), (512, 4), (2048, 16))
DIL_IN = Q_DIM + len(DIL_GROUPS) * 2 * DIL_KV_HEADS * HEAD_DIM

DSA_KV_HEADS = 4
IDX_HEADS = 8
IDX_DIM = 64
DSA_TOPK = 256
IDX_SCALE = (IDX_DIM * IDX_HEADS) ** -0.5
DSA_IN = Q_DIM + 2 * DSA_KV_HEADS * HEAD_DIM + IDX_HEADS * IDX_DIM + IDX_DIM + IDX_HEADS

SWA_KV_HEADS = 2
SWA_WINDOW = 128
SWA_IN = Q_DIM + 2 * SWA_KV_HEADS * HEAD_DIM

N_NSA_LAYERS = (DEPTH + 3) // 4
N_DIL_LAYERS = (DEPTH + 2) // 4
N_DSA_LAYERS = (DEPTH + 1) // 4
N_SWA_LAYERS = DEPTH // 4

kernel_name = "hybrid_nsa_dilated_dsa_swa_decode_step"


def rms_norm(x, g):
    xf = x.astype(jnp.float32)
    y = xf * lax.rsqrt(jnp.mean(xf * xf, axis=-1, keepdims=True) + NORM_EPS)
    return (y * g.astype(jnp.float32)).astype(x.dtype)


def rope(x, pos):
    half = x.shape[-1] // 2
    inv = ROPE_THETA ** (-jnp.arange(half, dtype=jnp.float32) / half)
    ang = pos.astype(jnp.float32)[:, None] * inv[None, :]
    shape = (pos.shape[0],) + (1,) * (x.ndim - 3) + (half,)
    cos, sin = jnp.cos(ang).reshape(shape), jnp.sin(ang).reshape(shape)
    xf = x.astype(jnp.float32)
    x1, x2 = xf[..., :half], xf[..., half:]
    return jnp.concatenate([x1 * cos - x2 * sin, x2 * cos + x1 * sin], axis=-1).astype(x.dtype)


def swiglu(x, w_in, w_out):
    gate, up = jnp.split(x @ w_in, 2, axis=-1)
    return (jax.nn.silu(gate) * up) @ w_out


def _tail(rows, window):
    n = rows.shape[1]
    return rows[:, n - min(window, n):]


def _masked_softmax(s, mask, sink=None):
    s = jnp.where(mask, s, NEG_INF)
    m = jnp.max(s, axis=-1, keepdims=True)
    if sink is not None:
        m = jnp.maximum(m, sink)
    e = jnp.where(mask, jnp.exp(s - m), 0.0)
    den = jnp.sum(e, axis=-1, keepdims=True)
    if sink is not None:
        den = den + jnp.exp(sink - m)
    den = jnp.maximum(den, 1e-30)
    return e / den, (m + jnp.log(den))[..., 0]


def _attend_dense(q, k, v, mask, sink=None):
    s = jnp.einsum('nqgrd,nkgd->ngrqk', q, k, preferred_element_type=jnp.float32) * (q.shape[-1] ** -0.5)
    sk = None if sink is None else sink.astype(jnp.float32)[None, :, :, None, None]
    p, lse = _masked_softmax(s, mask[:, None, None], sk)
    o = jnp.einsum('ngrqk,nkgd->nqgrd', p.astype(v.dtype), v)
    return o, jnp.transpose(lse, (0, 3, 1, 2))


def _attend_gathered(q, k, v, mask):
    s = jnp.einsum('nqgrd,nqkgd->nqgrk', q, k, preferred_element_type=jnp.float32) * (q.shape[-1] ** -0.5)
    p, lse = _masked_softmax(s, jnp.swapaxes(mask, 2, 3)[:, :, :, None, :])
    o = jnp.einsum('nqgrk,nqkgd->nqgrd', p.astype(v.dtype), v)
    return o, lse


def _blockwise(fn, n_blocks):
    out = lax.map(fn, jnp.arange(n_blocks))
    return jax.tree_util.tree_map(
        lambda a: jnp.moveaxis(a, 0, 1).reshape((a.shape[1], -1) + a.shape[3:]), out)


def _banded(q, k, v, window, sink=None):
    pad = ((0, 0), (window, 0), (0, 0), (0, 0))
    kp, vp = jnp.pad(k, pad), jnp.pad(v, pad)
    span = window + ATTN_BLOCK

    def body(i):
        start = i * ATTN_BLOCK
        qi = lax.dynamic_slice_in_dim(q, start, ATTN_BLOCK, 1)
        ki = lax.dynamic_slice_in_dim(kp, start, span, 1)
        vi = lax.dynamic_slice_in_dim(vp, start, span, 1)
        qpos = start + jnp.arange(ATTN_BLOCK)
        kpos = start - window + jnp.arange(span)
        diff = qpos[:, None] - kpos[None, :]
        mask = (diff >= 0) & (diff < window) & (kpos >= 0)[None, :]
        return _attend_dense(qi, ki, vi, mask[None], sink)[0]
    return _blockwise(body, q.shape[1] // ATTN_BLOCK)


def _nsa_project(h, w_in, pos):
    n, t, _ = h.shape
    g, r = NSA_KV_HEADS, N_HEADS // NSA_KV_HEADS
    z = h @ w_in
    kv_end = Q_DIM + 6 * g * HEAD_DIM
    q = rope(z[..., :Q_DIM].reshape(n, t, g, r, HEAD_DIM), pos)
    kv = z[..., Q_DIM:kv_end].reshape(n, t, 3, 2, g, HEAD_DIM)
    kv = jnp.stack([rope(kv[:, :, :, 0], pos), kv[:, :, :, 1]], axis=3).reshape(n, t, 6, g, HEAD_DIM)
    gates = jax.nn.sigmoid(z[..., kv_end:].astype(jnp.float32)).reshape(n, t, g, r, 3)
    return q, kv, gates


def _nsa_compress(rows, w_ck, w_cv):
    n, length, _, g, dh = rows.shape
    nb = length // NSA_CMP_BLOCK
    r = rows[:, :nb * NSA_CMP_BLOCK].reshape(n, nb, NSA_CMP_BLOCK, 2, g, dh)
    ck = jnp.einsum('nbcgd,gcde->nbge', r[:, :, :, 0], w_ck)
    cv = jnp.einsum('nbcgd,gcde->nbge', r[:, :, :, 1], w_cv)
    return ck, cv


def _nsa_cmp_attend(q, ck, cv, qpos):
    blk_end = (jnp.arange(ck.shape[1]) + 1) * NSA_CMP_BLOCK - 1
    mask = (blk_end[None, :] <= qpos[:, None])[None, :, None, None, :]
    s = jnp.einsum('nqgrd,ncgd->nqgrc', q, ck, preferred_element_type=jnp.float32) * (q.shape[-1] ** -0.5)
    p, _ = _masked_softmax(s, mask)
    o = jnp.einsum('nqgrc,ncgd->nqgrd', p.astype(cv.dtype), cv)
    return o, p


def _nsa_select(p_cmp, qpos, n_slc):
    n, t, g = p_cmp.shape[:3]
    ratio = NSA_SEL_BLOCK // NSA_CMP_BLOCK
    imp = p_cmp.sum(axis=3)
    imp = jnp.pad(imp, ((0, 0), (0, 0), (0, 0), (0, n_slc * ratio - imp.shape[-1])))
    imp = imp.reshape(n, t, g, n_slc, ratio).sum(-1)
    j = jnp.arange(n_slc)[None, :]
    cur = (qpos // NSA_SEL_BLOCK)[:, None]
    forced = ((j == 0) | (j == cur) | (j == cur - 1))[None, :, None, :]
    future = (j > cur)[None, :, None, :]
    score = jnp.where(future, NEG_INF, jnp.where(forced, imp + NSA_FORCE_BONUS, imp))
    return lax.top_k(score, min(NSA_N_SEL, n_slc))[1]


def _sel_positions(sel):
    n, t, g, ns = sel.shape
    pos = sel[..., None] * NSA_SEL_BLOCK + jnp.arange(NSA_SEL_BLOCK)
    return jnp.swapaxes(pos.reshape(n, t, g, ns * NSA_SEL_BLOCK), 2, 3)


def _nsa_gate(gates, o_cmp, o_slc, o_win, w_out, dtype):
    f = jnp.float32
    o = (gates[..., 0:1] * o_cmp.astype(f) + gates[..., 1:2] * o_slc.astype(f)
         + gates[..., 2:3] * o_win.astype(f))
    n, t = o.shape[:2]
    return o.astype(dtype).reshape(n, t, Q_DIM) @ w_out


def nsa_prompt(h, w_in, w_ck, w_cv, w_out):
    n, s, _ = h.shape
    pos = jnp.arange(s)
    q, kv, gates = _nsa_project(h, w_in, pos)
    ck, cv = _nsa_compress(kv[:, :, 0:2], w_ck, w_cv)
    o_cmp, p_cmp = _nsa_cmp_attend(q, ck, cv, pos)
    sel = _nsa_select(p_cmp, pos, -(-s // NSA_SEL_BLOCK))
    slc_k, slc_v = kv[:, :, 2], kv[:, :, 3]
    bidx = jnp.arange(n)[:, None, None, None]
    gidx = jnp.arange(NSA_KV_HEADS)[None, None, None, :]

    def body(i):
        start = i * ATTN_BLOCK
        qpos = start + jnp.arange(ATTN_BLOCK)
        kpos = _sel_positions(lax.dynamic_slice_in_dim(sel, start, ATTN_BLOCK, 1))
        qi = lax.dynamic_slice_in_dim(q, start, ATTN_BLOCK, 1)
        return _attend_gathered(qi, slc_k[bidx, kpos, gidx], slc_v[bidx, kpos, gidx],
                                kpos <= qpos[None, :, None, None])[0]
    o_slc = _blockwise(body, s // ATTN_BLOCK)
    o_win = _banded(q, kv[:, :, 4], kv[:, :, 5], NSA_WINDOW)
    y = _nsa_gate(gates, o_cmp, o_slc, o_win, w_out, h.dtype)
    return y, kv[:, :, 0:4], _tail(kv[:, :, 4:6], NSA_WINDOW)


def nsa_sample(h, pool, layer, win_buf, page_table, w_in, w_ck, w_cv, w_out, past):
    n, t, _ = h.shape
    pos = past + jnp.arange(t)
    q, kv, gates = _nsa_project(h, w_in, pos)
    past_rows = pool[layer, page_table, :, 0:2].reshape(n, past, 2, NSA_KV_HEADS, HEAD_DIM)
    ck_p, cv_p = _nsa_compress(past_rows, w_ck, w_cv)
    ck_n, cv_n = _nsa_compress(kv[:, :, 0:2], w_ck, w_cv)
    o_cmp, p_cmp = _nsa_cmp_attend(q, jnp.concatenate([ck_p, ck_n], 1),
                                   jnp.concatenate([cv_p, cv_n], 1), pos)
    sel = _nsa_select(p_cmp, pos, -(-(past + t) // NSA_SEL_BLOCK))
    kpos = _sel_positions(sel)
    bidx = jnp.arange(n)[:, None, None, None]
    gidx = jnp.arange(NSA_KV_HEADS)[None, None, None, :]
    pc = jnp.clip(kpos, 0, past - 1)
    phys = page_table[bidx, pc // PAGE_SIZE]
    off = pc % PAGE_SIZE
    nc = jnp.clip(kpos - past, 0, t - 1)
    in_past = (kpos < past)[..., None]
    k = jnp.where(in_past, pool[layer, phys, off, 2, gidx], kv[bidx, nc, 2, gidx])
    v = jnp.where(in_past, pool[layer, phys, off, 3, gidx], kv[bidx, nc, 3, gidx])
    o_slc, _ = _attend_gathered(q, k, v, kpos <= pos[None, :, None, None])
    keys = jnp.concatenate([win_buf, kv[:, :, 4:6]], axis=1)
    kp = past - win_buf.shape[1] + jnp.arange(keys.shape[1])
    diff = pos[:, None] - kp[None, :]
    o_win, _ = _attend_dense(q, keys[:, :, 0], keys[:, :, 1], ((diff >= 0) & (diff < NSA_WINDOW))[None])
    y = _nsa_gate(gates, o_cmp, o_slc, o_win, w_out, h.dtype)
    return y, kv[:, :, 0:4], _tail(keys, NSA_WINDOW)


def _dil_project(h, w_in, pos):
    n, t, _ = h.shape
    g = DIL_KV_HEADS
    z = h @ w_in
    q = rope(z[..., :Q_DIM].reshape(n, t, g, N_HEADS // g, HEAD_DIM), pos)
    kv = z[..., Q_DIM:].reshape(n, t, len(DIL_GROUPS), 2, g, HEAD_DIM)
    return q, jnp.stack([rope(kv[:, :, :, 0], pos), kv[:, :, :, 1]], axis=3)


def _dil_mix(outs, lses):
    w = jax.nn.softmax(jnp.stack(lses), axis=0)
    return jnp.sum(w[..., None] * jnp.stack(outs).astype(jnp.float32), axis=0)


def _dil_group_prompt(q, k, v, dil, n_keys):
    dist = jnp.arange(n_keys) * dil

    def body(i):
        start = i * ATTN_BLOCK
        qi = lax.dynamic_slice_in_dim(q, start, ATTN_BLOCK, 1)
        kpos = (start + jnp.arange(ATTN_BLOCK))[:, None] - dist[None, :]
        kc = jnp.maximum(kpos, 0)
        return _attend_gathered(qi, k[:, kc], v[:, kc], (kpos >= 0)[None, :, :, None])
    return _blockwise(body, q.shape[1] // ATTN_BLOCK)


def dil_prompt(h, w_in, w_out):
    n, s, _ = h.shape
    q, kv = _dil_project(h, w_in, jnp.arange(s))
    outs, lses, bufs = [], [], []
    for gi, (win, dil) in enumerate(DIL_GROUPS):
        o, lse = _dil_group_prompt(q, kv[:, :, gi, 0], kv[:, :, gi, 1], dil, win // dil + 1)
        outs.append(o)
        lses.append(lse)
        bufs.append(_tail(kv[:, :, gi], win))
    y = _dil_mix(outs, lses).astype(h.dtype).reshape(n, s, Q_DIM) @ w_out
    return y, bufs


def dil_sample(h, bufs, w_in, w_out, past):
    n, t, _ = h.shape
    pos = past + jnp.arange(t)
    q, kv = _dil_project(h, w_in, pos)
    outs, lses, new_bufs = [], [], []
    for gi, ((win, dil), buf) in enumerate(zip(DIL_GROUPS, bufs)):
        keys = jnp.concatenate([buf, kv[:, :, gi]], axis=1)
        base = past - buf.shape[1]
        local = pos[:, None] - (jnp.arange(win // dil + 1) * dil)[None, :] - base
        lc = jnp.maximum(local, 0)
        o, lse = _attend_gathered(q, keys[:, lc, 0], keys[:, lc, 1], (local >= 0)[None, :, :, None])
        outs.append(o)
        lses.append(lse)
        new_bufs.append(_tail(keys, win))
    y = _dil_mix(outs, lses).astype(h.dtype).reshape(n, t, Q_DIM) @ w_out
    return y, new_bufs


def _dsa_project(h, w_in, pos):
    n, t, _ = h.shape
    g = DSA_KV_HEADS
    z = h @ w_in
    o1 = Q_DIM
    o2 = o1 + 2 * g * HEAD_DIM
    o3 = o2 + IDX_HEADS * IDX_DIM
    o4 = o3 + IDX_DIM
    q = rope(z[..., :o1].reshape(n, t, g, N_HEADS // g, HEAD_DIM), pos)
    kv = z[..., o1:o2].reshape(n, t, 2, g, HEAD_DIM)
    kv = jnp.stack([rope(kv[:, :, 0], pos), kv[:, :, 1]], axis=2)
    q_idx = rope(z[..., o2:o3].reshape(n, t, IDX_HEADS, IDX_DIM), pos)
    k_idx = rope(z[..., o3:o4], pos)
    w_idx = z[..., o4:].astype(jnp.float32) * IDX_SCALE
    return q, kv, q_idx, k_idx, w_idx


def _index_scores(q_idx, w_idx, k_idx):
    dots = jnp.einsum('nqhd,nkd->nqhk', q_idx, k_idx, preferred_element_type=jnp.float32)
    return jnp.einsum('nqh,nqhk->nqk', w_idx, jax.nn.relu(dots))


def dsa_prompt(h, w_in, w_out):
    n, s, _ = h.shape
    pos = jnp.arange(s)
    q, kv, q_idx, k_idx, w_idx = _dsa_project(h, w_in, pos)
    top = min(DSA_TOPK, s // 4)
    bidx = jnp.arange(n)[:, None, None]

    def body(i):
        start = i * ATTN_BLOCK
        qpos = start + jnp.arange(ATTN_BLOCK)
        sc = _index_scores(lax.dynamic_slice_in_dim(q_idx, start, ATTN_BLOCK, 1),
                           lax.dynamic_slice_in_dim(w_idx, start, ATTN_BLOCK, 1), k_idx)
        sc = jnp.where((pos[None, :] <= qpos[:, None])[None], sc, NEG_INF)
        _, idx = lax.top_k(sc, top)
        sel = kv[bidx, idx]
        qi = lax.dynamic_slice_in_dim(q, start, ATTN_BLOCK, 1)
        return _attend_gathered(qi, sel[:, :, :, 0], sel[:, :, :, 1],
                                (idx <= qpos[None, :, None])[..., None])[0]
    o = _blockwise(body, s // ATTN_BLOCK)
    y = o.reshape(n, s, Q_DIM) @ w_out
    return y, kv, k_idx


def dsa_sample(h, pool_kv, pool_idx, layer, page_table, w_in, w_out, past):
    n, t, _ = h.shape
    pos = past + jnp.arange(t)
    q, kv, q_idx, k_idx, w_idx = _dsa_project(h, w_in, pos)
    total = past + t
    k_idx_all = jnp.concatenate([pool_idx[layer, page_table].reshape(n, past, IDX_DIM), k_idx], axis=1)
    sc = _index_scores(q_idx, w_idx, k_idx_all)
    sc = jnp.where((jnp.arange(total)[None, :] <= pos[:, None])[None], sc, NEG_INF)
    _, idx = lax.top_k(sc, min(DSA_TOPK, total // 4))
    bidx = jnp.arange(n)[:, None, None]
    pc = jnp.clip(idx, 0, past - 1)
    kv_past = pool_kv[layer, page_table[bidx, pc // PAGE_SIZE], pc % PAGE_SIZE]
    kv_new = kv[bidx, jnp.clip(idx - past, 0, t - 1)]
    sel = jnp.where((idx < past)[..., None, None, None], kv_past, kv_new)
    o, _ = _attend_gathered(q, sel[:, :, :, 0], sel[:, :, :, 1], (idx <= pos[None, :, None])[..., None])
    y = o.reshape(n, t, Q_DIM) @ w_out
    return y, kv, k_idx


def _swa_project(h, w_in, pos):
    n, t, _ = h.shape
    g = SWA_KV_HEADS
    z = h @ w_in
    q = rope(z[..., :Q_DIM].reshape(n, t, g, N_HEADS // g, HEAD_DIM), pos)
    kv = z[..., Q_DIM:].reshape(n, t, 2, g, HEAD_DIM)
    return q, jnp.stack([rope(kv[:, :, 0], pos), kv[:, :, 1]], axis=2)


def swa_prompt(h, w_in, sink, w_out):
    n, s, _ = h.shape
    q, kv = _swa_project(h, w_in, jnp.arange(s))
    o = _banded(q, kv[:, :, 0], kv[:, :, 1], SWA_WINDOW, sink.reshape(SWA_KV_HEADS, -1))
    return o.reshape(n, s, Q_DIM) @ w_out, _tail(kv, SWA_WINDOW)


def swa_sample(h, buf, w_in, sink, w_out, past):
    n, t, _ = h.shape
    pos = past + jnp.arange(t)
    q, kv = _swa_project(h, w_in, pos)
    keys = jnp.concatenate([buf, kv], axis=1)
    kp = past - buf.shape[1] + jnp.arange(keys.shape[1])
    diff = pos[:, None] - kp[None, :]
    o, _ = _attend_dense(q, keys[:, :, 0], keys[:, :, 1], ((diff >= 0) & (diff < SWA_WINDOW))[None],
                         sink.reshape(SWA_KV_HEADS, -1))
    return o.reshape(n, t, Q_DIM) @ w_out, _tail(keys, SWA_WINDOW)


def setup_inputs(seed: int = 0) -> dict:
    key = jax.random.key(seed)
    ks = iter(jax.random.split(key, 40))

    def nrm(shape, scale=1.0):
        return jax.random.normal(next(ks), shape, jnp.float32) * scale

    n_pages = PAST_LEN // PAGE_SIZE
    n_pool = (5 * DEC_BATCH * n_pages + 3) // 4
    page_table = jax.random.permutation(next(ks), n_pool)[:DEC_BATCH * n_pages]
    page_table = page_table.reshape(DEC_BATCH, n_pages).astype(jnp.int32)

    def buf(n_layers, window, kvh):
        return nrm((n_layers, DEC_BATCH, min(window, PAST_LEN), 2, kvh, HEAD_DIM))

    d_in, q_in, f_in = D_MODEL ** -0.5, Q_DIM ** -0.5, D_FF ** -0.5
    return {
        "x_prompt": nrm((BATCH, SEQ, D_MODEL)),
        "x_sample": nrm((DEC_BATCH, DEC_SEQ, D_MODEL)),
        "cache_nsa_kv": nrm((N_NSA_LAYERS, n_pool, PAGE_SIZE, 4, NSA_KV_HEADS, HEAD_DIM)),
        "cache_nsa_win": buf(N_NSA_LAYERS, NSA_WINDOW, NSA_KV_HEADS),
        "cache_dil1": buf(N_DIL_LAYERS, DIL_GROUPS[0][0], DIL_KV_HEADS),
        "cache_dil2": buf(N_DIL_LAYERS, DIL_GROUPS[1][0], DIL_KV_HEADS),
        "cache_dil3": buf(N_DIL_LAYERS, DIL_GROUPS[2][0], DIL_KV_HEADS),
        "cache_dsa_kv": nrm((N_DSA_LAYERS, n_pool, PAGE_SIZE, 2, DSA_KV_HEADS, HEAD_DIM)),
        "cache_dsa_idx": nrm((N_DSA_LAYERS, n_pool, PAGE_SIZE, IDX_DIM)),
        "cache_swa": buf(N_SWA_LAYERS, SWA_WINDOW, SWA_KV_HEADS),
        "page_table": page_table,
        "norm_mix": 1.0 + nrm((DEPTH, D_MODEL), 0.05),
        "norm_ffn": 1.0 + nrm((DEPTH, D_MODEL), 0.05),
        "norm_final": 1.0 + nrm((D_MODEL,), 0.05),
        "ffn_in": nrm((DEPTH, D_MODEL, 2 * D_FF), d_in),
        "ffn_out": nrm((DEPTH, D_FF, D_MODEL), f_in),
        "nsa_w_in": nrm((N_NSA_LAYERS, D_MODEL, NSA_IN), d_in),
        "nsa_w_cmp_k": nrm((N_NSA_LAYERS, NSA_KV_HEADS, NSA_CMP_BLOCK, HEAD_DIM, HEAD_DIM), (NSA_CMP_BLOCK * HEAD_DIM) ** -0.5),
        "nsa_w_cmp_v": nrm((N_NSA_LAYERS, NSA_KV_HEADS, NSA_CMP_BLOCK, HEAD_DIM, HEAD_DIM), (NSA_CMP_BLOCK * HEAD_DIM) ** -0.5),
        "nsa_w_out": nrm((N_NSA_LAYERS, Q_DIM, D_MODEL), q_in),
        "dil_w_in": nrm((N_DIL_LAYERS, D_MODEL, DIL_IN), d_in),
        "dil_w_out": nrm((N_DIL_LAYERS, Q_DIM, D_MODEL), q_in),
        "dsa_w_in": nrm((N_DSA_LAYERS, D_MODEL, DSA_IN), d_in),
        "dsa_w_out": nrm((N_DSA_LAYERS, Q_DIM, D_MODEL), q_in),
        "swa_w_in": nrm((N_SWA_LAYERS, D_MODEL, SWA_IN), d_in),
        "swa_sink": nrm((N_SWA_LAYERS, N_HEADS), 0.5),
        "swa_w_out": nrm((N_SWA_LAYERS, Q_DIM, D_MODEL), q_in),
    }


def reference(x_prompt, x_sample, cache_nsa_kv, cache_nsa_win, cache_dil1, cache_dil2, cache_dil3,
              cache_dsa_kv, cache_dsa_idx, cache_swa, page_table,
              norm_mix, norm_ffn, norm_final, ffn_in, ffn_out,
              nsa_w_in, nsa_w_cmp_k, nsa_w_cmp_v, nsa_w_out,
              dil_w_in, dil_w_out, dsa_w_in, dsa_w_out,
              swa_w_in, swa_sink, swa_w_out):
    past = page_table.shape[1] * PAGE_SIZE
    hp, hs = x_prompt, x_sample
    st = {name: [] for name in ("nsa_kv_p", "nsa_kv_s", "nsa_win_p", "nsa_win_s",
                                "dil1_p", "dil1_s", "dil2_p", "dil2_s", "dil3_p", "dil3_s",
                                "dsa_kv_p", "dsa_kv_s", "dsa_idx_p", "dsa_idx_s", "swa_p", "swa_s")}
    for i in range(DEPTH):
        kind, j = i % N_MIXERS, i // N_MIXERS
        ap, as_ = rms_norm(hp, norm_mix[i]), rms_norm(hs, norm_mix[i])
        if kind == 0:
            yp, kv_p, win_p = nsa_prompt(ap, nsa_w_in[j], nsa_w_cmp_k[j], nsa_w_cmp_v[j], nsa_w_out[j])
            ys, kv_s, win_s = nsa_sample(as_, cache_nsa_kv, j, cache_nsa_win[j], page_table,
                                         nsa_w_in[j], nsa_w_cmp_k[j], nsa_w_cmp_v[j], nsa_w_out[j], past)
            st["nsa_kv_p"].append(kv_p)
            st["nsa_kv_s"].append(kv_s)
            st["nsa_win_p"].append(win_p)
            st["nsa_win_s"].append(win_s)
        elif kind == 1:
            yp, bufs_p = dil_prompt(ap, dil_w_in[j], dil_w_out[j])
            ys, bufs_s = dil_sample(as_, [cache_dil1[j], cache_dil2[j], cache_dil3[j]],
                                    dil_w_in[j], dil_w_out[j], past)
            for gi in range(len(DIL_GROUPS)):
                st["dil%d_p" % (gi + 1)].append(bufs_p[gi])
                st["dil%d_s" % (gi + 1)].append(bufs_s[gi])
        elif kind == 2:
            yp, kv_p, idx_p = dsa_prompt(ap, dsa_w_in[j], dsa_w_out[j])
            ys, kv_s, idx_s = dsa_sample(as_, cache_dsa_kv, cache_dsa_idx, j, page_table,
                                         dsa_w_in[j], dsa_w_out[j], past)
            st["dsa_kv_p"].append(kv_p)
            st["dsa_kv_s"].append(kv_s)
            st["dsa_idx_p"].append(idx_p)
            st["dsa_idx_s"].append(idx_s)
        else:
            yp, buf_p = swa_prompt(ap, swa_w_in[j], swa_sink[j], swa_w_out[j])
            ys, buf_s = swa_sample(as_, cache_swa[j], swa_w_in[j], swa_sink[j], swa_w_out[j], past)
            st["swa_p"].append(buf_p)
            st["swa_s"].append(buf_s)
        hp = hp + yp
        hs = hs + ys
        hp = hp + swiglu(rms_norm(hp, norm_ffn[i]), ffn_in[i], ffn_out[i])
        hs = hs + swiglu(rms_norm(hs, norm_ffn[i]), ffn_in[i], ffn_out[i])
    y_prompt = rms_norm(hp, norm_final)
    y_sample = rms_norm(hs, norm_final)
    return (y_prompt, y_sample,
            jnp.stack(st["nsa_kv_p"]), jnp.stack(st["nsa_kv_s"]),
            jnp.stack(st["nsa_win_p"]), jnp.stack(st["nsa_win_s"]),
            jnp.stack(st["dil1_p"]), jnp.stack(st["dil1_s"]),
            jnp.stack(st["dil2_p"]), jnp.stack(st["dil2_s"]),
            jnp.stack(st["dil3_p"]), jnp.stack(st["dil3_s"]),
            jnp.stack(st["dsa_kv_p"]), jnp.stack(st["dsa_kv_s"]),
            jnp.stack(st["dsa_idx_p"]), jnp.stack(st["dsa_idx_s"]),
            jnp.stack(st["swa_p"]), jnp.stack(st["swa_s"]))
```

```python
import functools

import jax, jax.numpy as jnp
from jax import lax
import numpy as np
from jax.experimental import pallas as pl
from jax.experimental.pallas import tpu as pltpu

D_MODEL = 1024
BATCH = 2
SEQ = 8192
DEPTH = 4
DEC_BATCH = 128
DEC_SEQ = 4
PAST_LEN = 8192
PAGE_SIZE = 128

HEAD_DIM = 64
N_HEADS = D_MODEL // HEAD_DIM
Q_DIM = N_HEADS * HEAD_DIM
ROPE_THETA = 10000.0
NORM_EPS = 1e-6
ATTN_BLOCK = 128
N_MIXERS = 4
D_FF = -(-(8 * D_MODEL) // (3 * 256)) * 256
NEG_INF = -1e30

NSA_KV_HEADS = 2
NSA_CMP_BLOCK = 32
NSA_SEL_BLOCK = 64
NSA_N_SEL = 16
NSA_WINDOW = 512
NSA_FORCE_BONUS = 1e4
NSA_IN = Q_DIM + 6 * NSA_KV_HEADS * HEAD_DIM + 3 * N_HEADS

DIL_KV_HEADS = 4
DIL_GROUPS = ((128, 1), (512, 4), (2048, 16))
DIL_IN = Q_DIM + len(DIL_GROUPS) * 2 * DIL_KV_HEADS * HEAD_DIM

DSA_KV_HEADS = 4
IDX_HEADS = 8
IDX_DIM = 64
DSA_TOPK = 256
IDX_SCALE = (IDX_DIM * IDX_HEADS) ** -0.5
DSA_IN = Q_DIM + 2 * DSA_KV_HEADS * HEAD_DIM + IDX_HEADS * IDX_DIM + IDX_DIM + IDX_HEADS

SWA_KV_HEADS = 2
SWA_WINDOW = 128
SWA_IN = Q_DIM + 2 * SWA_KV_HEADS * HEAD_DIM

N_NSA_LAYERS = (DEPTH + 3) // 4
N_DIL_LAYERS = (DEPTH + 2) // 4
N_DSA_LAYERS = (DEPTH + 1) // 4
N_SWA_LAYERS = DEPTH // 4


def rms_norm(x, g):
    xf = x.astype(jnp.float32)
    y = xf * lax.rsqrt(jnp.mean(xf * xf, axis=-1, keepdims=True) + NORM_EPS)
    return (y * g.astype(jnp.float32)).astype(x.dtype)


def rope(x, pos):
    half = x.shape[-1] // 2
    inv = ROPE_THETA ** (-jnp.arange(half, dtype=jnp.float32) / half)
    ang = pos.astype(jnp.float32)[:, None] * inv[None, :]
    shape = (pos.shape[0],) + (1,) * (x.ndim - 3) + (half,)
    cos, sin = jnp.cos(ang).reshape(shape), jnp.sin(ang).reshape(shape)
    xf = x.astype(jnp.float32)
    x1, x2 = xf[..., :half], xf[..., half:]
    return jnp.concatenate([x1 * cos - x2 * sin, x2 * cos + x1 * sin], axis=-1).astype(x.dtype)


def _tail(rows, window):
    n = rows.shape[1]
    return rows[:, n - min(window, n):]


def _masked_softmax(s, mask, sink=None):
    s = jnp.where(mask, s, NEG_INF)
    m = jnp.max(s, axis=-1, keepdims=True)
    if sink is not None:
        m = jnp.maximum(m, sink)
    e = jnp.where(mask, jnp.exp(s - m), 0.0)
    den = jnp.sum(e, axis=-1, keepdims=True)
    if sink is not None:
        den = den + jnp.exp(sink - m)
    den = jnp.maximum(den, 1e-30)
    return e / den, (m + jnp.log(den))[..., 0]


def _attend_dense(q, k, v, mask, sink=None):
    s = jnp.einsum('nqgrd,nkgd->ngrqk', q, k, preferred_element_type=jnp.float32) * (q.shape[-1] ** -0.5)
    sk = None if sink is None else sink.astype(jnp.float32)[None, :, :, None, None]
    p, lse = _masked_softmax(s, mask[:, None, None], sk)
    o = jnp.einsum('ngrqk,nkgd->nqgrd', p.astype(v.dtype), v)
    return o, jnp.transpose(lse, (0, 3, 1, 2))


def _attend_gathered(q, k, v, mask):
    s = jnp.einsum('nqgrd,nqkgd->nqgrk', q, k, preferred_element_type=jnp.float32) * (q.shape[-1] ** -0.5)
    p, lse = _masked_softmax(s, jnp.swapaxes(mask, 2, 3)[:, :, :, None, :])
    o = jnp.einsum('nqgrk,nqkgd->nqgrd', p.astype(v.dtype), v)
    return o, lse


def _blockwise(fn, n_blocks):
    out = lax.map(fn, jnp.arange(n_blocks))
    return jax.tree_util.tree_map(
        lambda a: jnp.moveaxis(a, 0, 1).reshape((a.shape[1], -1) + a.shape[3:]), out)


def _banded(q, k, v, window, sink=None):
    pad = ((0, 0), (window, 0), (0, 0), (0, 0))
    kp, vp = jnp.pad(k, pad), jnp.pad(v, pad)
    span = window + ATTN_BLOCK

    def body(i):
        start = i * ATTN_BLOCK
        qi = lax.dynamic_slice_in_dim(q, start, ATTN_BLOCK, 1)
        ki = lax.dynamic_slice_in_dim(kp, start, span, 1)
        vi = lax.dynamic_slice_in_dim(vp, start, span, 1)
        qpos = start + jnp.arange(ATTN_BLOCK)
        kpos = start - window + jnp.arange(span)
        diff = qpos[:, None] - kpos[None, :]
        mask = (diff >= 0) & (diff < window) & (kpos >= 0)[None, :]
        return _attend_dense(qi, ki, vi, mask[None], sink)[0]
    return _blockwise(body, q.shape[1] // ATTN_BLOCK)


def _nsa_project(h, w_in, pos):
    n, t, _ = h.shape
    g, r = NSA_KV_HEADS, N_HEADS // NSA_KV_HEADS
    z = h @ w_in
    kv_end = Q_DIM + 6 * g * HEAD_DIM
    q = rope(z[..., :Q_DIM].reshape(n, t, g, r, HEAD_DIM), pos)
    kv = z[..., Q_DIM:kv_end].reshape(n, t, 3, 2, g, HEAD_DIM)
    kv = jnp.stack([rope(kv[:, :, :, 0], pos), kv[:, :, :, 1]], axis=3).reshape(n, t, 6, g, HEAD_DIM)
    gates = jax.nn.sigmoid(z[..., kv_end:].astype(jnp.float32)).reshape(n, t, g, r, 3)
    return q, kv, gates


def _nsa_compress(rows, w_ck, w_cv):
    n, length, _, g, dh = rows.shape
    nb = length // NSA_CMP_BLOCK
    r = rows[:, :nb * NSA_CMP_BLOCK].reshape(n, nb, NSA_CMP_BLOCK, 2, g, dh)
    ck = jnp.einsum('nbcgd,gcde->nbge', r[:, :, :, 0], w_ck)
    cv = jnp.einsum('nbcgd,gcde->nbge', r[:, :, :, 1], w_cv)
    return ck, cv


def _nsa_cmp_attend(q, ck, cv, qpos):
    blk_end = (jnp.arange(ck.shape[1]) + 1) * NSA_CMP_BLOCK - 1
    mask = (blk_end[None, :] <= qpos[:, None])[None, :, None, None, :]
    s = jnp.einsum('nqgrd,ncgd->nqgrc', q, ck, preferred_element_type=jnp.float32) * (q.shape[-1] ** -0.5)
    p, _ = _masked_softmax(s, mask)
    o = jnp.einsum('nqgrc,ncgd->nqgrd', p.astype(cv.dtype), cv)
    return o, p


def _nsa_select(p_cmp, qpos, n_slc):
    n, t, g = p_cmp.shape[:3]
    ratio = NSA_SEL_BLOCK // NSA_CMP_BLOCK
    imp = p_cmp.sum(axis=3)
    imp = jnp.pad(imp, ((0, 0), (0, 0), (0, 0), (0, n_slc * ratio - imp.shape[-1])))
    imp = imp.reshape(n, t, g, n_slc, ratio).sum(-1)
    j = jnp.arange(n_slc)[None, :]
    cur = (qpos // NSA_SEL_BLOCK)[:, None]
    forced = ((j == 0) | (j == cur) | (j == cur - 1))[None, :, None, :]
    future = (j > cur)[None, :, None, :]
    score = jnp.where(future, NEG_INF, jnp.where(forced, imp + NSA_FORCE_BONUS, imp))
    return lax.top_k(score, min(NSA_N_SEL, n_slc))[1]


def _sel_positions(sel):
    n, t, g, ns = sel.shape
    pos = sel[..., None] * NSA_SEL_BLOCK + jnp.arange(NSA_SEL_BLOCK)
    return jnp.swapaxes(pos.reshape(n, t, g, ns * NSA_SEL_BLOCK), 2, 3)


def _nsa_gate(gates, o_cmp, o_slc, o_win, w_out, dtype):
    f = jnp.float32
    o = (gates[..., 0:1] * o_cmp.astype(f) + gates[..., 1:2] * o_slc.astype(f)
         + gates[..., 2:3] * o_win.astype(f))
    n, t = o.shape[:2]
    return o.astype(dtype).reshape(n, t, Q_DIM) @ w_out


def nsa_prompt(h, w_in, w_ck, w_cv, w_out):
    n, s, _ = h.shape
    pos = jnp.arange(s)
    q, kv, gates = _nsa_project(h, w_in, pos)
    ck, cv = _nsa_compress(kv[:, :, 0:2], w_ck, w_cv)
    o_cmp, p_cmp = _nsa_cmp_attend(q, ck, cv, pos)
    sel = _nsa_select(p_cmp, pos, -(-s // NSA_SEL_BLOCK))
    slc_k, slc_v = kv[:, :, 2], kv[:, :, 3]
    bidx = jnp.arange(n)[:, None, None, None]
    gidx = jnp.arange(NSA_KV_HEADS)[None, None, None, :]

    def body(i):
        start = i * ATTN_BLOCK
        qpos = start + jnp.arange(ATTN_BLOCK)
        kpos = _sel_positions(lax.dynamic_slice_in_dim(sel, start, ATTN_BLOCK, 1))
        qi = lax.dynamic_slice_in_dim(q, start, ATTN_BLOCK, 1)
        return _attend_gathered(qi, slc_k[bidx, kpos, gidx], slc_v[bidx, kpos, gidx],
                                kpos <= qpos[None, :, None, None])[0]
    o_slc = _blockwise(body, s // ATTN_BLOCK)
    o_win = _banded(q, kv[:, :, 4], kv[:, :, 5], NSA_WINDOW)
    y = _nsa_gate(gates, o_cmp, o_slc, o_win, w_out, h.dtype)
    return y, kv[:, :, 0:4], _tail(kv[:, :, 4:6], NSA_WINDOW)


def nsa_sample(h, pool, layer, win_buf, page_table, w_in, w_ck, w_cv, w_out, past):
    n, t, _ = h.shape
    pos = past + jnp.arange(t)
    q, kv, gates = _nsa_project(h, w_in, pos)
    past_rows = pool[layer, page_table, :, 0:2].reshape(n, past, 2, NSA_KV_HEADS, HEAD_DIM)
    ck_p, cv_p = _nsa_compress(past_rows, w_ck, w_cv)
    ck_n, cv_n = _nsa_compress(kv[:, :, 0:2], w_ck, w_cv)
    o_cmp, p_cmp = _nsa_cmp_attend(q, jnp.concatenate([ck_p, ck_n], 1),
                                   jnp.concatenate([cv_p, cv_n], 1), pos)
    sel = _nsa_select(p_cmp, pos, -(-(past + t) // NSA_SEL_BLOCK))
    kpos = _sel_positions(sel)
    bidx = jnp.arange(n)[:, None, None, None]
    gidx = jnp.arange(NSA_KV_HEADS)[None, None, None, :]
    pc = jnp.clip(kpos, 0, past - 1)
    phys = page_table[bidx, pc // PAGE_SIZE]
    off = pc % PAGE_SIZE
    nc = jnp.clip(kpos - past, 0, t - 1)
    in_past = (kpos < past)[..., None]
    k = jnp.where(in_past, pool[layer, phys, off, 2, gidx], kv[bidx, nc, 2, gidx])
    v = jnp.where(in_past, pool[layer, phys, off, 3, gidx], kv[bidx, nc, 3, gidx])
    o_slc, _ = _attend_gathered(q, k, v, kpos <= pos[None, :, None, None])
    keys = jnp.concatenate([win_buf, kv[:, :, 4:6]], axis=1)
    kp = past - win_buf.shape[1] + jnp.arange(keys.shape[1])
    diff = pos[:, None] - kp[None, :]
    o_win, _ = _attend_dense(q, keys[:, :, 0], keys[:, :, 1], ((diff >= 0) & (diff < NSA_WINDOW))[None])
    y = _nsa_gate(gates, o_cmp, o_slc, o_win, w_out, h.dtype)
    return y, kv[:, :, 0:4], _tail(keys, NSA_WINDOW)


def _dil_project(h, w_in, pos):
    n, t, _ = h.shape
    g = DIL_KV_HEADS
    z = h @ w_in
    q = rope(z[..., :Q_DIM].reshape(n, t, g, N_HEADS // g, HEAD_DIM), pos)
    kv = z[..., Q_DIM:].reshape(n, t, len(DIL_GROUPS), 2, g, HEAD_DIM)
    return q, jnp.stack([rope(kv[:, :, :, 0], pos), kv[:, :, :, 1]], axis=3)


def _dil_mix(outs, lses):
    w = jax.nn.softmax(jnp.stack(lses), axis=0)
    return jnp.sum(w[..., None] * jnp.stack(outs).astype(jnp.float32), axis=0)


def _dil_group_prompt(q, k, v, dil, n_keys):
    dist = jnp.arange(n_keys) * dil

    def body(i):
        start = i * ATTN_BLOCK
        qi = lax.dynamic_slice_in_dim(q, start, ATTN_BLOCK, 1)
        kpos = (start + jnp.arange(ATTN_BLOCK))[:, None] - dist[None, :]
        kc = jnp.maximum(kpos, 0)
        return _attend_gathered(qi, k[:, kc], v[:, kc], (kpos >= 0)[None, :, :, None])
    return _blockwise(body, q.shape[1] // ATTN_BLOCK)


def dil_prompt(h, w_in, w_out):
    n, s, _ = h.shape
    q, kv = _dil_project(h, w_in, jnp.arange(s))
    outs, lses, bufs = [], [], []
    for gi, (win, dil) in enumerate(DIL_GROUPS):
        o, lse = _dil_group_prompt(q, kv[:, :, gi, 0], kv[:, :, gi, 1], dil, win // dil + 1)
        outs.append(o)
        lses.append(lse)
        bufs.append(_tail(kv[:, :, gi], win))
    y = _dil_mix(outs, lses).astype(h.dtype).reshape(n, s, Q_DIM) @ w_out
    return y, bufs


def dil_sample(h, bufs, w_in, w_out, past):
    n, t, _ = h.shape
    pos = past + jnp.arange(t)
    q, kv = _dil_project(h, w_in, pos)
    outs, lses, new_bufs = [], [], []
    for gi, ((win, dil), buf) in enumerate(zip(DIL_GROUPS, bufs)):
        keys = jnp.concatenate([buf, kv[:, :, gi]], axis=1)
        base = past - buf.shape[1]
        local = pos[:, None] - (jnp.arange(win // dil + 1) * dil)[None, :] - base
        lc = jnp.maximum(local, 0)
        o, lse = _attend_gathered(q, keys[:, lc, 0], keys[:, lc, 1], (local >= 0)[None, :, :, None])
        outs.append(o)
        lses.append(lse)
        new_bufs.append(_tail(keys, win))
    y = _dil_mix(outs, lses).astype(h.dtype).reshape(n, t, Q_DIM) @ w_out
    return y, new_bufs


def _dsa_project(h, w_in, pos):
    n, t, _ = h.shape
    g = DSA_KV_HEADS
    z = h @ w_in
    o1 = Q_DIM
    o2 = o1 + 2 * g * HEAD_DIM
    o3 = o2 + IDX_HEADS * IDX_DIM
    o4 = o3 + IDX_DIM
    q = rope(z[..., :o1].reshape(n, t, g, N_HEADS // g, HEAD_DIM), pos)
    kv = z[..., o1:o2].reshape(n, t, 2, g, HEAD_DIM)
    kv = jnp.stack([rope(kv[:, :, 0], pos), kv[:, :, 1]], axis=2)
    q_idx = rope(z[..., o2:o3].reshape(n, t, IDX_HEADS, IDX_DIM), pos)
    k_idx = rope(z[..., o3:o4], pos)
    w_idx = z[..., o4:].astype(jnp.float32) * IDX_SCALE
    return q, kv, q_idx, k_idx, w_idx


def _index_scores(q_idx, w_idx, k_idx):
    dots = jnp.einsum('nqhd,nkd->nqhk', q_idx, k_idx, preferred_element_type=jnp.float32)
    return jnp.einsum('nqh,nqhk->nqk', w_idx, jax.nn.relu(dots))


def dsa_prompt(h, w_in, w_out):
    n, s, _ = h.shape
    pos = jnp.arange(s)
    q, kv, q_idx, k_idx, w_idx = _dsa_project(h, w_in, pos)
    top = min(DSA_TOPK, s // 4)
    bidx = jnp.arange(n)[:, None, None]

    def body(i):
        start = i * ATTN_BLOCK
        qpos = start + jnp.arange(ATTN_BLOCK)
        sc = _index_scores(lax.dynamic_slice_in_dim(q_idx, start, ATTN_BLOCK, 1),
                           lax.dynamic_slice_in_dim(w_idx, start, ATTN_BLOCK, 1), k_idx)
        sc = jnp.where((pos[None, :] <= qpos[:, None])[None], sc, NEG_INF)
        _, idx = lax.top_k(sc, top)
        sel = kv[bidx, idx]
        qi = lax.dynamic_slice_in_dim(q, start, ATTN_BLOCK, 1)
        return _attend_gathered(qi, sel[:, :, :, 0], sel[:, :, :, 1],
                                (idx <= qpos[None, :, None])[..., None])[0]
    o = _blockwise(body, s // ATTN_BLOCK)
    y = o.reshape(n, s, Q_DIM) @ w_out
    return y, kv, k_idx


def dsa_sample(h, pool_kv, pool_idx, layer, page_table, w_in, w_out, past):
    n, t, _ = h.shape
    pos = past + jnp.arange(t)
    q, kv, q_idx, k_idx, w_idx = _dsa_project(h, w_in, pos)
    total = past + t
    k_idx_all = jnp.concatenate([pool_idx[layer, page_table].reshape(n, past, IDX_DIM), k_idx], axis=1)
    sc = _index_scores(q_idx, w_idx, k_idx_all)
    sc = jnp.where((jnp.arange(total)[None, :] <= pos[:, None])[None], sc, NEG_INF)
    _, idx = lax.top_k(sc, min(DSA_TOPK, total // 4))
    bidx = jnp.arange(n)[:, None, None]
    pc = jnp.clip(idx, 0, past - 1)
    kv_past = pool_kv[layer, page_table[bidx, pc // PAGE_SIZE], pc % PAGE_SIZE]
    kv_new = kv[bidx, jnp.clip(idx - past, 0, t - 1)]
    sel = jnp.where((idx < past)[..., None, None, None], kv_past, kv_new)
    o, _ = _attend_gathered(q, sel[:, :, :, 0], sel[:, :, :, 1], (idx <= pos[None, :, None])[..., None])
    y = o.reshape(n, t, Q_DIM) @ w_out
    return y, kv, k_idx


def _swa_project(h, w_in, pos):
    n, t, _ = h.shape
    g = SWA_KV_HEADS
    z = h @ w_in
    q = rope(z[..., :Q_DIM].reshape(n, t, g, N_HEADS // g, HEAD_DIM), pos)
    kv = z[..., Q_DIM:].reshape(n, t, 2, g, HEAD_DIM)
    return q, jnp.stack([rope(kv[:, :, 0], pos), kv[:, :, 1]], axis=2)


def swa_prompt(h, w_in, sink, w_out):
    n, s, _ = h.shape
    q, kv = _swa_project(h, w_in, jnp.arange(s))
    o = _banded(q, kv[:, :, 0], kv[:, :, 1], SWA_WINDOW, sink.reshape(SWA_KV_HEADS, -1))
    return o.reshape(n, s, Q_DIM) @ w_out, _tail(kv, SWA_WINDOW)


def swa_sample(h, buf, w_in, sink, w_out, past):
    n, t, _ = h.shape
    pos = past + jnp.arange(t)
    q, kv = _swa_project(h, w_in, pos)
    keys = jnp.concatenate([buf, kv], axis=1)
    kp = past - buf.shape[1] + jnp.arange(keys.shape[1])
    diff = pos[:, None] - kp[None, :]
    o, _ = _attend_dense(q, keys[:, :, 0], keys[:, :, 1], ((diff >= 0) & (diff < SWA_WINDOW))[None],
                         sink.reshape(SWA_KV_HEADS, -1))
    return o.reshape(n, t, Q_DIM) @ w_out, _tail(keys, SWA_WINDOW)


ROW_TILE = 512
FF_CHUNK = 256
VMEM_LIMIT = 56 * 1024 * 1024


def _ffn_body(x_ref, g_ref, win_ref, wout_ref, o_ref):
    x = x_ref[...]
    xn = x * lax.rsqrt(jnp.mean(x * x, axis=-1, keepdims=True) + NORM_EPS) * g_ref[...]
    xb = xn.astype(jnp.bfloat16)
    acc = x
    for j in range(D_FF // FF_CHUNK):
        lo = j * FF_CHUNK
        gate = jnp.dot(xb, win_ref[:, lo:lo + FF_CHUNK], preferred_element_type=jnp.float32)
        up = jnp.dot(xb, win_ref[:, D_FF + lo:D_FF + lo + FF_CHUNK], preferred_element_type=jnp.float32)
        a = (gate * jax.nn.sigmoid(gate) * up).astype(jnp.bfloat16)
        acc = acc + jnp.dot(a, wout_ref[lo:lo + FF_CHUNK, :], preferred_element_type=jnp.float32)
    o_ref[...] = acc


def ffn_residual(x, g, w_in_bf16, w_out_bf16):
    rows = x.shape[0]
    return pl.pallas_call(
        _ffn_body,
        grid=(rows // ROW_TILE,),
        in_specs=[
            pl.BlockSpec((ROW_TILE, D_MODEL), lambda i: (i, 0)),
            pl.BlockSpec((1, D_MODEL), lambda i: (0, 0)),
            pl.BlockSpec((D_MODEL, 2 * D_FF), lambda i: (0, 0)),
            pl.BlockSpec((D_FF, D_MODEL), lambda i: (0, 0)),
        ],
        out_specs=pl.BlockSpec((ROW_TILE, D_MODEL), lambda i: (i, 0)),
        out_shape=jax.ShapeDtypeStruct((rows, D_MODEL), jnp.float32),
        compiler_params=pltpu.CompilerParams(
            dimension_semantics=("arbitrary",), vmem_limit_bytes=VMEM_LIMIT),
        name="ffn_residual",
    )(x, g.reshape(1, D_MODEL), w_in_bf16, w_out_bf16)


def kernel(x_prompt, x_sample, cache_nsa_kv, cache_nsa_win, cache_dil1, cache_dil2, cache_dil3,
           cache_dsa_kv, cache_dsa_idx, cache_swa, page_table,
           norm_mix, norm_ffn, norm_final, ffn_in, ffn_out,
           nsa_w_in, nsa_w_cmp_k, nsa_w_cmp_v, nsa_w_out,
           dil_w_in, dil_w_out, dsa_w_in, dsa_w_out,
           swa_w_in, swa_sink, swa_w_out):
    past = page_table.shape[1] * PAGE_SIZE
    hp, hs = x_prompt, x_sample
    ffn_in_b = ffn_in.astype(jnp.bfloat16)
    ffn_out_b = ffn_out.astype(jnp.bfloat16)
    st = {name: [] for name in ("nsa_kv_p", "nsa_kv_s", "nsa_win_p", "nsa_win_s",
                                "dil1_p", "dil1_s", "dil2_p", "dil2_s", "dil3_p", "dil3_s",
                                "dsa_kv_p", "dsa_kv_s", "dsa_idx_p", "dsa_idx_s", "swa_p", "swa_s")}
    for i in range(DEPTH):
        kind, j = i % N_MIXERS, i // N_MIXERS
        ap, as_ = rms_norm(hp, norm_mix[i]), rms_norm(hs, norm_mix[i])
        if kind == 0:
            yp, kv_p, win_p = nsa_prompt(ap, nsa_w_in[j], nsa_w_cmp_k[j], nsa_w_cmp_v[j], nsa_w_out[j])
            ys, kv_s, win_s = nsa_sample(as_, cache_nsa_kv, j, cache_nsa_win[j], page_table,
                                         nsa_w_in[j], nsa_w_cmp_k[j], nsa_w_cmp_v[j], nsa_w_out[j], past)
            st["nsa_kv_p"].append(kv_p)
            st["nsa_kv_s"].append(kv_s)
            st["nsa_win_p"].append(win_p)
            st["nsa_win_s"].append(win_s)
        elif kind == 1:
            yp, bufs_p = dil_prompt(ap, dil_w_in[j], dil_w_out[j])
            ys, bufs_s = dil_sample(as_, [cache_dil1[j], cache_dil2[j], cache_dil3[j]],
                                    dil_w_in[j], dil_w_out[j], past)
            for gi in range(len(DIL_GROUPS)):
                st["dil%d_p" % (gi + 1)].append(bufs_p[gi])
                st["dil%d_s" % (gi + 1)].append(bufs_s[gi])
        elif kind == 2:
            yp, kv_p, idx_p = dsa_prompt(ap, dsa_w_in[j], dsa_w_out[j])
            ys, kv_s, idx_s = dsa_sample(as_, cache_dsa_kv, cache_dsa_idx, j, page_table,
                                         dsa_w_in[j], dsa_w_out[j], past)
            st["dsa_kv_p"].append(kv_p)
            st["dsa_kv_s"].append(kv_s)
            st["dsa_idx_p"].append(idx_p)
            st["dsa_idx_s"].append(idx_s)
        else:
            yp, buf_p = swa_prompt(ap, swa_w_in[j], swa_sink[j], swa_w_out[j])
            ys, buf_s = swa_sample(as_, cache_swa[j], swa_w_in[j], swa_sink[j], swa_w_out[j], past)
            st["swa_p"].append(buf_p)
            st["swa_s"].append(buf_s)
        hp = hp + yp
        hs = hs + ys
        hp = ffn_residual(hp.reshape(-1, D_MODEL), norm_ffn[i], ffn_in_b[i], ffn_out_b[i]).reshape(hp.shape)
        hs = ffn_residual(hs.reshape(-1, D_MODEL), norm_ffn[i], ffn_in_b[i], ffn_out_b[i]).reshape(hs.shape)
    y_prompt = rms_norm(hp, norm_final)
    y_sample = rms_norm(hs, norm_final)
    return (y_prompt, y_sample,
            jnp.stack(st["nsa_kv_p"]), jnp.stack(st["nsa_kv_s"]),
            jnp.stack(st["nsa_win_p"]), jnp.stack(st["nsa_win_s"]),
            jnp.stack(st["dil1_p"]), jnp.stack(st["dil1_s"]),
            jnp.stack(st["dil2_p"]), jnp.stack(st["dil2_s"]),
            jnp.stack(st["dil3_p"]), jnp.stack(st["dil3_s"]),
            jnp.stack(st["dsa_kv_p"]), jnp.stack(st["dsa_kv_s"]),
            jnp.stack(st["dsa_idx_p"]), jnp.stack(st["dsa_idx_s"]),
            jnp.stack(st["swa_p"]), jnp.stack(st["swa_s"]))
```

```python
import functools

import jax, jax.numpy as jnp
from jax import lax
import numpy as np
from jax.experimental import pallas as pl
from jax.experimental.pallas import tpu as pltpu

D_MODEL = 1024
BATCH = 2
SEQ = 8192
DEPTH = 4
DEC_BATCH = 128
DEC_SEQ = 4
PAST_LEN = 8192
PAGE_SIZE = 128

HEAD_DIM = 64
N_HEADS = D_MODEL // HEAD_DIM
Q_DIM = N_HEADS * HEAD_DIM
ROPE_THETA = 10000.0
NORM_EPS = 1e-6
ATTN_BLOCK = 128
N_MIXERS = 4
D_FF = -(-(8 * D_MODEL) // (3 * 256)) * 256
NEG_INF = -1e30

NSA_KV_HEADS = 2
NSA_CMP_BLOCK = 32
NSA_SEL_BLOCK = 64
NSA_N_SEL = 16
NSA_WINDOW = 512
NSA_FORCE_BONUS = 1e4
NSA_IN = Q_DIM + 6 * NSA_KV_HEADS * HEAD_DIM + 3 * N_HEADS

DIL_KV_HEADS = 4
DIL_GROUPS = ((128, 1), (512, 4), (2048, 16))
DIL_IN = Q_DIM + len(DIL_GROUPS) * 2 * DIL_KV_HEADS * HEAD_DIM

DSA_KV_HEADS = 4
IDX_HEADS = 8
IDX_DIM = 64
DSA_TOPK = 256
IDX_SCALE = (IDX_DIM * IDX_HEADS) ** -0.5
DSA_IN = Q_DIM + 2 * DSA_KV_HEADS * HEAD_DIM + IDX_HEADS * IDX_DIM + IDX_DIM + IDX_HEADS

SWA_KV_HEADS = 2
SWA_WINDOW = 128
SWA_IN = Q_DIM + 2 * SWA_KV_HEADS * HEAD_DIM

N_NSA_LAYERS = (DEPTH + 3) // 4
N_DIL_LAYERS = (DEPTH + 2) // 4
N_DSA_LAYERS = (DEPTH + 1) // 4
N_SWA_LAYERS = DEPTH // 4


def rms_norm(x, g):
    xf = x.astype(jnp.float32)
    y = xf * lax.rsqrt(jnp.mean(xf * xf, axis=-1, keepdims=True) + NORM_EPS)
    return (y * g.astype(jnp.float32)).astype(x.dtype)


def rope(x, pos):
    half = x.shape[-1] // 2
    inv = ROPE_THETA ** (-jnp.arange(half, dtype=jnp.float32) / half)
    ang = pos.astype(jnp.float32)[:, None] * inv[None, :]
    shape = (pos.shape[0],) + (1,) * (x.ndim - 3) + (half,)
    cos, sin = jnp.cos(ang).reshape(shape), jnp.sin(ang).reshape(shape)
    xf = x.astype(jnp.float32)
    x1, x2 = xf[..., :half], xf[..., half:]
    return jnp.concatenate([x1 * cos - x2 * sin, x2 * cos + x1 * sin], axis=-1).astype(x.dtype)


def _tail(rows, window):
    n = rows.shape[1]
    return rows[:, n - min(window, n):]


def _masked_softmax(s, mask, sink=None):
    s = jnp.where(mask, s, NEG_INF)
    m = jnp.max(s, axis=-1, keepdims=True)
    if sink is not None:
        m = jnp.maximum(m, sink)
    e = jnp.where(mask, jnp.exp(s - m), 0.0)
    den = jnp.sum(e, axis=-1, keepdims=True)
    if sink is not None:
        den = den + jnp.exp(sink - m)
    den = jnp.maximum(den, 1e-30)
    return e / den, (m + jnp.log(den))[..., 0]


def _attend_dense(q, k, v, mask, sink=None):
    s = jnp.einsum('nqgrd,nkgd->ngrqk', q, k, preferred_element_type=jnp.float32) * (q.shape[-1] ** -0.5)
    sk = None if sink is None else sink.astype(jnp.float32)[None, :, :, None, None]
    p, lse = _masked_softmax(s, mask[:, None, None], sk)
    o = jnp.einsum('ngrqk,nkgd->nqgrd', p.astype(v.dtype), v)
    return o, jnp.transpose(lse, (0, 3, 1, 2))


def _attend_gathered(q, k, v, mask):
    s = jnp.einsum('nqgrd,nqkgd->nqgrk', q, k, preferred_element_type=jnp.float32) * (q.shape[-1] ** -0.5)
    p, lse = _masked_softmax(s, jnp.swapaxes(mask, 2, 3)[:, :, :, None, :])
    o = jnp.einsum('nqgrk,nqkgd->nqgrd', p.astype(v.dtype), v)
    return o, lse


def _blockwise(fn, n_blocks):
    out = lax.map(fn, jnp.arange(n_blocks))
    return jax.tree_util.tree_map(
        lambda a: jnp.moveaxis(a, 0, 1).reshape((a.shape[1], -1) + a.shape[3:]), out)


def _banded(q, k, v, window, sink=None):
    pad = ((0, 0), (window, 0), (0, 0), (0, 0))
    kp, vp = jnp.pad(k, pad), jnp.pad(v, pad)
    span = window + ATTN_BLOCK

    def body(i):
        start = i * ATTN_BLOCK
        qi = lax.dynamic_slice_in_dim(q, start, ATTN_BLOCK, 1)
        ki = lax.dynamic_slice_in_dim(kp, start, span, 1)
        vi = lax.dynamic_slice_in_dim(vp, start, span, 1)
        qpos = start + jnp.arange(ATTN_BLOCK)
        kpos = start - window + jnp.arange(span)
        diff = qpos[:, None] - kpos[None, :]
        mask = (diff >= 0) & (diff < window) & (kpos >= 0)[None, :]
        return _attend_dense(qi, ki, vi, mask[None], sink)[0]
    return _blockwise(body, q.shape[1] // ATTN_BLOCK)


def _nsa_project(h, w_in, pos):
    n, t, _ = h.shape
    g, r = NSA_KV_HEADS, N_HEADS // NSA_KV_HEADS
    z = h @ w_in
    kv_end = Q_DIM + 6 * g * HEAD_DIM
    q = rope(z[..., :Q_DIM].reshape(n, t, g, r, HEAD_DIM), pos)
    kv = z[..., Q_DIM:kv_end].reshape(n, t, 3, 2, g, HEAD_DIM)
    kv = jnp.stack([rope(kv[:, :, :, 0], pos), kv[:, :, :, 1]], axis=3).reshape(n, t, 6, g, HEAD_DIM)
    gates = jax.nn.sigmoid(z[..., kv_end:].astype(jnp.float32)).reshape(n, t, g, r, 3)
    return q, kv, gates


def _nsa_compress(rows, w_ck, w_cv):
    n, length, _, g, dh = rows.shape
    nb = length // NSA_CMP_BLOCK
    r = rows[:, :nb * NSA_CMP_BLOCK].reshape(n, nb, NSA_CMP_BLOCK, 2, g, dh)
    ck = jnp.einsum('nbcgd,gcde->nbge', r[:, :, :, 0], w_ck)
    cv = jnp.einsum('nbcgd,gcde->nbge', r[:, :, :, 1], w_cv)
    return ck, cv


def _nsa_cmp_attend(q, ck, cv, qpos):
    blk_end = (jnp.arange(ck.shape[1]) + 1) * NSA_CMP_BLOCK - 1
    mask = (blk_end[None, :] <= qpos[:, None])[None, :, None, None, :]
    s = jnp.einsum('nqgrd,ncgd->nqgrc', q, ck, preferred_element_type=jnp.float32) * (q.shape[-1] ** -0.5)
    p, _ = _masked_softmax(s, mask)
    o = jnp.einsum('nqgrc,ncgd->nqgrd', p.astype(cv.dtype), cv)
    return o, p


def _nsa_select(p_cmp, qpos, n_slc):
    n, t, g = p_cmp.shape[:3]
    ratio = NSA_SEL_BLOCK // NSA_CMP_BLOCK
    imp = p_cmp.sum(axis=3)
    imp = jnp.pad(imp, ((0, 0), (0, 0), (0, 0), (0, n_slc * ratio - imp.shape[-1])))
    imp = imp.reshape(n, t, g, n_slc, ratio).sum(-1)
    j = jnp.arange(n_slc)[None, :]
    cur = (qpos // NSA_SEL_BLOCK)[:, None]
    forced = ((j == 0) | (j == cur) | (j == cur - 1))[None, :, None, :]
    future = (j > cur)[None, :, None, :]
    score = jnp.where(future, NEG_INF, jnp.where(forced, imp + NSA_FORCE_BONUS, imp))
    return lax.top_k(score, min(NSA_N_SEL, n_slc))[1]


def _sel_positions(sel):
    n, t, g, ns = sel.shape
    pos = sel[..., None] * NSA_SEL_BLOCK + jnp.arange(NSA_SEL_BLOCK)
    return jnp.swapaxes(pos.reshape(n, t, g, ns * NSA_SEL_BLOCK), 2, 3)


def _nsa_gate(gates, o_cmp, o_slc, o_win, w_out, dtype):
    f = jnp.float32
    o = (gates[..., 0:1] * o_cmp.astype(f) + gates[..., 1:2] * o_slc.astype(f)
         + gates[..., 2:3] * o_win.astype(f))
    n, t = o.shape[:2]
    return o.astype(dtype).reshape(n, t, Q_DIM) @ w_out


def nsa_prompt(h, w_in, w_ck, w_cv, w_out):
    n, s, _ = h.shape
    pos = jnp.arange(s)
    q, kv, gates = _nsa_project(h, w_in, pos)
    ck, cv = _nsa_compress(kv[:, :, 0:2], w_ck, w_cv)
    o_cmp, p_cmp = _nsa_cmp_attend(q, ck, cv, pos)
    sel = _nsa_select(p_cmp, pos, -(-s // NSA_SEL_BLOCK))
    slc_k, slc_v = kv[:, :, 2], kv[:, :, 3]
    bidx = jnp.arange(n)[:, None, None, None]
    gidx = jnp.arange(NSA_KV_HEADS)[None, None, None, :]

    def body(i):
        start = i * ATTN_BLOCK
        qpos = start + jnp.arange(ATTN_BLOCK)
        kpos = _sel_positions(lax.dynamic_slice_in_dim(sel, start, ATTN_BLOCK, 1))
        qi = lax.dynamic_slice_in_dim(q, start, ATTN_BLOCK, 1)
        return _attend_gathered(qi, slc_k[bidx, kpos, gidx], slc_v[bidx, kpos, gidx],
                                kpos <= qpos[None, :, None, None])[0]
    o_slc = _blockwise(body, s // ATTN_BLOCK)
    o_win = _banded(q, kv[:, :, 4], kv[:, :, 5], NSA_WINDOW)
    y = _nsa_gate(gates, o_cmp, o_slc, o_win, w_out, h.dtype)
    return y, kv[:, :, 0:4], _tail(kv[:, :, 4:6], NSA_WINDOW)


def nsa_sample(h, pool, layer, win_buf, page_table, w_in, w_ck, w_cv, w_out, past):
    n, t, _ = h.shape
    pos = past + jnp.arange(t)
    q, kv, gates = _nsa_project(h, w_in, pos)
    past_rows = pool[layer, page_table, :, 0:2].reshape(n, past, 2, NSA_KV_HEADS, HEAD_DIM)
    ck_p, cv_p = _nsa_compress(past_rows, w_ck, w_cv)
    ck_n, cv_n = _nsa_compress(kv[:, :, 0:2], w_ck, w_cv)
    o_cmp, p_cmp = _nsa_cmp_attend(q, jnp.concatenate([ck_p, ck_n], 1),
                                   jnp.concatenate([cv_p, cv_n], 1), pos)
    sel = _nsa_select(p_cmp, pos, -(-(past + t) // NSA_SEL_BLOCK))
    kpos = _sel_positions(sel)
    bidx = jnp.arange(n)[:, None, None, None]
    gidx = jnp.arange(NSA_KV_HEADS)[None, None, None, :]
    pc = jnp.clip(kpos, 0, past - 1)
    phys = page_table[bidx, pc // PAGE_SIZE]
    off = pc % PAGE_SIZE
    nc = jnp.clip(kpos - past, 0, t - 1)
    in_past = (kpos < past)[..., None]
    k = jnp.where(in_past, pool[layer, phys, off, 2, gidx], kv[bidx, nc, 2, gidx])
    v = jnp.where(in_past, pool[layer, phys, off, 3, gidx], kv[bidx, nc, 3, gidx])
    o_slc, _ = _attend_gathered(q, k, v, kpos <= pos[None, :, None, None])
    keys = jnp.concatenate([win_buf, kv[:, :, 4:6]], axis=1)
    kp = past - win_buf.shape[1] + jnp.arange(keys.shape[1])
    diff = pos[:, None] - kp[None, :]
    o_win, _ = _attend_dense(q, keys[:, :, 0], keys[:, :, 1], ((diff >= 0) & (diff < NSA_WINDOW))[None])
    y = _nsa_gate(gates, o_cmp, o_slc, o_win, w_out, h.dtype)
    return y, kv[:, :, 0:4], _tail(keys, NSA_WINDOW)


def _dil_project(h, w_in, pos):
    n, t, _ = h.shape
    g = DIL_KV_HEADS
    z = h @ w_in
    q = rope(z[..., :Q_DIM].reshape(n, t, g, N_HEADS // g, HEAD_DIM), pos)
    kv = z[..., Q_DIM:].reshape(n, t, len(DIL_GROUPS), 2, g, HEAD_DIM)
    return q, jnp.stack([rope(kv[:, :, :, 0], pos), kv[:, :, :, 1]], axis=3)


def _dil_mix(outs, lses):
    w = jax.nn.softmax(jnp.stack(lses), axis=0)
    return jnp.sum(w[..., None] * jnp.stack(outs).astype(jnp.float32), axis=0)


def _dil_group_prompt(q, k, v, dil, n_keys):
    dist = jnp.arange(n_keys) * dil

    def body(i):
        start = i * ATTN_BLOCK
        qi = lax.dynamic_slice_in_dim(q, start, ATTN_BLOCK, 1)
        kpos = (start + jnp.arange(ATTN_BLOCK))[:, None] - dist[None, :]
        kc = jnp.maximum(kpos, 0)
        return _attend_gathered(qi, k[:, kc], v[:, kc], (kpos >= 0)[None, :, :, None])
    return _blockwise(body, q.shape[1] // ATTN_BLOCK)


def dil_prompt(h, w_in, w_out):
    n, s, _ = h.shape
    q, kv = _dil_project(h, w_in, jnp.arange(s))
    outs, lses, bufs = [], [], []
    for gi, (win, dil) in enumerate(DIL_GROUPS):
        o, lse = _dil_group_prompt(q, kv[:, :, gi, 0], kv[:, :, gi, 1], dil, win // dil + 1)
        outs.append(o)
        lses.append(lse)
        bufs.append(_tail(kv[:, :, gi], win))
    y = _dil_mix(outs, lses).astype(h.dtype).reshape(n, s, Q_DIM) @ w_out
    return y, bufs


def dil_sample(h, bufs, w_in, w_out, past):
    n, t, _ = h.shape
    pos = past + jnp.arange(t)
    q, kv = _dil_project(h, w_in, pos)
    outs, lses, new_bufs = [], [], []
    for gi, ((win, dil), buf) in enumerate(zip(DIL_GROUPS, bufs)):
        keys = jnp.concatenate([buf, kv[:, :, gi]], axis=1)
        base = past - buf.shape[1]
        local = pos[:, None] - (jnp.arange(win // dil + 1) * dil)[None, :] - base
        lc = jnp.maximum(local, 0)
        o, lse = _attend_gathered(q, keys[:, lc, 0], keys[:, lc, 1], (local >= 0)[None, :, :, None])
        outs.append(o)
        lses.append(lse)
        new_bufs.append(_tail(keys, win))
    y = _dil_mix(outs, lses).astype(h.dtype).reshape(n, t, Q_DIM) @ w_out
    return y, new_bufs


def _dsa_project(h, w_in, pos):
    n, t, _ = h.shape
    g = DSA_KV_HEADS
    z = h @ w_in
    o1 = Q_DIM
    o2 = o1 + 2 * g * HEAD_DIM
    o3 = o2 + IDX_HEADS * IDX_DIM
    o4 = o3 + IDX_DIM
    q = rope(z[..., :o1].reshape(n, t, g, N_HEADS // g, HEAD_DIM), pos)
    kv = z[..., o1:o2].reshape(n, t, 2, g, HEAD_DIM)
    kv = jnp.stack([rope(kv[:, :, 0], pos), kv[:, :, 1]], axis=2)
    q_idx = rope(z[..., o2:o3].reshape(n, t, IDX_HEADS, IDX_DIM), pos)
    k_idx = rope(z[..., o3:o4], pos)
    w_idx = z[..., o4:].astype(jnp.float32) * IDX_SCALE
    return q, kv, q_idx, k_idx, w_idx


def _index_scores(q_idx, w_idx, k_idx):
    dots = jnp.einsum('nqhd,nkd->nqhk', q_idx, k_idx, preferred_element_type=jnp.float32)
    return jnp.einsum('nqh,nqhk->nqk', w_idx, jax.nn.relu(dots))


def dsa_prompt(h, w_in, w_out):
    n, s, _ = h.shape
    pos = jnp.arange(s)
    q, kv, q_idx, k_idx, w_idx = _dsa_project(h, w_in, pos)
    top = min(DSA_TOPK, s // 4)
    bidx = jnp.arange(n)[:, None, None]

    def body(i):
        start = i * ATTN_BLOCK
        qpos = start + jnp.arange(ATTN_BLOCK)
        sc = _index_scores(lax.dynamic_slice_in_dim(q_idx, start, ATTN_BLOCK, 1),
                           lax.dynamic_slice_in_dim(w_idx, start, ATTN_BLOCK, 1), k_idx)
        sc = jnp.where((pos[None, :] <= qpos[:, None])[None], sc, NEG_INF)
        _, idx = lax.top_k(sc, top)
        sel = kv[bidx, idx]
        qi = lax.dynamic_slice_in_dim(q, start, ATTN_BLOCK, 1)
        return _attend_gathered(qi, sel[:, :, :, 0], sel[:, :, :, 1],
                                (idx <= qpos[None, :, None])[..., None])[0]
    o = _blockwise(body, s // ATTN_BLOCK)
    y = o.reshape(n, s, Q_DIM) @ w_out
    return y, kv, k_idx


def dsa_sample(h, pool_kv, pool_idx, layer, page_table, w_in, w_out, past):
    n, t, _ = h.shape
    pos = past + jnp.arange(t)
    q, kv, q_idx, k_idx, w_idx = _dsa_project(h, w_in, pos)
    total = past + t
    k_idx_all = jnp.concatenate([pool_idx[layer, page_table].reshape(n, past, IDX_DIM), k_idx], axis=1)
    sc = _index_scores(q_idx, w_idx, k_idx_all)
    sc = jnp.where((jnp.arange(total)[None, :] <= pos[:, None])[None], sc, NEG_INF)
    _, idx = lax.top_k(sc, min(DSA_TOPK, total // 4))
    bidx = jnp.arange(n)[:, None, None]
    pc = jnp.clip(idx, 0, past - 1)
    kv_past = pool_kv[layer, page_table[bidx, pc // PAGE_SIZE], pc % PAGE_SIZE]
    kv_new = kv[bidx, jnp.clip(idx - past, 0, t - 1)]
    sel = jnp.where((idx < past)[..., None, None, None], kv_past, kv_new)
    o, _ = _attend_gathered(q, sel[:, :, :, 0], sel[:, :, :, 1], (idx <= pos[None, :, None])[..., None])
    y = o.reshape(n, t, Q_DIM) @ w_out
    return y, kv, k_idx


def _swa_project(h, w_in, pos):
    n, t, _ = h.shape
    g = SWA_KV_HEADS
    z = h @ w_in
    q = rope(z[..., :Q_DIM].reshape(n, t, g, N_HEADS // g, HEAD_DIM), pos)
    kv = z[..., Q_DIM:].reshape(n, t, 2, g, HEAD_DIM)
    return q, jnp.stack([rope(kv[:, :, 0], pos), kv[:, :, 1]], axis=2)


def swa_prompt(h, w_in, sink, w_out):
    n, s, _ = h.shape
    q, kv = _swa_project(h, w_in, jnp.arange(s))
    o = _banded(q, kv[:, :, 0], kv[:, :, 1], SWA_WINDOW, sink.reshape(SWA_KV_HEADS, -1))
    return o.reshape(n, s, Q_DIM) @ w_out, _tail(kv, SWA_WINDOW)


def swa_sample(h, buf, w_in, sink, w_out, past):
    n, t, _ = h.shape
    pos = past + jnp.arange(t)
    q, kv = _swa_project(h, w_in, pos)
    keys = jnp.concatenate([buf, kv], axis=1)
    kp = past - buf.shape[1] + jnp.arange(keys.shape[1])
    diff = pos[:, None] - kp[None, :]
    o, _ = _attend_dense(q, keys[:, :, 0], keys[:, :, 1], ((diff >= 0) & (diff < SWA_WINDOW))[None],
                         sink.reshape(SWA_KV_HEADS, -1))
    return o.reshape(n, t, Q_DIM) @ w_out, _tail(keys, SWA_WINDOW)


MXU_DTYPE = jnp.bfloat16
BIAS_DTYPE = jnp.bfloat16
ROW_TILE = 512
FF_CHUNK = 256
VMEM_LIMIT = 56 * 1024 * 1024


def _ffn_body(x_ref, g_ref, win_ref, wout_ref, o_ref):
    x = x_ref[...]
    xn = x * lax.rsqrt(jnp.mean(x * x, axis=-1, keepdims=True) + NORM_EPS) * g_ref[...]
    xb = xn.astype(MXU_DTYPE)
    acc = x
    for j in range(D_FF // FF_CHUNK):
        lo = j * FF_CHUNK
        gate = jnp.dot(xb, win_ref[:, lo:lo + FF_CHUNK], preferred_element_type=jnp.float32)
        up = jnp.dot(xb, win_ref[:, D_FF + lo:D_FF + lo + FF_CHUNK], preferred_element_type=jnp.float32)
        a = (gate * jax.nn.sigmoid(gate) * up).astype(MXU_DTYPE)
        acc = acc + jnp.dot(a, wout_ref[lo:lo + FF_CHUNK, :], preferred_element_type=jnp.float32)
    o_ref[...] = acc


def ffn_residual(x, g, w_in_bf16, w_out_bf16):
    rows = x.shape[0]
    return pl.pallas_call(
        _ffn_body,
        grid=(rows // ROW_TILE,),
        in_specs=[
            pl.BlockSpec((ROW_TILE, D_MODEL), lambda i: (i, 0)),
            pl.BlockSpec((1, D_MODEL), lambda i: (0, 0)),
            pl.BlockSpec((D_MODEL, 2 * D_FF), lambda i: (0, 0)),
            pl.BlockSpec((D_FF, D_MODEL), lambda i: (0, 0)),
        ],
        out_specs=pl.BlockSpec((ROW_TILE, D_MODEL), lambda i: (i, 0)),
        out_shape=jax.ShapeDtypeStruct((rows, D_MODEL), jnp.float32),
        compiler_params=pltpu.CompilerParams(
            dimension_semantics=("arbitrary",), vmem_limit_bytes=VMEM_LIMIT),
        name="ffn_residual",
    )(x, g.reshape(1, D_MODEL), w_in_bf16, w_out_bf16)


Q_TILE = 128
K_TILE = 512
_AT_BT = (((1,), (1,)), ((), ()))
INT32_MIN = -2 ** 31


def _causal_tiles(qi, tq, tk):
    return (qi * tq + tq + tk - 1) // tk


def _mflash_body(q_ref, k_ref, v_ref, b_ref, o_ref, m_ref, l_ref, acc_ref, *, r, tq, tk):
    qi = pl.program_id(2)
    q = q_ref[0, 0].reshape(r * tq, HEAD_DIM)
    m_ref[...] = jnp.full(m_ref.shape, NEG_INF, jnp.float32)
    l_ref[...] = jnp.zeros(l_ref.shape, jnp.float32)
    acc_ref[...] = jnp.zeros(acc_ref.shape, jnp.float32)

    def step(kt, carry):
        ks = pl.multiple_of(kt * tk, tk)
        k = k_ref[0, 0, pl.ds(ks, tk), :]
        v = v_ref[0, 0, pl.ds(ks, tk), :]
        s = lax.dot_general(q, k, _AT_BT, preferred_element_type=jnp.float32)
        bias = b_ref[0, 0, kt].astype(jnp.float32)
        s = (s.reshape(r, tq, tk) + bias[None]).reshape(r * tq, tk)
        m_old = m_ref[...]
        m_new = jnp.maximum(m_old, jnp.max(s, axis=1, keepdims=True))
        p = jnp.exp(s - m_new)
        alpha = jnp.exp(m_old - m_new)
        l_ref[...] = alpha * l_ref[...] + jnp.sum(p, axis=1, keepdims=True)
        acc_ref[...] = alpha * acc_ref[...] + jnp.dot(p.astype(MXU_DTYPE), v,
                                                      preferred_element_type=jnp.float32)
        m_ref[...] = m_new
        return carry

    lax.fori_loop(0, _causal_tiles(qi, tq, tk), step, 0)
    o = acc_ref[...] / jnp.maximum(l_ref[...], 1e-30)
    o_ref[0, 0] = o.reshape(r, tq, HEAD_DIM)


def masked_flash(q, k, v, bias):
    n, g, r, t, dh = q.shape
    gb = bias.shape[1]
    nkt = t // K_TILE
    bmap = (lambda a, b, c: (a, b, 0, c, 0)) if gb == g else (lambda a, b, c: (a, 0, 0, c, 0))
    return pl.pallas_call(
        functools.partial(_mflash_body, r=r, tq=Q_TILE, tk=K_TILE),
        grid=(n, g, t // Q_TILE),
        in_specs=[
            pl.BlockSpec((1, 1, r, Q_TILE, dh), lambda a, b, c: (a, b, 0, c, 0)),
            pl.BlockSpec((1, 1, t, dh), lambda a, b, c: (a, b, 0, 0)),
            pl.BlockSpec((1, 1, t, dh), lambda a, b, c: (a, b, 0, 0)),
            pl.BlockSpec((1, 1, nkt, Q_TILE, K_TILE), bmap),
        ],
        out_specs=pl.BlockSpec((1, 1, r, Q_TILE, dh), lambda a, b, c: (a, b, 0, c, 0)),
        out_shape=jax.ShapeDtypeStruct((n, g, r, t, dh), jnp.float32),
        scratch_shapes=[pltpu.VMEM((r * Q_TILE, 1), jnp.float32),
                        pltpu.VMEM((r * Q_TILE, 1), jnp.float32),
                        pltpu.VMEM((r * Q_TILE, dh), jnp.float32)],
        compiler_params=pltpu.CompilerParams(
            dimension_semantics=("arbitrary", "arbitrary", "arbitrary"), vmem_limit_bytes=VMEM_LIMIT),
        name="masked_flash",
    )(q, k, v, bias)


def _dsa_select_body(qx_ref, kx_ref, w_ref, o_ref, key_ref, *, tq, tk, top):
    qi = pl.program_id(1)
    nkt = key_ref.shape[0]
    n_valid = _causal_tiles(qi, tq, tk)
    qx = qx_ref[0].reshape(IDX_HEADS * tq, IDX_DIM)
    w = w_ref[0]
    row = qi * tq + lax.broadcasted_iota(jnp.int32, (tq, 1), 0)

    def causal(kt):
        col = kt * tk + lax.broadcasted_iota(jnp.int32, (1, tk), 1)
        return col <= row

    def score_step(kt, carry):
        ks = pl.multiple_of(kt * tk, tk)
        d = lax.dot_general(qx, kx_ref[0, pl.ds(ks, tk), :], _AT_BT, preferred_element_type=jnp.float32)
        d = jnp.maximum(d, 0.0).reshape(IDX_HEADS, tq, tk)
        sc = w[:, 0:1] * d[0]
        for h in range(1, IDX_HEADS):
            sc = sc + w[:, h:h + 1] * d[h]
        sc = jnp.where(causal(kt), sc, NEG_INF)
        bits = pltpu.bitcast(sc, jnp.int32)
        key = jnp.where(bits >= 0, bits, bits ^ 0x7FFFFFFF)
        key_ref[kt] = jnp.where(key == -1, 0, key)
        return carry

    lax.fori_loop(0, n_valid, score_step, 0)

    def count(pred):
        def body(kt, c):
            return c + jnp.sum(jnp.where(pred(key_ref[kt]), 1.0, 0.0), axis=1, keepdims=True)
        return lax.fori_loop(0, n_valid, body, jnp.zeros((tq, 1), jnp.float32))

    c0 = count(lambda kk: kk >= 0)
    tau = jnp.where(c0 >= top, 0, INT32_MIN).astype(jnp.int32)

    def bit_step(i, tau):
        cand = tau | jnp.left_shift(jnp.int32(1), 30 - i)
        c = count(lambda kk: kk >= cand)
        return jnp.where(c >= top, cand, tau)

    tau = lax.fori_loop(0, 31, bit_step, tau)
    need = top - count(lambda kk: kk > tau)
    ri = lax.broadcasted_iota(jnp.int32, (tk, tk), 0)
    ci = lax.broadcasted_iota(jnp.int32, (tk, tk), 1)
    tri = jnp.where(ri <= ci, 1.0, 0.0).astype(MXU_DTYPE)

    def out_step(kt, seen):
        kk = key_ref[kt]
        tie = kk == tau
        tie_f = jnp.where(tie, 1.0, 0.0)
        rank = seen + jnp.dot(tie_f.astype(MXU_DTYPE), tri, preferred_element_type=jnp.float32)
        sel = (kk > tau) | (tie & (rank <= need))
        o_ref[0, kt] = jnp.where(sel & causal(kt), 0.0, NEG_INF).astype(o_ref.dtype)
        return seen + jnp.sum(tie_f, axis=1, keepdims=True)

    lax.fori_loop(0, n_valid, out_step, jnp.zeros((tq, 1), jnp.float32))

    def fill_step(kt, carry):
        o_ref[0, kt] = jnp.full((tq, tk), NEG_INF, o_ref.dtype)
        return carry

    lax.fori_loop(n_valid, nkt, fill_step, 0)


def dsa_select_bias(q_idx, k_idx, w_idx, top):
    n, h, t, di = q_idx.shape
    nkt = t // K_TILE
    return pl.pallas_call(
        functools.partial(_dsa_select_body, tq=Q_TILE, tk=K_TILE, top=top),
        grid=(n, t // Q_TILE),
        in_specs=[
            pl.BlockSpec((1, h, Q_TILE, di), lambda a, c: (a, 0, c, 0)),
            pl.BlockSpec((1, t, di), lambda a, c: (a, 0, 0)),
            pl.BlockSpec((1, Q_TILE, h), lambda a, c: (a, c, 0)),
        ],
        out_specs=pl.BlockSpec((1, nkt, Q_TILE, K_TILE), lambda a, c: (a, 0, c, 0)),
        out_shape=jax.ShapeDtypeStruct((n, nkt, t, K_TILE), BIAS_DTYPE),
        scratch_shapes=[pltpu.VMEM((nkt, Q_TILE, K_TILE), jnp.int32)],
        compiler_params=pltpu.CompilerParams(
            dimension_semantics=("arbitrary", "arbitrary"), vmem_limit_bytes=VMEM_LIMIT),
        name="dsa_select",
    )(q_idx, k_idx, w_idx)


def _heads_major(x):
    return jnp.transpose(x, (0, 2, 3, 1, 4))


def _attn_scaled_q(q):
    return _heads_major(q * (HEAD_DIM ** -0.5)).astype(MXU_DTYPE)


def nsa_prompt_p(h, w_in, w_ck, w_cv, w_out):
    n, s, _ = h.shape
    pos = jnp.arange(s)
    q, kv, gates = _nsa_project(h, w_in, pos)
    ck, cv = _nsa_compress(kv[:, :, 0:2], w_ck, w_cv)
    o_cmp, p_cmp = _nsa_cmp_attend(q, ck, cv, pos)
    n_slc = -(-s // NSA_SEL_BLOCK)
    sel = _nsa_select(p_cmp, pos, n_slc)
    blk = (sel[..., None] == jnp.arange(n_slc)).any(axis=3)
    blk = jnp.transpose(blk, (0, 2, 1, 3))
    nkt = s // K_TILE
    per_tile = K_TILE // NSA_SEL_BLOCK
    kpos = jnp.arange(s).reshape(nkt, 1, K_TILE)
    allow = (jnp.repeat(blk.reshape(n, NSA_KV_HEADS, s, nkt, per_tile), NSA_SEL_BLOCK, axis=-1)
             .transpose(0, 1, 3, 2, 4)) & (kpos <= pos[None, :, None])
    bias = jnp.where(allow, 0.0, NEG_INF).astype(BIAS_DTYPE)
    kb = jnp.transpose(kv[:, :, 2], (0, 2, 1, 3)).astype(MXU_DTYPE)
    vb = jnp.transpose(kv[:, :, 3], (0, 2, 1, 3)).astype(MXU_DTYPE)
    o_slc = jnp.transpose(masked_flash(_attn_scaled_q(q), kb, vb, bias), (0, 3, 1, 2, 4))
    o_win = _banded(q, kv[:, :, 4], kv[:, :, 5], NSA_WINDOW)
    y = _nsa_gate(gates, o_cmp, o_slc, o_win, w_out, h.dtype)
    return y, kv[:, :, 0:4], _tail(kv[:, :, 4:6], NSA_WINDOW)


def dsa_prompt_p(h, w_in, w_out):
    n, s, _ = h.shape
    pos = jnp.arange(s)
    q, kv, q_idx, k_idx, w_idx = _dsa_project(h, w_in, pos)
    top = min(DSA_TOPK, s // 4)
    bias = dsa_select_bias(jnp.transpose(q_idx, (0, 2, 1, 3)).astype(MXU_DTYPE),
                           k_idx.astype(MXU_DTYPE), w_idx, top)
    kb = jnp.transpose(kv[:, :, 0], (0, 2, 1, 3)).astype(MXU_DTYPE)
    vb = jnp.transpose(kv[:, :, 1], (0, 2, 1, 3)).astype(MXU_DTYPE)
    o = masked_flash(_attn_scaled_q(q), kb, vb, bias[:, None])
    o = jnp.transpose(o, (0, 3, 1, 2, 4))
    y = o.reshape(n, s, Q_DIM) @ w_out
    return y, kv, k_idx


def kernel(x_prompt, x_sample, cache_nsa_kv, cache_nsa_win, cache_dil1, cache_dil2, cache_dil3,
           cache_dsa_kv, cache_dsa_idx, cache_swa, page_table,
           norm_mix, norm_ffn, norm_final, ffn_in, ffn_out,
           nsa_w_in, nsa_w_cmp_k, nsa_w_cmp_v, nsa_w_out,
           dil_w_in, dil_w_out, dsa_w_in, dsa_w_out,
           swa_w_in, swa_sink, swa_w_out):
    past = page_table.shape[1] * PAGE_SIZE
    hp, hs = x_prompt, x_sample
    ffn_in_b = ffn_in.astype(MXU_DTYPE)
    ffn_out_b = ffn_out.astype(MXU_DTYPE)
    st = {name: [] for name in ("nsa_kv_p", "nsa_kv_s", "nsa_win_p", "nsa_win_s",
                                "dil1_p", "dil1_s", "dil2_p", "dil2_s", "dil3_p", "dil3_s",
                                "dsa_kv_p", "dsa_kv_s", "dsa_idx_p", "dsa_idx_s", "swa_p", "swa_s")}
    for i in range(DEPTH):
        kind, j = i % N_MIXERS, i // N_MIXERS
        ap, as_ = rms_norm(hp, norm_mix[i]), rms_norm(hs, norm_mix[i])
        if kind == 0:
            yp, kv_p, win_p = nsa_prompt_p(ap, nsa_w_in[j], nsa_w_cmp_k[j], nsa_w_cmp_v[j], nsa_w_out[j])
            ys, kv_s, win_s = nsa_sample(as_, cache_nsa_kv, j, cache_nsa_win[j], page_table,
                                         nsa_w_in[j], nsa_w_cmp_k[j], nsa_w_cmp_v[j], nsa_w_out[j], past)
            st["nsa_kv_p"].append(kv_p)
            st["nsa_kv_s"].append(kv_s)
            st["nsa_win_p"].append(win_p)
            st["nsa_win_s"].append(win_s)
        elif kind == 1:
            yp, bufs_p = dil_prompt(ap, dil_w_in[j], dil_w_out[j])
            ys, bufs_s = dil_sample(as_, [cache_dil1[j], cache_dil2[j], cache_dil3[j]],
                                    dil_w_in[j], dil_w_out[j], past)
            for gi in range(len(DIL_GROUPS)):
                st["dil%d_p" % (gi + 1)].append(bufs_p[gi])
                st["dil%d_s" % (gi + 1)].append(bufs_s[gi])
        elif kind == 2:
            yp, kv_p, idx_p = dsa_prompt_p(ap, dsa_w_in[j], dsa_w_out[j])
            ys, kv_s, idx_s = dsa_sample(as_, cache_dsa_kv, cache_dsa_idx, j, page_table,
                                         dsa_w_in[j], dsa_w_out[j], past)
            st["dsa_kv_p"].append(kv_p)
            st["dsa_kv_s"].append(kv_s)
            st["dsa_idx_p"].append(idx_p)
            st["dsa_idx_s"].append(idx_s)
        else:
            yp, buf_p = swa_prompt(ap, swa_w_in[j], swa_sink[j], swa_w_out[j])
            ys, buf_s = swa_sample(as_, cache_swa[j], swa_w_in[j], swa_sink[j], swa_w_out[j], past)
            st["swa_p"].append(buf_p)
            st["swa_s"].append(buf_s)
        hp = hp + yp
        hs = hs + ys
        hp = ffn_residual(hp.reshape(-1, D_MODEL), norm_ffn[i], ffn_in_b[i], ffn_out_b[i]).reshape(hp.shape)
        hs = ffn_residual(hs.reshape(-1, D_MODEL), norm_ffn[i], ffn_in_b[i], ffn_out_b[i]).reshape(hs.shape)
    y_prompt = rms_norm(hp, norm_final)
    y_sample = rms_norm(hs, norm_final)
    return (y_prompt, y_sample,
            jnp.stack(st["nsa_kv_p"]), jnp.stack(st["nsa_kv_s"]),
            jnp.stack(st["nsa_win_p"]), jnp.stack(st["nsa_win_s"]),
            jnp.stack(st["dil1_p"]), jnp.stack(st["dil1_s"]),
            jnp.stack(st["dil2_p"]), jnp.stack(st["dil2_s"]),
            jnp.stack(st["dil3_p"]), jnp.stack(st["dil3_s"]),
            jnp.stack(st["dsa_kv_p"]), jnp.stack(st["dsa_kv_s"]),
            jnp.stack(st["dsa_idx_p"]), jnp.stack(st["dsa_idx_s"]),
            jnp.stack(st["swa_p"]), jnp.stack(st["swa_s"]))
```

```python
import functools

import jax, jax.numpy as jnp
from jax import lax
import numpy as np
from jax.experimental import pallas as pl
from jax.experimental.pallas import tpu as pltpu

D_MODEL = 1024
BATCH = 2
SEQ = 8192
DEPTH = 4
DEC_BATCH = 128
DEC_SEQ = 4
PAST_LEN = 8192
PAGE_SIZE = 128

HEAD_DIM = 64
N_HEADS = D_MODEL // HEAD_DIM
Q_DIM = N_HEADS * HEAD_DIM
ROPE_THETA = 10000.0
NORM_EPS = 1e-6
ATTN_BLOCK = 128
N_MIXERS = 4
D_FF = -(-(8 * D_MODEL) // (3 * 256)) * 256
NEG_INF = -1e30

NSA_KV_HEADS = 2
NSA_CMP_BLOCK = 32
NSA_SEL_BLOCK = 64
NSA_N_SEL = 16
NSA_WINDOW = 512
NSA_FORCE_BONUS = 1e4
NSA_IN = Q_DIM + 6 * NSA_KV_HEADS * HEAD_DIM + 3 * N_HEADS

DIL_KV_HEADS = 4
DIL_GROUPS = ((128, 1), (512, 4), (2048, 16))
DIL_IN = Q_DIM + len(DIL_GROUPS) * 2 * DIL_KV_HEADS * HEAD_DIM

DSA_KV_HEADS = 4
IDX_HEADS = 8
IDX_DIM = 64
DSA_TOPK = 256
IDX_SCALE = (IDX_DIM * IDX_HEADS) ** -0.5
DSA_IN = Q_DIM + 2 * DSA_KV_HEADS * HEAD_DIM + IDX_HEADS * IDX_DIM + IDX_DIM + IDX_HEADS

SWA_KV_HEADS = 2
SWA_WINDOW = 128
SWA_IN = Q_DIM + 2 * SWA_KV_HEADS * HEAD_DIM

N_NSA_LAYERS = (DEPTH + 3) // 4
N_DIL_LAYERS = (DEPTH + 2) // 4
N_DSA_LAYERS = (DEPTH + 1) // 4
N_SWA_LAYERS = DEPTH // 4


def rms_norm(x, g):
    xf = x.astype(jnp.float32)
    y = xf * lax.rsqrt(jnp.mean(xf * xf, axis=-1, keepdims=True) + NORM_EPS)
    return (y * g.astype(jnp.float32)).astype(x.dtype)


def rope(x, pos):
    half = x.shape[-1] // 2
    inv = ROPE_THETA ** (-jnp.arange(half, dtype=jnp.float32) / half)
    ang = pos.astype(jnp.float32)[:, None] * inv[None, :]
    shape = (pos.shape[0],) + (1,) * (x.ndim - 3) + (half,)
    cos, sin = jnp.cos(ang).reshape(shape), jnp.sin(ang).reshape(shape)
    xf = x.astype(jnp.float32)
    x1, x2 = xf[..., :half], xf[..., half:]
    return jnp.concatenate([x1 * cos - x2 * sin, x2 * cos + x1 * sin], axis=-1).astype(x.dtype)


def _tail(rows, window):
    n = rows.shape[1]
    return rows[:, n - min(window, n):]


def _masked_softmax(s, mask, sink=None):
    s = jnp.where(mask, s, NEG_INF)
    m = jnp.max(s, axis=-1, keepdims=True)
    if sink is not None:
        m = jnp.maximum(m, sink)
    e = jnp.where(mask, jnp.exp(s - m), 0.0)
    den = jnp.sum(e, axis=-1, keepdims=True)
    if sink is not None:
        den = den + jnp.exp(sink - m)
    den = jnp.maximum(den, 1e-30)
    return e / den, (m + jnp.log(den))[..., 0]


def _attend_dense(q, k, v, mask, sink=None):
    s = jnp.einsum('nqgrd,nkgd->ngrqk', q, k, preferred_element_type=jnp.float32) * (q.shape[-1] ** -0.5)
    sk = None if sink is None else sink.astype(jnp.float32)[None, :, :, None, None]
    p, lse = _masked_softmax(s, mask[:, None, None], sk)
    o = jnp.einsum('ngrqk,nkgd->nqgrd', p.astype(v.dtype), v)
    return o, jnp.transpose(lse, (0, 3, 1, 2))


def _attend_gathered(q, k, v, mask):
    s = jnp.einsum('nqgrd,nqkgd->nqgrk', q, k, preferred_element_type=jnp.float32) * (q.shape[-1] ** -0.5)
    p, lse = _masked_softmax(s, jnp.swapaxes(mask, 2, 3)[:, :, :, None, :])
    o = jnp.einsum('nqgrk,nqkgd->nqgrd', p.astype(v.dtype), v)
    return o, lse


def _blockwise(fn, n_blocks):
    out = lax.map(fn, jnp.arange(n_blocks))
    return jax.tree_util.tree_map(
        lambda a: jnp.moveaxis(a, 0, 1).reshape((a.shape[1], -1) + a.shape[3:]), out)


def _banded(q, k, v, window, sink=None):
    pad = ((0, 0), (window, 0), (0, 0), (0, 0))
    kp, vp = jnp.pad(k, pad), jnp.pad(v, pad)
    span = window + ATTN_BLOCK

    def body(i):
        start = i * ATTN_BLOCK
        qi = lax.dynamic_slice_in_dim(q, start, ATTN_BLOCK, 1)
        ki = lax.dynamic_slice_in_dim(kp, start, span, 1)
        vi = lax.dynamic_slice_in_dim(vp, start, span, 1)
        qpos = start + jnp.arange(ATTN_BLOCK)
        kpos = start - window + jnp.arange(span)
        diff = qpos[:, None] - kpos[None, :]
        mask = (diff >= 0) & (diff < window) & (kpos >= 0)[None, :]
        return _attend_dense(qi, ki, vi, mask[None], sink)[0]
    return _blockwise(body, q.shape[1] // ATTN_BLOCK)


def _nsa_project(h, w_in, pos):
    n, t, _ = h.shape
    g, r = NSA_KV_HEADS, N_HEADS // NSA_KV_HEADS
    z = h @ w_in
    kv_end = Q_DIM + 6 * g * HEAD_DIM
    q = rope(z[..., :Q_DIM].reshape(n, t, g, r, HEAD_DIM), pos)
    kv = z[..., Q_DIM:kv_end].reshape(n, t, 3, 2, g, HEAD_DIM)
    kv = jnp.stack([rope(kv[:, :, :, 0], pos), kv[:, :, :, 1]], axis=3).reshape(n, t, 6, g, HEAD_DIM)
    gates = jax.nn.sigmoid(z[..., kv_end:].astype(jnp.float32)).reshape(n, t, g, r, 3)
    return q, kv, gates


def _nsa_compress(rows, w_ck, w_cv):
    n, length, _, g, dh = rows.shape
    nb = length // NSA_CMP_BLOCK
    r = rows[:, :nb * NSA_CMP_BLOCK].reshape(n, nb, NSA_CMP_BLOCK, 2, g, dh)
    ck = jnp.einsum('nbcgd,gcde->nbge', r[:, :, :, 0], w_ck)
    cv = jnp.einsum('nbcgd,gcde->nbge', r[:, :, :, 1], w_cv)
    return ck, cv


def _nsa_cmp_attend(q, ck, cv, qpos):
    blk_end = (jnp.arange(ck.shape[1]) + 1) * NSA_CMP_BLOCK - 1
    mask = (blk_end[None, :] <= qpos[:, None])[None, :, None, None, :]
    s = jnp.einsum('nqgrd,ncgd->nqgrc', q, ck, preferred_element_type=jnp.float32) * (q.shape[-1] ** -0.5)
    p, _ = _masked_softmax(s, mask)
    o = jnp.einsum('nqgrc,ncgd->nqgrd', p.astype(cv.dtype), cv)
    return o, p


def _nsa_select(p_cmp, qpos, n_slc):
    n, t, g = p_cmp.shape[:3]
    ratio = NSA_SEL_BLOCK // NSA_CMP_BLOCK
    imp = p_cmp.sum(axis=3)
    imp = jnp.pad(imp, ((0, 0), (0, 0), (0, 0), (0, n_slc * ratio - imp.shape[-1])))
    imp = imp.reshape(n, t, g, n_slc, ratio).sum(-1)
    j = jnp.arange(n_slc)[None, :]
    cur = (qpos // NSA_SEL_BLOCK)[:, None]
    forced = ((j == 0) | (j == cur) | (j == cur - 1))[None, :, None, :]
    future = (j > cur)[None, :, None, :]
    score = jnp.where(future, NEG_INF, jnp.where(forced, imp + NSA_FORCE_BONUS, imp))
    return lax.top_k(score, min(NSA_N_SEL, n_slc))[1]


def _sel_positions(sel):
    n, t, g, ns = sel.shape
    pos = sel[..., None] * NSA_SEL_BLOCK + jnp.arange(NSA_SEL_BLOCK)
    return jnp.swapaxes(pos.reshape(n, t, g, ns * NSA_SEL_BLOCK), 2, 3)


def _nsa_gate(gates, o_cmp, o_slc, o_win, w_out, dtype):
    f = jnp.float32
    o = (gates[..., 0:1] * o_cmp.astype(f) + gates[..., 1:2] * o_slc.astype(f)
         + gates[..., 2:3] * o_win.astype(f))
    n, t = o.shape[:2]
    return o.astype(dtype).reshape(n, t, Q_DIM) @ w_out


def nsa_prompt(h, w_in, w_ck, w_cv, w_out):
    n, s, _ = h.shape
    pos = jnp.arange(s)
    q, kv, gates = _nsa_project(h, w_in, pos)
    ck, cv = _nsa_compress(kv[:, :, 0:2], w_ck, w_cv)
    o_cmp, p_cmp = _nsa_cmp_attend(q, ck, cv, pos)
    sel = _nsa_select(p_cmp, pos, -(-s // NSA_SEL_BLOCK))
    slc_k, slc_v = kv[:, :, 2], kv[:, :, 3]
    bidx = jnp.arange(n)[:, None, None, None]
    gidx = jnp.arange(NSA_KV_HEADS)[None, None, None, :]

    def body(i):
        start = i * ATTN_BLOCK
        qpos = start + jnp.arange(ATTN_BLOCK)
        kpos = _sel_positions(lax.dynamic_slice_in_dim(sel, start, ATTN_BLOCK, 1))
        qi = lax.dynamic_slice_in_dim(q, start, ATTN_BLOCK, 1)
        return _attend_gathered(qi, slc_k[bidx, kpos, gidx], slc_v[bidx, kpos, gidx],
                                kpos <= qpos[None, :, None, None])[0]
    o_slc = _blockwise(body, s // ATTN_BLOCK)
    o_win = _banded(q, kv[:, :, 4], kv[:, :, 5], NSA_WINDOW)
    y = _nsa_gate(gates, o_cmp, o_slc, o_win, w_out, h.dtype)
    return y, kv[:, :, 0:4], _tail(kv[:, :, 4:6], NSA_WINDOW)


def nsa_sample(h, pool, layer, win_buf, page_table, w_in, w_ck, w_cv, w_out, past):
    n, t, _ = h.shape
    pos = past + jnp.arange(t)
    q, kv, gates = _nsa_project(h, w_in, pos)
    past_rows = pool[layer, page_table, :, 0:2].reshape(n, past, 2, NSA_KV_HEADS, HEAD_DIM)
    ck_p, cv_p = _nsa_compress(past_rows, w_ck, w_cv)
    ck_n, cv_n = _nsa_compress(kv[:, :, 0:2], w_ck, w_cv)
    o_cmp, p_cmp = _nsa_cmp_attend(q, jnp.concatenate([ck_p, ck_n], 1),
                                   jnp.concatenate([cv_p, cv_n], 1), pos)
    sel = _nsa_select(p_cmp, pos, -(-(past + t) // NSA_SEL_BLOCK))
    kpos = _sel_positions(sel)
    bidx = jnp.arange(n)[:, None, None, None]
    gidx = jnp.arange(NSA_KV_HEADS)[None, None, None, :]
    pc = jnp.clip(kpos, 0, past - 1)
    phys = page_table[bidx, pc // PAGE_SIZE]
    off = pc % PAGE_SIZE
    nc = jnp.clip(kpos - past, 0, t - 1)
    in_past = (kpos < past)[..., None]
    k = jnp.where(in_past, pool[layer, phys, off, 2, gidx], kv[bidx, nc, 2, gidx])
    v = jnp.where(in_past, pool[layer, phys, off, 3, gidx], kv[bidx, nc, 3, gidx])
    o_slc, _ = _attend_gathered(q, k, v, kpos <= pos[None, :, None, None])
    keys = jnp.concatenate([win_buf, kv[:, :, 4:6]], axis=1)
    kp = past - win_buf.shape[1] + jnp.arange(keys.shape[1])
    diff = pos[:, None] - kp[None, :]
    o_win, _ = _attend_dense(q, keys[:, :, 0], keys[:, :, 1], ((diff >= 0) & (diff < NSA_WINDOW))[None])
    y = _nsa_gate(gates, o_cmp, o_slc, o_win, w_out, h.dtype)
    return y, kv[:, :, 0:4], _tail(keys, NSA_WINDOW)


def _dil_project(h, w_in, pos):
    n, t, _ = h.shape
    g = DIL_KV_HEADS
    z = h @ w_in
    q = rope(z[..., :Q_DIM].reshape(n, t, g, N_HEADS // g, HEAD_DIM), pos)
    kv = z[..., Q_DIM:].reshape(n, t, len(DIL_GROUPS), 2, g, HEAD_DIM)
    return q, jnp.stack([rope(kv[:, :, :, 0], pos), kv[:, :, :, 1]], axis=3)


def _dil_mix(outs, lses):
    w = jax.nn.softmax(jnp.stack(lses), axis=0)
    return jnp.sum(w[..., None] * jnp.stack(outs).astype(jnp.float32), axis=0)


def _dil_group_prompt(q, k, v, dil, n_keys):
    dist = jnp.arange(n_keys) * dil

    def body(i):
        start = i * ATTN_BLOCK
        qi = lax.dynamic_slice_in_dim(q, start, ATTN_BLOCK, 1)
        kpos = (start + jnp.arange(ATTN_BLOCK))[:, None] - dist[None, :]
        kc = jnp.maximum(kpos, 0)
        return _attend_gathered(qi, k[:, kc], v[:, kc], (kpos >= 0)[None, :, :, None])
    return _blockwise(body, q.shape[1] // ATTN_BLOCK)


def dil_prompt(h, w_in, w_out):
    n, s, _ = h.shape
    q, kv = _dil_project(h, w_in, jnp.arange(s))
    outs, lses, bufs = [], [], []
    for gi, (win, dil) in enumerate(DIL_GROUPS):
        o, lse = _dil_group_prompt(q, kv[:, :, gi, 0], kv[:, :, gi, 1], dil, win // dil + 1)
        outs.append(o)
        lses.append(lse)
        bufs.append(_tail(kv[:, :, gi], win))
    y = _dil_mix(outs, lses).astype(h.dtype).reshape(n, s, Q_DIM) @ w_out
    return y, bufs


def dil_sample(h, bufs, w_in, w_out, past):
    n, t, _ = h.shape
    pos = past + jnp.arange(t)
    q, kv = _dil_project(h, w_in, pos)
    outs, lses, new_bufs = [], [], []
    for gi, ((win, dil), buf) in enumerate(zip(DIL_GROUPS, bufs)):
        keys = jnp.concatenate([buf, kv[:, :, gi]], axis=1)
        base = past - buf.shape[1]
        local = pos[:, None] - (jnp.arange(win // dil + 1) * dil)[None, :] - base
        lc = jnp.maximum(local, 0)
        o, lse = _attend_gathered(q, keys[:, lc, 0], keys[:, lc, 1], (local >= 0)[None, :, :, None])
        outs.append(o)
        lses.append(lse)
        new_bufs.append(_tail(keys, win))
    y = _dil_mix(outs, lses).astype(h.dtype).reshape(n, t, Q_DIM) @ w_out
    return y, new_bufs


def _dsa_project(h, w_in, pos):
    n, t, _ = h.shape
    g = DSA_KV_HEADS
    z = h @ w_in
    o1 = Q_DIM
    o2 = o1 + 2 * g * HEAD_DIM
    o3 = o2 + IDX_HEADS * IDX_DIM
    o4 = o3 + IDX_DIM
    q = rope(z[..., :o1].reshape(n, t, g, N_HEADS // g, HEAD_DIM), pos)
    kv = z[..., o1:o2].reshape(n, t, 2, g, HEAD_DIM)
    kv = jnp.stack([rope(kv[:, :, 0], pos), kv[:, :, 1]], axis=2)
    q_idx = rope(z[..., o2:o3].reshape(n, t, IDX_HEADS, IDX_DIM), pos)
    k_idx = rope(z[..., o3:o4], pos)
    w_idx = z[..., o4:].astype(jnp.float32) * IDX_SCALE
    return q, kv, q_idx, k_idx, w_idx


def _index_scores(q_idx, w_idx, k_idx):
    dots = jnp.einsum('nqhd,nkd->nqhk', q_idx, k_idx, preferred_element_type=jnp.float32)
    return jnp.einsum('nqh,nqhk->nqk', w_idx, jax.nn.relu(dots))


def dsa_prompt(h, w_in, w_out):
    n, s, _ = h.shape
    pos = jnp.arange(s)
    q, kv, q_idx, k_idx, w_idx = _dsa_project(h, w_in, pos)
    top = min(DSA_TOPK, s // 4)
    bidx = jnp.arange(n)[:, None, None]

    def body(i):
        start = i * ATTN_BLOCK
        qpos = start + jnp.arange(ATTN_BLOCK)
        sc = _index_scores(lax.dynamic_slice_in_dim(q_idx, start, ATTN_BLOCK, 1),
                           lax.dynamic_slice_in_dim(w_idx, start, ATTN_BLOCK, 1), k_idx)
        sc = jnp.where((pos[None, :] <= qpos[:, None])[None], sc, NEG_INF)
        _, idx = lax.top_k(sc, top)
        sel = kv[bidx, idx]
        qi = lax.dynamic_slice_in_dim(q, start, ATTN_BLOCK, 1)
        return _attend_gathered(qi, sel[:, :, :, 0], sel[:, :, :, 1],
                                (idx <= qpos[None, :, None])[..., None])[0]
    o = _blockwise(body, s // ATTN_BLOCK)
    y = o.reshape(n, s, Q_DIM) @ w_out
    return y, kv, k_idx


def dsa_sample(h, pool_kv, pool_idx, layer, page_table, w_in, w_out, past):
    n, t, _ = h.shape
    pos = past + jnp.arange(t)
    q, kv, q_idx, k_idx, w_idx = _dsa_project(h, w_in, pos)
    total = past + t
    k_idx_all = jnp.concatenate([pool_idx[layer, page_table].reshape(n, past, IDX_DIM), k_idx], axis=1)
    sc = _index_scores(q_idx, w_idx, k_idx_all)
    sc = jnp.where((jnp.arange(total)[None, :] <= pos[:, None])[None], sc, NEG_INF)
    _, idx = lax.top_k(sc, min(DSA_TOPK, total // 4))
    bidx = jnp.arange(n)[:, None, None]
    pc = jnp.clip(idx, 0, past - 1)
    kv_past = pool_kv[layer, page_table[bidx, pc // PAGE_SIZE], pc % PAGE_SIZE]
    kv_new = kv[bidx, jnp.clip(idx - past, 0, t - 1)]
    sel = jnp.where((idx < past)[..., None, None, None], kv_past, kv_new)
    o, _ = _attend_gathered(q, sel[:, :, :, 0], sel[:, :, :, 1], (idx <= pos[None, :, None])[..., None])
    y = o.reshape(n, t, Q_DIM) @ w_out
    return y, kv, k_idx


def _swa_project(h, w_in, pos):
    n, t, _ = h.shape
    g = SWA_KV_HEADS
    z = h @ w_in
    q = rope(z[..., :Q_DIM].reshape(n, t, g, N_HEADS // g, HEAD_DIM), pos)
    kv = z[..., Q_DIM:].reshape(n, t, 2, g, HEAD_DIM)
    return q, jnp.stack([rope(kv[:, :, 0], pos), kv[:, :, 1]], axis=2)


def swa_prompt(h, w_in, sink, w_out):
    n, s, _ = h.shape
    q, kv = _swa_project(h, w_in, jnp.arange(s))
    o = _banded(q, kv[:, :, 0], kv[:, :, 1], SWA_WINDOW, sink.reshape(SWA_KV_HEADS, -1))
    return o.reshape(n, s, Q_DIM) @ w_out, _tail(kv, SWA_WINDOW)


def swa_sample(h, buf, w_in, sink, w_out, past):
    n, t, _ = h.shape
    pos = past + jnp.arange(t)
    q, kv = _swa_project(h, w_in, pos)
    keys = jnp.concatenate([buf, kv], axis=1)
    kp = past - buf.shape[1] + jnp.arange(keys.shape[1])
    diff = pos[:, None] - kp[None, :]
    o, _ = _attend_dense(q, keys[:, :, 0], keys[:, :, 1], ((diff >= 0) & (diff < SWA_WINDOW))[None],
                         sink.reshape(SWA_KV_HEADS, -1))
    return o.reshape(n, t, Q_DIM) @ w_out, _tail(keys, SWA_WINDOW)


MXU_DTYPE = jnp.bfloat16
BIAS_DTYPE = jnp.bfloat16
ROW_TILE = 512
FF_CHUNK = 256
VMEM_LIMIT = 56 * 1024 * 1024


def _ffn_body(x_ref, g_ref, win_ref, wout_ref, o_ref):
    x = x_ref[...]
    xn = x * lax.rsqrt(jnp.mean(x * x, axis=-1, keepdims=True) + NORM_EPS) * g_ref[...]
    xb = xn.astype(MXU_DTYPE)
    acc = x
    for j in range(D_FF // FF_CHUNK):
        lo = j * FF_CHUNK
        gate = jnp.dot(xb, win_ref[:, lo:lo + FF_CHUNK], preferred_element_type=jnp.float32)
        up = jnp.dot(xb, win_ref[:, D_FF + lo:D_FF + lo + FF_CHUNK], preferred_element_type=jnp.float32)
        a = (gate * jax.nn.sigmoid(gate) * up).astype(MXU_DTYPE)
        acc = acc + jnp.dot(a, wout_ref[lo:lo + FF_CHUNK, :], preferred_element_type=jnp.float32)
    o_ref[...] = acc


def ffn_residual(x, g, w_in_bf16, w_out_bf16):
    rows = x.shape[0]
    return pl.pallas_call(
        _ffn_body,
        grid=(rows // ROW_TILE,),
        in_specs=[
            pl.BlockSpec((ROW_TILE, D_MODEL), lambda i: (i, 0)),
            pl.BlockSpec((1, D_MODEL), lambda i: (0, 0)),
            pl.BlockSpec((D_MODEL, 2 * D_FF), lambda i: (0, 0)),
            pl.BlockSpec((D_FF, D_MODEL), lambda i: (0, 0)),
        ],
        out_specs=pl.BlockSpec((ROW_TILE, D_MODEL), lambda i: (i, 0)),
        out_shape=jax.ShapeDtypeStruct((rows, D_MODEL), jnp.float32),
        compiler_params=pltpu.CompilerParams(
            dimension_semantics=("arbitrary",), vmem_limit_bytes=VMEM_LIMIT),
        name="ffn_residual",
    )(x, g.reshape(1, D_MODEL), w_in_bf16, w_out_bf16)


Q_TILE = 128
K_TILE = 512
_AT_BT = (((1,), (1,)), ((), ()))
INT32_MIN = -2 ** 31


def _causal_tiles(qi, tq, tk):
    return (qi * tq + tq + tk - 1) // tk


def _mflash_body(q_ref, k_ref, v_ref, b_ref, o_ref, m_ref, l_ref, acc_ref, *, r, tq, tk):
    qi = pl.program_id(2)
    q = q_ref[0, 0].reshape(r * tq, HEAD_DIM)
    m_ref[...] = jnp.full(m_ref.shape, NEG_INF, jnp.float32)
    l_ref[...] = jnp.zeros(l_ref.shape, jnp.float32)
    acc_ref[...] = jnp.zeros(acc_ref.shape, jnp.float32)

    def step(kt, carry):
        ks = pl.multiple_of(kt * tk, tk)
        k = k_ref[0, 0, pl.ds(ks, tk), :]
        v = v_ref[0, 0, pl.ds(ks, tk), :]
        s = lax.dot_general(q, k, _AT_BT, preferred_element_type=jnp.float32)
        bias = b_ref[0, 0, kt].astype(jnp.float32)
        s = (s.reshape(r, tq, tk) + bias[None]).reshape(r * tq, tk)
        m_old = m_ref[...]
        m_new = jnp.maximum(m_old, jnp.max(s, axis=1, keepdims=True))
        p = jnp.exp(s - m_new)
        alpha = jnp.exp(m_old - m_new)
        l_ref[...] = alpha * l_ref[...] + jnp.sum(p, axis=1, keepdims=True)
        acc_ref[...] = alpha * acc_ref[...] + jnp.dot(p.astype(MXU_DTYPE), v,
                                                      preferred_element_type=jnp.float32)
        m_ref[...] = m_new
        return carry

    lax.fori_loop(0, _causal_tiles(qi, tq, tk), step, 0)
    o = acc_ref[...] / jnp.maximum(l_ref[...], 1e-30)
    o_ref[0, 0] = o.reshape(r, tq, HEAD_DIM)


def masked_flash(q, k, v, bias):
    n, g, r, t, dh = q.shape
    gb = bias.shape[1]
    nkt = t // K_TILE
    bmap = (lambda a, b, c: (a, b, 0, c, 0)) if gb == g else (lambda a, b, c: (a, 0, 0, c, 0))
    return pl.pallas_call(
        functools.partial(_mflash_body, r=r, tq=Q_TILE, tk=K_TILE),
        grid=(n, g, t // Q_TILE),
        in_specs=[
            pl.BlockSpec((1, 1, r, Q_TILE, dh), lambda a, b, c: (a, b, 0, c, 0)),
            pl.BlockSpec((1, 1, t, dh), lambda a, b, c: (a, b, 0, 0)),
            pl.BlockSpec((1, 1, t, dh), lambda a, b, c: (a, b, 0, 0)),
            pl.BlockSpec((1, 1, nkt, Q_TILE, K_TILE), bmap),
        ],
        out_specs=pl.BlockSpec((1, 1, r, Q_TILE, dh), lambda a, b, c: (a, b, 0, c, 0)),
        out_shape=jax.ShapeDtypeStruct((n, g, r, t, dh), jnp.float32),
        scratch_shapes=[pltpu.VMEM((r * Q_TILE, 1), jnp.float32),
                        pltpu.VMEM((r * Q_TILE, 1), jnp.float32),
                        pltpu.VMEM((r * Q_TILE, dh), jnp.float32)],
        compiler_params=pltpu.CompilerParams(
            dimension_semantics=("arbitrary", "arbitrary", "arbitrary"), vmem_limit_bytes=VMEM_LIMIT),
        name="masked_flash",
    )(q, k, v, bias)


def _dsa_select_body(qx_ref, kx_ref, w_ref, o_ref, key_ref, *, tq, tk, top):
    qi = pl.program_id(1)
    nkt = key_ref.shape[0]
    n_valid = _causal_tiles(qi, tq, tk)
    qx = qx_ref[0].reshape(IDX_HEADS * tq, IDX_DIM)
    w = w_ref[0]
    row = qi * tq + lax.broadcasted_iota(jnp.int32, (tq, 1), 0)

    def causal(kt):
        col = kt * tk + lax.broadcasted_iota(jnp.int32, (1, tk), 1)
        return col <= row

    def score_step(kt, carry):
        ks = pl.multiple_of(kt * tk, tk)
        d = lax.dot_general(qx, kx_ref[0, pl.ds(ks, tk), :], _AT_BT, preferred_element_type=jnp.float32)
        d = jnp.maximum(d, 0.0).reshape(IDX_HEADS, tq, tk)
        sc = w[:, 0:1] * d[0]
        for h in range(1, IDX_HEADS):
            sc = sc + w[:, h:h + 1] * d[h]
        sc = jnp.where(causal(kt), sc, NEG_INF)
        bits = pltpu.bitcast(sc, jnp.int32)
        key = jnp.where(bits >= 0, bits, bits ^ 0x7FFFFFFF)
        key_ref[kt] = jnp.where(key == -1, 0, key)
        return carry

    lax.fori_loop(0, n_valid, score_step, 0)

    def count(pred):
        def body(kt, c):
            return c + jnp.sum(jnp.where(pred(key_ref[kt]), 1.0, 0.0), axis=1, keepdims=True)
        return lax.fori_loop(0, n_valid, body, jnp.zeros((tq, 1), jnp.float32))

    c0 = count(lambda kk: kk >= 0)
    tau = jnp.where(c0 >= top, 0, INT32_MIN).astype(jnp.int32)

    def bit_step(i, tau):
        cand = tau | jnp.left_shift(jnp.int32(1), 30 - i)
        c = count(lambda kk: kk >= cand)
        return jnp.where(c >= top, cand, tau)

    tau = lax.fori_loop(0, 31, bit_step, tau)
    need = top - count(lambda kk: kk > tau)
    ri = lax.broadcasted_iota(jnp.int32, (tk, tk), 0)
    ci = lax.broadcasted_iota(jnp.int32, (tk, tk), 1)
    tri = jnp.where(ri <= ci, 1.0, 0.0).astype(MXU_DTYPE)

    def out_step(kt, seen):
        kk = key_ref[kt]
        tie = kk == tau
        tie_f = jnp.where(tie, 1.0, 0.0)
        rank = seen + jnp.dot(tie_f.astype(MXU_DTYPE), tri, preferred_element_type=jnp.float32)
        sel = (kk > tau) | (tie & (rank <= need))
        o_ref[0, kt] = jnp.where(sel & causal(kt), 0.0, NEG_INF).astype(o_ref.dtype)
        return seen + jnp.sum(tie_f, axis=1, keepdims=True)

    lax.fori_loop(0, n_valid, out_step, jnp.zeros((tq, 1), jnp.float32))

    def fill_step(kt, carry):
        o_ref[0, kt] = jnp.full((tq, tk), NEG_INF, o_ref.dtype)
        return carry

    lax.fori_loop(n_valid, nkt, fill_step, 0)


def dsa_select_bias(q_idx, k_idx, w_idx, top):
    n, h, t, di = q_idx.shape
    nkt = t // K_TILE
    return pl.pallas_call(
        functools.partial(_dsa_select_body, tq=Q_TILE, tk=K_TILE, top=top),
        grid=(n, t // Q_TILE),
        in_specs=[
            pl.BlockSpec((1, h, Q_TILE, di), lambda a, c: (a, 0, c, 0)),
            pl.BlockSpec((1, t, di), lambda a, c: (a, 0, 0)),
            pl.BlockSpec((1, Q_TILE, h), lambda a, c: (a, c, 0)),
        ],
        out_specs=pl.BlockSpec((1, nkt, Q_TILE, K_TILE), lambda a, c: (a, 0, c, 0)),
        out_shape=jax.ShapeDtypeStruct((n, nkt, t, K_TILE), BIAS_DTYPE),
        scratch_shapes=[pltpu.VMEM((nkt, Q_TILE, K_TILE), jnp.int32)],
        compiler_params=pltpu.CompilerParams(
            dimension_semantics=("arbitrary", "arbitrary"), vmem_limit_bytes=VMEM_LIMIT),
        name="dsa_select",
    )(q_idx, k_idx, w_idx)


TOPK_ROWS = 256
REMOVED = -3e38


def _topk_mask_body(s_ref, o_ref, *, k):
    s0 = s_ref[...]
    lane = lax.broadcasted_iota(jnp.int32, s0.shape, 1).astype(jnp.float32)
    width = float(s0.shape[1])

    def step(_, carry):
        s, sel = carry
        m = jnp.max(s, axis=1, keepdims=True)
        first = jnp.min(jnp.where(s == m, lane, width), axis=1, keepdims=True)
        hit = lane == first
        return jnp.where(hit, REMOVED, s), jnp.where(hit, 1.0, sel)

    _, sel = lax.fori_loop(0, k, step, (s0, jnp.zeros(s0.shape, jnp.float32)))
    o_ref[...] = sel


def topk_mask(score, k):
    lead, n = score.shape[:-1], score.shape[-1]
    rows = int(np.prod(lead))
    width = -(-n // LANE) * LANE
    s2 = jnp.pad(score.reshape(rows, n), ((0, 0), (0, width - n)), constant_values=REMOVED)
    tm = min(TOPK_ROWS, rows)
    sel = pl.pallas_call(
        functools.partial(_topk_mask_body, k=k),
        grid=(rows // tm,),
        in_specs=[pl.BlockSpec((tm, width), lambda i: (i, 0))],
        out_specs=pl.BlockSpec((tm, width), lambda i: (i, 0)),
        out_shape=jax.ShapeDtypeStruct((rows, width), jnp.float32),
        compiler_params=pltpu.CompilerParams(dimension_semantics=("arbitrary",)),
        name="topk_mask",
    )(s2)
    return (sel[:, :n] > 0.5).reshape(lead + (n,))


def _nsa_select_mask(p_cmp, qpos, n_slc):
    n, t, g = p_cmp.shape[:3]
    ratio = NSA_SEL_BLOCK // NSA_CMP_BLOCK
    imp = p_cmp.sum(axis=3)
    imp = jnp.pad(imp, ((0, 0), (0, 0), (0, 0), (0, n_slc * ratio - imp.shape[-1])))
    imp = imp.reshape(n, t, g, n_slc, ratio).sum(-1)
    j = jnp.arange(n_slc)[None, :]
    cur = (qpos // NSA_SEL_BLOCK)[:, None]
    forced = ((j == 0) | (j == cur) | (j == cur - 1))[None, :, None, :]
    future = (j > cur)[None, :, None, :]
    score = jnp.where(future, NEG_INF, jnp.where(forced, imp + NSA_FORCE_BONUS, imp))
    return topk_mask(score, min(NSA_N_SEL, n_slc))


def _heads_major(x):
    return jnp.transpose(x, (0, 2, 3, 1, 4))


def _attn_scaled_q(q):
    return _heads_major(q * (HEAD_DIM ** -0.5)).astype(MXU_DTYPE)


def nsa_prompt_p(h, w_in, w_ck, w_cv, w_out):
    n, s, _ = h.shape
    pos = jnp.arange(s)
    q, kv, gates = _nsa_project(h, w_in, pos)
    ck, cv = _nsa_compress(kv[:, :, 0:2], w_ck, w_cv)
    o_cmp, p_cmp = _nsa_cmp_attend(q, ck, cv, pos)
    n_slc = -(-s // NSA_SEL_BLOCK)
    blk = _nsa_select_mask(p_cmp, pos, n_slc)
    blk = jnp.transpose(blk, (0, 2, 1, 3))
    nkt = s // K_TILE
    per_tile = K_TILE // NSA_SEL_BLOCK
    kpos = jnp.arange(s).reshape(nkt, 1, K_TILE)
    allow = (jnp.repeat(blk.reshape(n, NSA_KV_HEADS, s, nkt, per_tile), NSA_SEL_BLOCK, axis=-1)
             .transpose(0, 1, 3, 2, 4)) & (kpos <= pos[None, :, None])
    bias = jnp.where(allow, 0.0, NEG_INF).astype(BIAS_DTYPE)
    kb = jnp.transpose(kv[:, :, 2], (0, 2, 1, 3)).astype(MXU_DTYPE)
    vb = jnp.transpose(kv[:, :, 3], (0, 2, 1, 3)).astype(MXU_DTYPE)
    o_slc = jnp.transpose(masked_flash(_attn_scaled_q(q), kb, vb, bias), (0, 3, 1, 2, 4))
    o_win = _banded(q, kv[:, :, 4], kv[:, :, 5], NSA_WINDOW)
    y = _nsa_gate(gates, o_cmp, o_slc, o_win, w_out, h.dtype)
    return y, kv[:, :, 0:4], _tail(kv[:, :, 4:6], NSA_WINDOW)


def dsa_prompt_p(h, w_in, w_out):
    n, s, _ = h.shape
    pos = jnp.arange(s)
    q, kv, q_idx, k_idx, w_idx = _dsa_project(h, w_in, pos)
    top = min(DSA_TOPK, s // 4)
    bias = dsa_select_bias(jnp.transpose(q_idx, (0, 2, 1, 3)).astype(MXU_DTYPE),
                           k_idx.astype(MXU_DTYPE), w_idx, top)
    kb = jnp.transpose(kv[:, :, 0], (0, 2, 1, 3)).astype(MXU_DTYPE)
    vb = jnp.transpose(kv[:, :, 1], (0, 2, 1, 3)).astype(MXU_DTYPE)
    o = masked_flash(_attn_scaled_q(q), kb, vb, bias[:, None])
    o = jnp.transpose(o, (0, 3, 1, 2, 4))
    y = o.reshape(n, s, Q_DIM) @ w_out
    return y, kv, k_idx


PAGES_PER_CHUNK = 32
LANE = 128


def _flash_update(s, pv, m_ref, l_ref, acc_ref):
    m_old = m_ref[...]
    m_new = jnp.maximum(m_old, jnp.max(s, axis=1, keepdims=True))
    p = jnp.exp(s - m_new)
    alpha = jnp.exp(m_old - m_new)
    l_ref[...] = alpha * l_ref[...] + jnp.sum(p, axis=1, keepdims=True)
    acc_ref[...] = alpha * acc_ref[...] + pv(p.astype(MXU_DTYPE))
    m_ref[...] = m_new


def _paged_attn_body(pt_ref, q_ref, bn_ref, kn_ref, vn_ref, b_ref, *rest, ppc):
    kp, vp = rest[:ppc], rest[ppc:2 * ppc]
    o_ref, m_ref, l_ref, acc_ref = rest[2 * ppc:]
    c = pl.program_id(1)
    q = q_ref[0]

    @pl.when(c == 0)
    def _():
        m_ref[...] = jnp.full(m_ref.shape, NEG_INF, jnp.float32)
        l_ref[...] = jnp.zeros(l_ref.shape, jnp.float32)
        acc_ref[...] = jnp.zeros(acc_ref.shape, jnp.float32)
        s = lax.dot_general(q, kn_ref[0], _AT_BT, preferred_element_type=jnp.float32) + bn_ref[0].astype(jnp.float32)
        _flash_update(s, lambda p: jnp.dot(p, vn_ref[0], preferred_element_type=jnp.float32), m_ref, l_ref, acc_ref)

    parts = [lax.dot_general(q, kp[i][...].astype(MXU_DTYPE), _AT_BT, preferred_element_type=jnp.float32)
             for i in range(ppc)]
    s = jnp.concatenate(parts, axis=1) + b_ref[0].astype(jnp.float32)

    def pv(p):
        acc = jnp.dot(p[:, 0:PAGE_SIZE], vp[0][...].astype(MXU_DTYPE), preferred_element_type=jnp.float32)
        for i in range(1, ppc):
            acc = acc + jnp.dot(p[:, i * PAGE_SIZE:(i + 1) * PAGE_SIZE], vp[i][...].astype(MXU_DTYPE),
                                preferred_element_type=jnp.float32)
        return acc

    _flash_update(s, pv, m_ref, l_ref, acc_ref)

    @pl.when(c == pl.num_programs(1) - 1)
    def _():
        o_ref[0] = acc_ref[...] / jnp.maximum(l_ref[...], 1e-30)


def paged_attn(q2, pool, layer, page_table, k_blk, v_blk, bias, k_new, v_new, bias_new):
    n, rows, lw = q2.shape
    n_pages = page_table.shape[1]
    ppc = min(PAGES_PER_CHUNK, n_pages)
    n_chunks = n_pages // ppc

    def page_spec(i, blk):
        return pl.BlockSpec((None, None, PAGE_SIZE, lw),
                            lambda a, c, pt: (layer, pt[a * n_pages + c * ppc + i], 0, blk))

    per_seq = lambda a, c, pt: (a, 0, 0)
    grid_spec = pltpu.PrefetchScalarGridSpec(
        num_scalar_prefetch=1,
        grid=(n, n_chunks),
        in_specs=[pl.BlockSpec((1, rows, lw), per_seq),
                  pl.BlockSpec((1, rows, PAGE_SIZE), per_seq),
                  pl.BlockSpec((1, PAGE_SIZE, lw), per_seq),
                  pl.BlockSpec((1, PAGE_SIZE, lw), per_seq),
                  pl.BlockSpec((1, rows, ppc * PAGE_SIZE), lambda a, c, pt: (a, 0, c))]
                 + [page_spec(i, k_blk) for i in range(ppc)]
                 + [page_spec(i, v_blk) for i in range(ppc)],
        out_specs=pl.BlockSpec((1, rows, lw), per_seq),
        scratch_shapes=[pltpu.VMEM((rows, 1), jnp.float32),
                        pltpu.VMEM((rows, 1), jnp.float32),
                        pltpu.VMEM((rows, lw), jnp.float32)],
    )
    return pl.pallas_call(
        functools.partial(_paged_attn_body, ppc=ppc),
        grid_spec=grid_spec,
        out_shape=jax.ShapeDtypeStruct((n, rows, lw), jnp.float32),
        compiler_params=pltpu.CompilerParams(
            dimension_semantics=("arbitrary", "arbitrary"), vmem_limit_bytes=VMEM_LIMIT),
        name="paged_attn",
    )(page_table.reshape(-1), q2, bias_new, k_new, v_new, bias, *([pool] * (2 * ppc)))


def _group_lane_queries(q):
    n, t, g, r, dh = q.shape
    qs = jnp.transpose(q * (dh ** -0.5), (0, 2, 1, 3, 4))
    onehot = jnp.eye(g, dtype=qs.dtype)
    q2 = qs[:, :, :, :, None, :] * onehot[None, :, None, None, :, None]
    return q2.reshape(n, g * t * r, g * dh).astype(MXU_DTYPE)


def _ungroup_lanes(o, t, g, r, dh):
    n = o.shape[0]
    o6 = o.reshape(n, g, t, r, g, dh)
    return jnp.stack([o6[:, gi, :, :, gi] for gi in range(g)], axis=2)


def _rows_bias(allow, g, r):
    n, t, ga, k = allow.shape
    a = jnp.broadcast_to(jnp.transpose(allow, (0, 2, 1, 3))[:, :, :, None, :], (n, ga, t, r, k))
    if ga != g:
        a = jnp.broadcast_to(a, (n, g, t, r, k))
    return jnp.where(a, 0.0, NEG_INF).astype(BIAS_DTYPE).reshape(n, g * t * r, k)


def _new_rows_page(x, lw):
    n, t, _ = x.shape
    return jnp.pad(x, ((0, 0), (0, PAGE_SIZE - t), (0, 0))).astype(MXU_DTYPE)


def nsa_sample_p(h, pool, layer, win_buf, page_table, w_in, w_ck, w_cv, w_out, past):
    n, t, _ = h.shape
    g, r = NSA_KV_HEADS, N_HEADS // NSA_KV_HEADS
    pos = past + jnp.arange(t)
    q, kv, gates = _nsa_project(h, w_in, pos)
    past_rows = pool[layer, page_table, :, 0:2].reshape(n, past, 2, NSA_KV_HEADS, HEAD_DIM)
    ck_p, cv_p = _nsa_compress(past_rows, w_ck, w_cv)
    ck_n, cv_n = _nsa_compress(kv[:, :, 0:2], w_ck, w_cv)
    o_cmp, p_cmp = _nsa_cmp_attend(q, jnp.concatenate([ck_p, ck_n], 1),
                                   jnp.concatenate([cv_p, cv_n], 1), pos)
    n_slc = -(-(past + t) // NSA_SEL_BLOCK)
    blk = _nsa_select_mask(p_cmp, pos, n_slc)
    n_past_blk = past // NSA_SEL_BLOCK
    allow_past = jnp.repeat(blk[..., :n_past_blk], NSA_SEL_BLOCK, axis=-1)
    j = jnp.arange(PAGE_SIZE)
    new_blk = (past + j) // NSA_SEL_BLOCK
    allow_new = (jnp.take(blk, jnp.minimum(new_blk, n_slc - 1), axis=-1)
                 & (j[None, :] <= jnp.arange(t)[:, None])[None, :, None, :] & (j < t))
    lw = g * HEAD_DIM
    pool_flat = pool.reshape(pool.shape[0], pool.shape[1], PAGE_SIZE, 4 * lw)
    o = paged_attn(_group_lane_queries(q), pool_flat, layer, page_table, 2, 3,
                   _rows_bias(allow_past, g, r),
                   _new_rows_page(kv[:, :, 2].reshape(n, t, lw), lw),
                   _new_rows_page(kv[:, :, 3].reshape(n, t, lw), lw),
                   _rows_bias(allow_new, g, r))
    o_slc = _ungroup_lanes(o, t, g, r, HEAD_DIM)
    keys = jnp.concatenate([win_buf, kv[:, :, 4:6]], axis=1)
    kp = past - win_buf.shape[1] + jnp.arange(keys.shape[1])
    diff = pos[:, None] - kp[None, :]
    o_win, _ = _attend_dense(q, keys[:, :, 0], keys[:, :, 1], ((diff >= 0) & (diff < NSA_WINDOW))[None])
    y = _nsa_gate(gates, o_cmp, o_slc, o_win, w_out, h.dtype)
    return y, kv[:, :, 0:4], _tail(keys, NSA_WINDOW)


def _dsa_sample_select_body(pt_ref, qx_ref, w_ref, kn_ref, *rest, n_pages, t_new, top):
    pages = rest[:n_pages]
    o_ref, key_ref = rest[n_pages:]
    qx = qx_ref[0]
    w = w_ref[0]
    row = lax.broadcasted_iota(jnp.int32, (8, 1), 0)
    col = lax.broadcasted_iota(jnp.int32, (1, PAGE_SIZE), 1)
    new_ok = (col <= row) & (col < t_new)

    for i in range(n_pages + 1):
        kx = kn_ref[0] if i == n_pages else pages[i][...].astype(MXU_DTYPE)
        d = lax.dot_general(qx, kx, _AT_BT, preferred_element_type=jnp.float32)
        d = jnp.maximum(d, 0.0).reshape(IDX_HEADS, 8, PAGE_SIZE)
        sc = w[:, 0:1] * d[0]
        for hh in range(1, IDX_HEADS):
            sc = sc + w[:, hh:hh + 1] * d[hh]
        if i == n_pages:
            sc = jnp.where(new_ok, sc, NEG_INF)
        bits = pltpu.bitcast(sc, jnp.int32)
        key = jnp.where(bits >= 0, bits, bits ^ 0x7FFFFFFF)
        key_ref[i] = jnp.where(key == -1, 0, key)

    def count(pred):
        hit = jnp.where(pred(key_ref[...]), 1.0, 0.0)
        return jnp.sum(jnp.sum(hit, axis=0), axis=1, keepdims=True)

    c0 = count(lambda kk: kk >= 0)
    tau = jnp.where(c0 >= top, 0, INT32_MIN).astype(jnp.int32)

    def bit_step(i, tau):
        cand = tau | jnp.left_shift(jnp.int32(1), 30 - i)
        return jnp.where(count(lambda kk: kk >= cand) >= top, cand, tau)

    tau = lax.fori_loop(0, 31, bit_step, tau)
    need = top - count(lambda kk: kk > tau)
    ri = lax.broadcasted_iota(jnp.int32, (PAGE_SIZE, PAGE_SIZE), 0)
    ci = lax.broadcasted_iota(jnp.int32, (PAGE_SIZE, PAGE_SIZE), 1)
    tri = jnp.where(ri <= ci, 1.0, 0.0).astype(MXU_DTYPE)
    seen = jnp.zeros((8, 1), jnp.float32)
    for i in range(n_pages + 1):
        kk = key_ref[i]
        tie = kk == tau
        tie_f = jnp.where(tie, 1.0, 0.0)
        rank = seen + jnp.dot(tie_f.astype(MXU_DTYPE), tri, preferred_element_type=jnp.float32)
        sel = (kk > tau) | (tie & (rank <= need))
        if i == n_pages:
            sel = sel & new_ok
        o_ref[0, :, i * PAGE_SIZE:(i + 1) * PAGE_SIZE] = jnp.where(sel, 0.0, NEG_INF)
        seen = seen + jnp.sum(tie_f, axis=1, keepdims=True)


def dsa_sample_select(q_idx, k_idx_new, w_idx, pool_idx, layer, page_table, top):
    n, t, hh, di = q_idx.shape
    n_pages = page_table.shape[1]
    qx = jnp.pad(jnp.transpose(q_idx, (0, 2, 1, 3)), ((0, 0), (0, 0), (0, 8 - t), (0, 0)))
    qx = qx.reshape(n, hh * 8, di).astype(MXU_DTYPE)
    w8 = jnp.pad(w_idx, ((0, 0), (0, 8 - t), (0, 0)))
    kn = _new_rows_page(k_idx_new, di)
    per_seq = lambda a, pt: (a, 0, 0)
    total = (n_pages + 1) * PAGE_SIZE
    grid_spec = pltpu.PrefetchScalarGridSpec(
        num_scalar_prefetch=1,
        grid=(n,),
        in_specs=[pl.BlockSpec((1, hh * 8, di), per_seq),
                  pl.BlockSpec((1, 8, hh), per_seq),
                  pl.BlockSpec((1, PAGE_SIZE, di), per_seq)]
                 + [pl.BlockSpec((None, None, PAGE_SIZE, di),
                                 functools.partial(lambda a, pt, i: (layer, pt[a * n_pages + i], 0, 0), i=i))
                    for i in range(n_pages)],
        out_specs=pl.BlockSpec((1, 8, total), per_seq),
        scratch_shapes=[pltpu.VMEM((n_pages + 1, 8, PAGE_SIZE), jnp.int32)],
    )
    return pl.pallas_call(
        functools.partial(_dsa_sample_select_body, n_pages=n_pages, t_new=t, top=top),
        grid_spec=grid_spec,
        out_shape=jax.ShapeDtypeStruct((n, 8, total), jnp.float32),
        compiler_params=pltpu.CompilerParams(
            dimension_semantics=("arbitrary",), vmem_limit_bytes=VMEM_LIMIT),
        name="dsa_sample_select",
    )(page_table.reshape(-1), qx, w8, kn, *([pool_idx] * n_pages))


def dsa_sample_p(h, pool_kv, pool_idx, layer, page_table, w_in, w_out, past):
    n, t, _ = h.shape
    g, r = DSA_KV_HEADS, N_HEADS // DSA_KV_HEADS
    pos = past + jnp.arange(t)
    q, kv, q_idx, k_idx, w_idx = _dsa_project(h, w_in, pos)
    total = past + t
    bias8 = dsa_sample_select(q_idx, k_idx, w_idx, pool_idx, layer, page_table, min(DSA_TOPK, total // 4))
    allow = (bias8[:, :t] == 0.0)[:, :, None, :]
    lw = g * HEAD_DIM
    pool_flat = pool_kv.reshape(pool_kv.shape[0], pool_kv.shape[1], PAGE_SIZE, 2 * lw)
    o = paged_attn(_group_lane_queries(q), pool_flat, layer, page_table, 0, 1,
                   _rows_bias(allow[..., :past], g, r),
                   _new_rows_page(kv[:, :, 0].reshape(n, t, lw), lw),
                   _new_rows_page(kv[:, :, 1].reshape(n, t, lw), lw),
                   _rows_bias(allow[..., past:], g, r))
    o = _ungroup_lanes(o, t, g, r, HEAD_DIM)
    y = o.reshape(n, t, Q_DIM) @ w_out
    return y, kv, k_idx


def _dil_body(q_ref, kp_ref, kc_ref, vp_ref, vc_ref, o_ref, lse_ref, *, tq, r):
    ui = pl.program_id(2)
    i = lax.broadcasted_iota(jnp.int32, (tq, tq), 0)
    j = lax.broadcasted_iota(jnp.int32, (tq, tq), 1)
    allow = jnp.concatenate([(j >= i) & (ui > 0), j <= i], axis=1)
    lane = lax.broadcasted_iota(jnp.int32, (tq, LANE), 1)
    lse_tile = jnp.zeros((tq, LANE), jnp.float32)
    for h in range(N_HEADS):
        gp = (h // r) // 2
        sl = slice(gp * LANE, (gp + 1) * LANE)
        k2 = jnp.concatenate([kp_ref[0, :, sl], kc_ref[0, :, sl]], axis=0)
        v2 = jnp.concatenate([vp_ref[0, :, sl], vc_ref[0, :, sl]], axis=0)
        s = lax.dot_general(q_ref[0, :, h * LANE:(h + 1) * LANE], k2, _AT_BT, preferred_element_type=jnp.float32)
        s = jnp.where(allow, s, NEG_INF)
        m = jnp.max(s, axis=1, keepdims=True)
        e = jnp.exp(s - m)
        den = jnp.maximum(jnp.sum(e, axis=1, keepdims=True), 1e-30)
        o = jnp.dot(e.astype(MXU_DTYPE), v2, preferred_element_type=jnp.float32) / den
        o_ref[0, :, h * LANE:(h + 1) * LANE] = o
        lse_tile = jnp.where(lane == h, m + jnp.log(den), lse_tile)
    lse_ref[0] = lse_tile


def dil_group_attn(q_pad, k, v, dil):
    n, t, qw = q_pad.shape
    kw = k.shape[-1]
    length = t // dil
    tq = Q_TILE
    view = lambda x: x.reshape(n, length, dil * x.shape[-1])
    cur = lambda a, c, u: (a, u, c)
    prev = lambda a, c, u: (a, jnp.maximum(u - 1, 0), c)
    o, lse = pl.pallas_call(
        functools.partial(_dil_body, tq=tq, r=N_HEADS // DIL_KV_HEADS),
        grid=(n, dil, length // tq),
        in_specs=[pl.BlockSpec((1, tq, qw), cur),
                  pl.BlockSpec((1, tq, kw), prev), pl.BlockSpec((1, tq, kw), cur),
                  pl.BlockSpec((1, tq, kw), prev), pl.BlockSpec((1, tq, kw), cur)],
        out_specs=[pl.BlockSpec((1, tq, qw), cur), pl.BlockSpec((1, tq, LANE), cur)],
        out_shape=[jax.ShapeDtypeStruct((n, length, dil * qw), jnp.float32),
                   jax.ShapeDtypeStruct((n, length, dil * LANE), jnp.float32)],
        compiler_params=pltpu.CompilerParams(
            dimension_semantics=("arbitrary", "arbitrary", "arbitrary"), vmem_limit_bytes=VMEM_LIMIT),
        name="dil_group_attn",
    )(view(q_pad), view(k), view(k), view(v), view(v))
    return o.reshape(n, t, qw), lse.reshape(n, t, LANE)


def dil_prompt_p(h, w_in, w_out):
    n, s, _ = h.shape
    g, r = DIL_KV_HEADS, N_HEADS // DIL_KV_HEADS
    q, kv = _dil_project(h, w_in, jnp.arange(s))
    half = (jnp.arange(N_HEADS) // r) % 2
    place = (half[:, None] == jnp.arange(2)[None, :]).astype(jnp.float32)
    q_pad = ((q * (HEAD_DIM ** -0.5)).reshape(n, s, N_HEADS, 1, HEAD_DIM) * place[None, None, :, :, None])
    q_pad = q_pad.reshape(n, s, N_HEADS * LANE).astype(MXU_DTYPE)
    outs, lses, bufs = [], [], []
    for gi, (win, dil) in enumerate(DIL_GROUPS):
        assert win // dil == Q_TILE
        kb = kv[:, :, gi, 0].reshape(n, s, g * HEAD_DIM).astype(MXU_DTYPE)
        vb = kv[:, :, gi, 1].reshape(n, s, g * HEAD_DIM).astype(MXU_DTYPE)
        o, lse = dil_group_attn(q_pad, kb, vb, dil)
        o = o.reshape(n, s, N_HEADS, 2, HEAD_DIM)
        o = jnp.where((half == 0)[None, None, :, None], o[:, :, :, 0], o[:, :, :, 1])
        outs.append(o.reshape(n, s, g, r, HEAD_DIM))
        lses.append(lse[:, :, :N_HEADS].reshape(n, s, g, r))
        bufs.append(_tail(kv[:, :, gi], win))
    y = _dil_mix(outs, lses).astype(h.dtype).reshape(n, s, Q_DIM) @ w_out
    return y, bufs


def kernel(x_prompt, x_sample, cache_nsa_kv, cache_nsa_win, cache_dil1, cache_dil2, cache_dil3,
           cache_dsa_kv, cache_dsa_idx, cache_swa, page_table,
           norm_mix, norm_ffn, norm_final, ffn_in, ffn_out,
           nsa_w_in, nsa_w_cmp_k, nsa_w_cmp_v, nsa_w_out,
           dil_w_in, dil_w_out, dsa_w_in, dsa_w_out,
           swa_w_in, swa_sink, swa_w_out):
    past = page_table.shape[1] * PAGE_SIZE
    hp, hs = x_prompt, x_sample
    ffn_in_b = ffn_in.astype(MXU_DTYPE)
    ffn_out_b = ffn_out.astype(MXU_DTYPE)
    st = {name: [] for name in ("nsa_kv_p", "nsa_kv_s", "nsa_win_p", "nsa_win_s",
                                "dil1_p", "dil1_s", "dil2_p", "dil2_s", "dil3_p", "dil3_s",
                                "dsa_kv_p", "dsa_kv_s", "dsa_idx_p", "dsa_idx_s", "swa_p", "swa_s")}
    for i in range(DEPTH):
        kind, j = i % N_MIXERS, i // N_MIXERS
        ap, as_ = rms_norm(hp, norm_mix[i]), rms_norm(hs, norm_mix[i])
        if kind == 0:
            yp, kv_p, win_p = nsa_prompt_p(ap, nsa_w_in[j], nsa_w_cmp_k[j], nsa_w_cmp_v[j], nsa_w_out[j])
            ys, kv_s, win_s = nsa_sample_p(as_, cache_nsa_kv, j, cache_nsa_win[j], page_table,
                                         nsa_w_in[j], nsa_w_cmp_k[j], nsa_w_cmp_v[j], nsa_w_out[j], past)
            st["nsa_kv_p"].append(kv_p)
            st["nsa_kv_s"].append(kv_s)
            st["nsa_win_p"].append(win_p)
            st["nsa_win_s"].append(win_s)
        elif kind == 1:
            yp, bufs_p = dil_prompt_p(ap, dil_w_in[j], dil_w_out[j])
            ys, bufs_s = dil_sample(as_, [cache_dil1[j], cache_dil2[j], cache_dil3[j]],
                                    dil_w_in[j], dil_w_out[j], past)
            for gi in range(len(DIL_GROUPS)):
                st["dil%d_p" % (gi + 1)].append(bufs_p[gi])
                st["dil%d_s" % (gi + 1)].append(bufs_s[gi])
        elif kind == 2:
            yp, kv_p, idx_p = dsa_prompt_p(ap, dsa_w_in[j], dsa_w_out[j])
            ys, kv_s, idx_s = dsa_sample_p(as_, cache_dsa_kv, cache_dsa_idx, j, page_table,
                                         dsa_w_in[j], dsa_w_out[j], past)
            st["dsa_kv_p"].append(kv_p)
            st["dsa_kv_s"].append(kv_s)
            st["dsa_idx_p"].append(idx_p)
            st["dsa_idx_s"].append(idx_s)
        else:
            yp, buf_p = swa_prompt(ap, swa_w_in[j], swa_sink[j], swa_w_out[j])
            ys, buf_s = swa_sample(as_, cache_swa[j], swa_w_in[j], swa_sink[j], swa_w_out[j], past)
            st["swa_p"].append(buf_p)
            st["swa_s"].append(buf_s)
        hp = hp + yp
        hs = hs + ys
        hp = ffn_residual(hp.reshape(-1, D_MODEL), norm_ffn[i], ffn_in_b[i], ffn_out_b[i]).reshape(hp.shape)
        hs = ffn_residual(hs.reshape(-1, D_MODEL), norm_ffn[i], ffn_in_b[i], ffn_out_b[i]).reshape(hs.shape)
    y_prompt = rms_norm(hp, norm_final)
    y_sample = rms_norm(hs, norm_final)
    return (y_prompt, y_sample,
            jnp.stack(st["nsa_kv_p"]), jnp.stack(st["nsa_kv_s"]),
            jnp.stack(st["nsa_win_p"]), jnp.stack(st["nsa_win_s"]),
            jnp.stack(st["dil1_p"]), jnp.stack(st["dil1_s"]),
            jnp.stack(st["dil2_p"]), jnp.stack(st["dil2_s"]),
            jnp.stack(st["dil3_p"]), jnp.stack(st["dil3_s"]),
            jnp.stack(st["dsa_kv_p"]), jnp.stack(st["dsa_kv_s"]),
            jnp.stack(st["dsa_idx_p"]), jnp.stack(st["dsa_idx_s"]),
            jnp.stack(st["swa_p"]), jnp.stack(st["swa_s"]))
```

```python
import functools

import jax, jax.numpy as jnp
from jax import lax
import numpy as np
from jax.experimental import pallas as pl
from jax.experimental.pallas import tpu as pltpu

D_MODEL = 1024
BATCH = 2
SEQ = 8192
DEPTH = 4
DEC_BATCH = 128
DEC_SEQ = 4
PAST_LEN = 8192
PAGE_SIZE = 128

HEAD_DIM = 64
N_HEADS = D_MODEL // HEAD_DIM
Q_DIM = N_HEADS * HEAD_DIM
ROPE_THETA = 10000.0
NORM_EPS = 1e-6
N_MIXERS = 4
D_FF = -(-(8 * D_MODEL) // (3 * 256)) * 256
NEG_INF = -1e30

NSA_KV_HEADS = 2
NSA_CMP_BLOCK = 32
NSA_SEL_BLOCK = 64
NSA_N_SEL = 16
NSA_WINDOW = 512
NSA_FORCE_BONUS = 1e4
NSA_IN = Q_DIM + 6 * NSA_KV_HEADS * HEAD_DIM + 3 * N_HEADS

DIL_KV_HEADS = 4
DIL_GROUPS = ((128, 1), (512, 4), (2048, 16))
DIL_IN = Q_DIM + len(DIL_GROUPS) * 2 * DIL_KV_HEADS * HEAD_DIM

DSA_KV_HEADS = 4
IDX_HEADS = 8
IDX_DIM = 64
DSA_TOPK = 256
IDX_SCALE = (IDX_DIM * IDX_HEADS) ** -0.5
DSA_IN = Q_DIM + 2 * DSA_KV_HEADS * HEAD_DIM + IDX_HEADS * IDX_DIM + IDX_DIM + IDX_HEADS

SWA_KV_HEADS = 2
SWA_WINDOW = 128
SWA_IN = Q_DIM + 2 * SWA_KV_HEADS * HEAD_DIM


def rms_norm(x, g):
    xf = x.astype(jnp.float32)
    y = xf * lax.rsqrt(jnp.mean(xf * xf, axis=-1, keepdims=True) + NORM_EPS)
    return (y * g.astype(jnp.float32)).astype(x.dtype)


def rope(x, pos):
    half = x.shape[-1] // 2
    inv = ROPE_THETA ** (-jnp.arange(half, dtype=jnp.float32) / half)
    ang = pos.astype(jnp.float32)[:, None] * inv[None, :]
    shape = (pos.shape[0],) + (1,) * (x.ndim - 3) + (half,)
    cos, sin = jnp.cos(ang).reshape(shape), jnp.sin(ang).reshape(shape)
    xf = x.astype(jnp.float32)
    x1, x2 = xf[..., :half], xf[..., half:]
    return jnp.concatenate([x1 * cos - x2 * sin, x2 * cos + x1 * sin], axis=-1).astype(x.dtype)


def _tail(rows, window):
    n = rows.shape[1]
    return rows[:, n - min(window, n):]


def _masked_softmax(s, mask, sink=None):
    s = jnp.where(mask, s, NEG_INF)
    m = jnp.max(s, axis=-1, keepdims=True)
    if sink is not None:
        m = jnp.maximum(m, sink)
    e = jnp.where(mask, jnp.exp(s - m), 0.0)
    den = jnp.sum(e, axis=-1, keepdims=True)
    if sink is not None:
        den = den + jnp.exp(sink - m)
    den = jnp.maximum(den, 1e-30)
    return e / den, (m + jnp.log(den))[..., 0]


def _attend_dense(q, k, v, mask, sink=None):
    s = jnp.einsum('nqgrd,nkgd->ngrqk', q, k, preferred_element_type=jnp.float32) * (q.shape[-1] ** -0.5)
    sk = None if sink is None else sink.astype(jnp.float32)[None, :, :, None, None]
    p, lse = _masked_softmax(s, mask[:, None, None], sk)
    o = jnp.einsum('ngrqk,nkgd->nqgrd', p.astype(v.dtype), v)
    return o, jnp.transpose(lse, (0, 3, 1, 2))


def _attend_gathered(q, k, v, mask):
    s = jnp.einsum('nqgrd,nqkgd->nqgrk', q, k, preferred_element_type=jnp.float32) * (q.shape[-1] ** -0.5)
    p, lse = _masked_softmax(s, jnp.swapaxes(mask, 2, 3)[:, :, :, None, :])
    o = jnp.einsum('nqgrk,nqkgd->nqgrd', p.astype(v.dtype), v)
    return o, lse


def _nsa_project(z, pos):
    n, t, _ = z.shape
    g, r = NSA_KV_HEADS, N_HEADS // NSA_KV_HEADS
    kv_end = Q_DIM + 6 * g * HEAD_DIM
    q = rope(z[..., :Q_DIM].reshape(n, t, g, r, HEAD_DIM), pos)
    kv = z[..., Q_DIM:kv_end].reshape(n, t, 3, 2, g, HEAD_DIM)
    kv = jnp.stack([rope(kv[:, :, :, 0], pos), kv[:, :, :, 1]], axis=3).reshape(n, t, 6, g, HEAD_DIM)
    gates = jax.nn.sigmoid(z[..., kv_end:].astype(jnp.float32)).reshape(n, t, g, r, 3)
    return q, kv, gates


def _nsa_compress(rows, w_ck, w_cv):
    n, length, _, g, dh = rows.shape
    nb = length // NSA_CMP_BLOCK
    r = rows[:, :nb * NSA_CMP_BLOCK].reshape(n, nb, NSA_CMP_BLOCK, 2, g, dh)
    ck = jnp.einsum('nbcgd,gcde->nbge', r[:, :, :, 0], w_ck)
    cv = jnp.einsum('nbcgd,gcde->nbge', r[:, :, :, 1], w_cv)
    return ck, cv


def _nsa_cmp_attend(q, ck, cv, qpos):
    blk_end = (jnp.arange(ck.shape[1]) + 1) * NSA_CMP_BLOCK - 1
    mask = (blk_end[None, :] <= qpos[:, None])[None, :, None, None, :]
    s = jnp.einsum('nqgrd,ncgd->nqgrc', q, ck, preferred_element_type=jnp.float32) * (q.shape[-1] ** -0.5)
    p, _ = _masked_softmax(s, mask)
    o = jnp.einsum('nqgrc,ncgd->nqgrd', p.astype(cv.dtype), cv)
    return o, p


def _nsa_gate(gates, o_cmp, o_slc, o_win):
    f = jnp.float32
    o = (gates[..., 0:1] * o_cmp.astype(f) + gates[..., 1:2] * o_slc.astype(f)
         + gates[..., 2:3] * o_win.astype(f))
    n, t = o.shape[:2]
    return o.reshape(n, t, Q_DIM)


def _dil_project(z, pos):
    n, t, _ = z.shape
    g = DIL_KV_HEADS
    q = rope(z[..., :Q_DIM].reshape(n, t, g, N_HEADS // g, HEAD_DIM), pos)
    kv = z[..., Q_DIM:].reshape(n, t, len(DIL_GROUPS), 2, g, HEAD_DIM)
    return q, jnp.stack([rope(kv[:, :, :, 0], pos), kv[:, :, :, 1]], axis=3)


def _dil_mix(outs, lses):
    w = jax.nn.softmax(jnp.stack(lses), axis=0)
    return jnp.sum(w[..., None] * jnp.stack(outs).astype(jnp.float32), axis=0)


def _dsa_project(z, pos):
    n, t, _ = z.shape
    g = DSA_KV_HEADS
    o1 = Q_DIM
    o2 = o1 + 2 * g * HEAD_DIM
    o3 = o2 + IDX_HEADS * IDX_DIM
    o4 = o3 + IDX_DIM
    q = rope(z[..., :o1].reshape(n, t, g, N_HEADS // g, HEAD_DIM), pos)
    kv = z[..., o1:o2].reshape(n, t, 2, g, HEAD_DIM)
    kv = jnp.stack([rope(kv[:, :, 0], pos), kv[:, :, 1]], axis=2)
    q_idx = rope(z[..., o2:o3].reshape(n, t, IDX_HEADS, IDX_DIM), pos)
    k_idx = rope(z[..., o3:o4], pos)
    w_idx = z[..., o4:].astype(jnp.float32) * IDX_SCALE
    return q, kv, q_idx, k_idx, w_idx


def _swa_project(z, pos):
    n, t, _ = z.shape
    g = SWA_KV_HEADS
    q = rope(z[..., :Q_DIM].reshape(n, t, g, N_HEADS // g, HEAD_DIM), pos)
    kv = z[..., Q_DIM:].reshape(n, t, 2, g, HEAD_DIM)
    return q, jnp.stack([rope(kv[:, :, 0], pos), kv[:, :, 1]], axis=2)


MXU_DTYPE = jnp.bfloat16
BIAS_DTYPE = jnp.bfloat16
ROW_TILE = 512
FF_CHUNK = 256
VMEM_LIMIT = 56 * 1024 * 1024
LANE = 128
PAGES_PER_CHUNK = 32
Q_TILE = 128
K_TILE = 512
FLASH_ROWS = 1024
TOPK_ROWS = 256
REMOVED = -3e38
INT32_MIN = -2 ** 31
_AT_BT = (((1,), (1,)), ((), ()))


def _ffn_body(x_ref, g_ref, win_ref, wout_ref, o_ref):
    x = x_ref[...]
    xn = x * lax.rsqrt(jnp.mean(x * x, axis=-1, keepdims=True) + NORM_EPS) * g_ref[...]
    xb = xn.astype(MXU_DTYPE)
    acc = x
    for j in range(D_FF // FF_CHUNK):
        lo = j * FF_CHUNK
        gate = jnp.dot(xb, win_ref[:, lo:lo + FF_CHUNK], preferred_element_type=jnp.float32)
        up = jnp.dot(xb, win_ref[:, D_FF + lo:D_FF + lo + FF_CHUNK], preferred_element_type=jnp.float32)
        a = (gate * jax.nn.sigmoid(gate) * up).astype(MXU_DTYPE)
        acc = acc + jnp.dot(a, wout_ref[lo:lo + FF_CHUNK, :], preferred_element_type=jnp.float32)
    o_ref[...] = acc


def ffn_residual(x, g, w_in_bf16, w_out_bf16):
    rows = x.shape[0]
    return pl.pallas_call(
        _ffn_body,
        grid=(rows // ROW_TILE,),
        in_specs=[
            pl.BlockSpec((ROW_TILE, D_MODEL), lambda i: (i, 0)),
            pl.BlockSpec((1, D_MODEL), lambda i: (0, 0)),
            pl.BlockSpec((D_MODEL, 2 * D_FF), lambda i: (0, 0)),
            pl.BlockSpec((D_FF, D_MODEL), lambda i: (0, 0)),
        ],
        out_specs=pl.BlockSpec((ROW_TILE, D_MODEL), lambda i: (i, 0)),
        out_shape=jax.ShapeDtypeStruct((rows, D_MODEL), jnp.float32),
        compiler_params=pltpu.CompilerParams(
            dimension_semantics=("arbitrary",), vmem_limit_bytes=VMEM_LIMIT),
        name="ffn_residual",
    )(x, g.reshape(1, D_MODEL), w_in_bf16, w_out_bf16)


def _norm_proj_body(x_ref, g_ref, w_ref, o_ref):
    x = x_ref[...]
    xn = x * lax.rsqrt(jnp.mean(x * x, axis=-1, keepdims=True) + NORM_EPS) * g_ref[...]
    o_ref[...] = jnp.dot(xn.astype(MXU_DTYPE), w_ref[...], preferred_element_type=jnp.float32)


def norm_proj(x, g, w_b):
    rows, n_out = x.shape[0], w_b.shape[1]
    return pl.pallas_call(
        _norm_proj_body,
        grid=(rows // ROW_TILE,),
        in_specs=[pl.BlockSpec((ROW_TILE, D_MODEL), lambda i: (i, 0)),
                  pl.BlockSpec((1, D_MODEL), lambda i: (0, 0)),
                  pl.BlockSpec((D_MODEL, n_out), lambda i: (0, 0))],
        out_specs=pl.BlockSpec((ROW_TILE, n_out), lambda i: (i, 0)),
        out_shape=jax.ShapeDtypeStruct((rows, n_out), jnp.float32),
        compiler_params=pltpu.CompilerParams(dimension_semantics=("arbitrary",), vmem_limit_bytes=VMEM_LIMIT),
        name="norm_proj",
    )(x, g.reshape(1, D_MODEL), w_b)


def _proj_residual_body(a_ref, w_ref, r_ref, o_ref):
    o_ref[...] = r_ref[...] + jnp.dot(a_ref[...].astype(MXU_DTYPE), w_ref[...], preferred_element_type=jnp.float32)


def proj_residual(a, w_b, res):
    rows = a.shape[0]
    return pl.pallas_call(
        _proj_residual_body,
        grid=(rows // ROW_TILE,),
        in_specs=[pl.BlockSpec((ROW_TILE, Q_DIM), lambda i: (i, 0)),
                  pl.BlockSpec((Q_DIM, D_MODEL), lambda i: (0, 0)),
                  pl.BlockSpec((ROW_TILE, D_MODEL), lambda i: (i, 0))],
        out_specs=pl.BlockSpec((ROW_TILE, D_MODEL), lambda i: (i, 0)),
        out_shape=jax.ShapeDtypeStruct((rows, D_MODEL), jnp.float32),
        compiler_params=pltpu.CompilerParams(dimension_semantics=("arbitrary",), vmem_limit_bytes=VMEM_LIMIT),
        name="proj_residual",
    )(a, w_b, res)


def _causal_tiles(qi, tq, tk):
    return (qi * tq + tq + tk - 1) // tk


def _mflash_body(qt_ref, k_ref, vt_ref, bt_ref, o_ref, m_ref, l_ref, acc_ref, *, r, tq, tk):
    qi = pl.program_id(2)
    qt = qt_ref[0, 0, 0]
    m_ref[...] = jnp.full(m_ref.shape, NEG_INF, jnp.float32)
    l_ref[...] = jnp.zeros(l_ref.shape, jnp.float32)
    acc_ref[...] = jnp.zeros(acc_ref.shape, jnp.float32)

    def step(kt, carry):
        ks = pl.multiple_of(kt * tk, tk)
        k = k_ref[0, 0, pl.ds(ks, tk), :]
        vt = vt_ref[0, 0, :, pl.ds(ks, tk)]
        bias = bt_ref[0, 0, kt, 0].astype(jnp.float32)
        s = jnp.dot(k, qt, preferred_element_type=jnp.float32) + jnp.concatenate([bias] * r, axis=1)
        m_old = m_ref[...]
        m_new = jnp.maximum(m_old, jnp.max(s, axis=0, keepdims=True))
        p = jnp.exp(s - m_new)
        alpha = jnp.exp(m_old - m_new)
        l_ref[...] = alpha * l_ref[...] + jnp.sum(p, axis=0, keepdims=True)
        acc_ref[...] = alpha * acc_ref[...] + jnp.dot(vt, p.astype(MXU_DTYPE), preferred_element_type=jnp.float32)
        m_ref[...] = m_new
        return carry

    lax.fori_loop(0, _causal_tiles(qi, tq, tk), step, 0)
    o_ref[0, 0, 0] = acc_ref[...] / jnp.maximum(l_ref[...], 1e-30)


def masked_flash(q, k, v, allow_t):
    n, t, g, r, dh = q.shape
    gb = allow_t.shape[1]
    tq, tk = FLASH_ROWS // r, K_TILE
    nq, nkt = t // tq, t // tk
    qs = (q * (dh ** -0.5)).reshape(n, nq, tq, g, r, dh)
    qt = jnp.transpose(qs, (0, 3, 1, 5, 4, 2)).reshape(n, g, nq, dh, r * tq).astype(MXU_DTYPE)
    kb = jnp.transpose(k, (0, 2, 1, 3)).astype(MXU_DTYPE)
    vt = jnp.transpose(v, (0, 2, 3, 1)).astype(MXU_DTYPE)
    bt = jnp.where(allow_t, 0.0, NEG_INF).astype(BIAS_DTYPE).reshape(n, gb, nkt, tk, nq, tq)
    bt = jnp.transpose(bt, (0, 1, 2, 4, 3, 5))
    bmap = (lambda a, b, c: (a, b, 0, c, 0, 0)) if gb == g else (lambda a, b, c: (a, 0, 0, c, 0, 0))
    o = pl.pallas_call(
        functools.partial(_mflash_body, r=r, tq=tq, tk=tk),
        grid=(n, g, nq),
        in_specs=[
            pl.BlockSpec((1, 1, 1, dh, r * tq), lambda a, b, c: (a, b, c, 0, 0)),
            pl.BlockSpec((1, 1, t, dh), lambda a, b, c: (a, b, 0, 0)),
            pl.BlockSpec((1, 1, dh, t), lambda a, b, c: (a, b, 0, 0)),
            pl.BlockSpec((1, 1, nkt, 1, tk, tq), bmap),
        ],
        out_specs=pl.BlockSpec((1, 1, 1, dh, r * tq), lambda a, b, c: (a, b, c, 0, 0)),
        out_shape=jax.ShapeDtypeStruct((n, g, nq, dh, r * tq), jnp.float32),
        scratch_shapes=[pltpu.VMEM((1, r * tq), jnp.float32),
                        pltpu.VMEM((1, r * tq), jnp.float32),
                        pltpu.VMEM((dh, r * tq), jnp.float32)],
        compiler_params=pltpu.CompilerParams(
            dimension_semantics=("arbitrary", "arbitrary", "arbitrary"), vmem_limit_bytes=VMEM_LIMIT),
        name="masked_flash",
    )(qt, kb, vt, bt)
    o = o.reshape(n, g, nq, dh, r, tq)
    return jnp.transpose(o, (0, 2, 5, 1, 4, 3)).reshape(n, t, g, r, dh)


def _dsa_select_body(qx_ref, kx_ref, w_ref, o_ref, key_ref, *, tq, tk, top):
    qi = pl.program_id(1)
    nkt = key_ref.shape[0]
    n_valid = _causal_tiles(qi, tq, tk)
    qx = qx_ref[0].reshape(IDX_HEADS * tq, IDX_DIM)
    w = w_ref[0]
    row = qi * tq + lax.broadcasted_iota(jnp.int32, (tq, 1), 0)

    def causal(kt):
        col = kt * tk + lax.broadcasted_iota(jnp.int32, (1, tk), 1)
        return col <= row

    def score_step(kt, carry):
        ks = pl.multiple_of(kt * tk, tk)
        d = lax.dot_general(qx, kx_ref[0, pl.ds(ks, tk), :], _AT_BT, preferred_element_type=jnp.float32)
        d = jnp.maximum(d, 0.0).reshape(IDX_HEADS, tq, tk)
        sc = w[:, 0:1] * d[0]
        for h in range(1, IDX_HEADS):
            sc = sc + w[:, h:h + 1] * d[h]
        sc = jnp.where(causal(kt), sc, NEG_INF)
        bits = pltpu.bitcast(sc, jnp.int32)
        key = jnp.where(bits >= 0, bits, bits ^ 0x7FFFFFFF)
        key_ref[kt] = jnp.where(key == -1, 0, key)
        return carry

    lax.fori_loop(0, n_valid, score_step, 0)

    def count(bound, strict):
        bb = jnp.broadcast_to(bound, (tq, LANE))

        def body(kt, c):
            for b in range(tk // LANE):
                kk = key_ref[kt, :, b * LANE:(b + 1) * LANE]
                c = c + jnp.where((kk > bb) if strict else (kk >= bb), 1.0, 0.0)
            return c
        c = lax.fori_loop(0, n_valid, body, jnp.zeros((tq, LANE), jnp.float32))
        return jnp.sum(c, axis=1, keepdims=True)

    c0 = count(jnp.zeros((tq, 1), jnp.int32), False)
    tau = jnp.where(c0 >= top, 0, INT32_MIN).astype(jnp.int32)

    def bit_step(i, tau):
        cand = tau | jnp.left_shift(jnp.int32(1), 30 - i)
        return jnp.where(count(cand, False) >= top, cand, tau)

    tau = lax.fori_loop(0, 31, bit_step, tau)
    need = top - count(tau, True)
    ri = lax.broadcasted_iota(jnp.int32, (tk, tk), 0)
    ci = lax.broadcasted_iota(jnp.int32, (tk, tk), 1)
    tri = jnp.where(ri <= ci, 1.0, 0.0).astype(MXU_DTYPE)

    def out_step(kt, seen):
        kk = key_ref[kt]
        tie = kk == tau
        tie_f = jnp.where(tie, 1.0, 0.0)
        rank = seen + jnp.dot(tie_f.astype(MXU_DTYPE), tri, preferred_element_type=jnp.float32)
        sel = (kk > tau) | (tie & (rank <= need))
        o_ref[0, kt] = jnp.where(sel & causal(kt), 0.0, NEG_INF).astype(o_ref.dtype)
        return seen + jnp.sum(tie_f, axis=1, keepdims=True)

    lax.fori_loop(0, n_valid, out_step, jnp.zeros((tq, 1), jnp.float32))

    def fill_step(kt, carry):
        o_ref[0, kt] = jnp.full((tq, tk), NEG_INF, o_ref.dtype)
        return carry

    lax.fori_loop(n_valid, nkt, fill_step, 0)


def dsa_select_bias(q_idx, k_idx, w_idx, top):
    n, h, t, di = q_idx.shape
    nkt = t // K_TILE
    return pl.pallas_call(
        functools.partial(_dsa_select_body, tq=Q_TILE, tk=K_TILE, top=top),
        grid=(n, t // Q_TILE),
        in_specs=[
            pl.BlockSpec((1, h, Q_TILE, di), lambda a, c: (a, 0, c, 0)),
            pl.BlockSpec((1, t, di), lambda a, c: (a, 0, 0)),
            pl.BlockSpec((1, Q_TILE, h), lambda a, c: (a, c, 0)),
        ],
        out_specs=pl.BlockSpec((1, nkt, Q_TILE, K_TILE), lambda a, c: (a, 0, c, 0)),
        out_shape=jax.ShapeDtypeStruct((n, nkt, t, K_TILE), BIAS_DTYPE),
        scratch_shapes=[pltpu.VMEM((nkt, Q_TILE, K_TILE), jnp.int32)],
        compiler_params=pltpu.CompilerParams(
            dimension_semantics=("arbitrary", "arbitrary"), vmem_limit_bytes=VMEM_LIMIT),
        name="dsa_select",
    )(q_idx, k_idx, w_idx)


def _topk_mask_body(s_ref, o_ref, *, k):
    s0 = s_ref[...]
    lane = lax.broadcasted_iota(jnp.int32, s0.shape, 1).astype(jnp.float32)
    width = float(s0.shape[1])

    def step(_, carry):
        s, sel = carry
        m = jnp.max(s, axis=1, keepdims=True)
        first = jnp.min(jnp.where(s == m, lane, width), axis=1, keepdims=True)
        hit = lane == first
        return jnp.where(hit, REMOVED, s), jnp.where(hit, 1.0, sel)

    _, sel = lax.fori_loop(0, k, step, (s0, jnp.zeros(s0.shape, jnp.float32)))
    o_ref[...] = sel


def topk_mask(score, k):
    lead, n = score.shape[:-1], score.shape[-1]
    rows = int(np.prod(lead))
    width = -(-n // LANE) * LANE
    s2 = jnp.pad(score.reshape(rows, n), ((0, 0), (0, width - n)), constant_values=REMOVED)
    tm = min(TOPK_ROWS, rows)
    sel = pl.pallas_call(
        functools.partial(_topk_mask_body, k=k),
        grid=(rows // tm,),
        in_specs=[pl.BlockSpec((tm, width), lambda i: (i, 0))],
        out_specs=pl.BlockSpec((tm, width), lambda i: (i, 0)),
        out_shape=jax.ShapeDtypeStruct((rows, width), jnp.float32),
        compiler_params=pltpu.CompilerParams(dimension_semantics=("arbitrary",)),
        name="topk_mask",
    )(s2)
    return (sel[:, :n] > 0.5).reshape(lead + (n,))


def _nsa_select_mask(p_cmp, qpos, n_slc):
    n, t, g = p_cmp.shape[:3]
    ratio = NSA_SEL_BLOCK // NSA_CMP_BLOCK
    imp = p_cmp.sum(axis=3)
    imp = jnp.pad(imp, ((0, 0), (0, 0), (0, 0), (0, n_slc * ratio - imp.shape[-1])))
    imp = imp.reshape(n, t, g, n_slc, ratio).sum(-1)
    j = jnp.arange(n_slc)[None, :]
    cur = (qpos // NSA_SEL_BLOCK)[:, None]
    forced = ((j == 0) | (j == cur) | (j == cur - 1))[None, :, None, :]
    future = (j > cur)[None, :, None, :]
    score = jnp.where(future, NEG_INF, jnp.where(forced, imp + NSA_FORCE_BONUS, imp))
    return topk_mask(score, min(NSA_N_SEL, n_slc))


def _compress_body(*refs, n_pages):
    if n_pages:
        pages, (w_ref, o_ref, x_ref) = refs[1:1 + n_pages], refs[1 + n_pages:]
        srcs = [(pages[i], i * PAGE_SIZE) for i in range(n_pages)]
    else:
        in_ref, w_ref, o_ref, x_ref = refs
        srcs = [(in_ref.at[0], 0)]
    n_planes = x_ref.shape[0]
    for src, row0 in srcs:
        for pi in range(n_planes):
            x_ref[pi, row0:row0 + src.shape[0], :] = src[:, pi * LANE:(pi + 1) * LANE]
    nb = o_ref.shape[1]
    acc = jnp.zeros(o_ref.shape[1:], jnp.float32)
    for c in range(NSA_CMP_BLOCK):
        rows_c = jnp.concatenate([x_ref[pi, pl.ds(c, nb, stride=NSA_CMP_BLOCK), :] for pi in range(n_planes)],
                                 axis=1).astype(MXU_DTYPE)
        acc = acc + jnp.dot(rows_c, w_ref[c], preferred_element_type=jnp.float32)
    o_ref[0] = acc


def _compress_weights(w_ck, w_cv):
    w2 = jnp.stack([w_ck, w_cv])
    g = w_ck.shape[0]
    eye = jnp.eye(2 * g, dtype=w2.dtype).reshape(2, g, 2, g)
    wbd = jnp.einsum('sgcde,sgtk->csgdtke', w2, eye)
    width = 2 * g * HEAD_DIM
    return wbd.reshape(NSA_CMP_BLOCK, width, width).astype(MXU_DTYPE)


def _split_summaries(out, g):
    n, nb, _ = out.shape
    half = g * HEAD_DIM
    return out[..., :half].reshape(n, nb, g, HEAD_DIM), out[..., half:].reshape(n, nb, g, HEAD_DIM)


def nsa_compress_rows(kv_rows, w_ck, w_cv):
    n, length, _ = kv_rows.shape
    g = w_ck.shape[0]
    width = 2 * g * HEAD_DIM
    nb = length // NSA_CMP_BLOCK
    out = pl.pallas_call(
        functools.partial(_compress_body, n_pages=0),
        grid=(n,),
        in_specs=[pl.BlockSpec((1, length, width), lambda a: (a, 0, 0)),
                  pl.BlockSpec((NSA_CMP_BLOCK, width, width), lambda a: (0, 0, 0))],
        out_specs=pl.BlockSpec((1, nb, width), lambda a: (a, 0, 0)),
        out_shape=jax.ShapeDtypeStruct((n, nb, width), jnp.float32),
        scratch_shapes=[pltpu.VMEM((width // LANE, length, LANE), jnp.float32)],
        compiler_params=pltpu.CompilerParams(dimension_semantics=("arbitrary",), vmem_limit_bytes=VMEM_LIMIT),
        name="nsa_compress_rows",
    )(kv_rows, _compress_weights(w_ck, w_cv))
    return _split_summaries(out, g)


def nsa_compress_paged(pool, layer, page_table, w_ck, w_cv):
    n, n_pages = page_table.shape
    g = w_ck.shape[0]
    width = 2 * g * HEAD_DIM
    nb = n_pages * PAGE_SIZE // NSA_CMP_BLOCK
    pool_flat = pool.reshape(pool.shape[0], pool.shape[1], PAGE_SIZE, 2 * width)
    grid_spec = pltpu.PrefetchScalarGridSpec(
        num_scalar_prefetch=1,
        grid=(n,),
        in_specs=[pl.BlockSpec((None, None, PAGE_SIZE, width),
                               functools.partial(lambda a, pt, i: (layer, pt[a * n_pages + i], 0, 0), i=i))
                  for i in range(n_pages)]
                 + [pl.BlockSpec((NSA_CMP_BLOCK, width, width), lambda a, pt: (0, 0, 0))],
        out_specs=pl.BlockSpec((1, nb, width), lambda a, pt: (a, 0, 0)),
        scratch_shapes=[pltpu.VMEM((width // LANE, n_pages * PAGE_SIZE, LANE), jnp.float32)],
    )
    out = pl.pallas_call(
        functools.partial(_compress_body, n_pages=n_pages),
        grid_spec=grid_spec,
        out_shape=jax.ShapeDtypeStruct((n, nb, width), jnp.float32),
        compiler_params=pltpu.CompilerParams(dimension_semantics=("arbitrary",), vmem_limit_bytes=VMEM_LIMIT),
        name="nsa_compress_paged",
    )(page_table.reshape(-1), *([pool_flat] * n_pages), _compress_weights(w_ck, w_cv))
    return _split_summaries(out, g)


def _band_body(*refs, tq, n_prev, span, r, has_sink):
    nk = n_prev + 1
    q_ref, k_refs, v_refs = refs[0], refs[1:1 + nk], refs[1 + nk:1 + 2 * nk]
    sink_ref = refs[1 + 2 * nk] if has_sink else None
    o_ref, lse_ref = refs[-2:]
    ui = pl.program_id(2)
    i = lax.broadcasted_iota(jnp.int32, (tq, tq), 0)
    j = lax.broadcasted_iota(jnp.int32, (tq, tq), 1)
    parts = []
    for p in range(n_prev, -1, -1):
        diff = p * tq + i - j
        parts.append((diff >= 0) & (diff < span) & (ui >= p))
    allow = jnp.concatenate(parts, axis=1)
    lane = lax.broadcasted_iota(jnp.int32, (tq, LANE), 1)
    lse_tile = jnp.zeros((tq, LANE), jnp.float32)
    for h in range(N_HEADS):
        gp = (h // r) // 2
        sl = slice(gp * LANE, (gp + 1) * LANE)
        k2 = jnp.concatenate([kr[0, :, sl] for kr in k_refs], axis=0)
        v2 = jnp.concatenate([vr[0, :, sl] for vr in v_refs], axis=0)
        s = lax.dot_general(q_ref[0, :, h * LANE:(h + 1) * LANE], k2, _AT_BT, preferred_element_type=jnp.float32)
        s = jnp.where(allow, s, NEG_INF)
        m = jnp.max(s, axis=1, keepdims=True)
        if has_sink:
            m = jnp.maximum(m, sink_ref[h])
        e = jnp.exp(s - m)
        den = jnp.sum(e, axis=1, keepdims=True)
        if has_sink:
            den = den + jnp.exp(sink_ref[h] - m)
        den = jnp.maximum(den, 1e-30)
        o_ref[0, :, h * LANE:(h + 1) * LANE] = jnp.dot(e.astype(MXU_DTYPE), v2, preferred_element_type=jnp.float32) / den
        lse_tile = jnp.where(lane == h, m + jnp.log(den), lse_tile)
    lse_ref[0] = lse_tile


def band_attn(q, k, v, dil, span, sink=None):
    n, t, g, r, dh = q.shape
    tq = Q_TILE
    n_prev = -(-(span - 1) // tq)
    half = (jnp.arange(N_HEADS) // r) % 2
    place = (half[:, None] == jnp.arange(2)[None, :]).astype(jnp.float32)
    q_pad = (q * (dh ** -0.5)).reshape(n, t, N_HEADS, 1, dh) * place[None, None, :, :, None]
    q_pad = q_pad.reshape(n, t, N_HEADS * LANE).astype(MXU_DTYPE)
    kb = k.reshape(n, t, g * dh).astype(MXU_DTYPE)
    vb = v.reshape(n, t, g * dh).astype(MXU_DTYPE)
    qw, kw = N_HEADS * LANE, g * dh
    length = t // dil
    view = lambda x: x.reshape(n, length, dil * x.shape[-1])
    cur = lambda a, c, u: (a, u, c)
    back = [functools.partial(lambda a, c, u, p: (a, jnp.maximum(u - p, 0), c), p=p) for p in range(n_prev, -1, -1)]
    in_specs = ([pl.BlockSpec((1, tq, qw), cur)] + [pl.BlockSpec((1, tq, kw), b) for b in back] * 2)
    operands = [view(q_pad)] + [view(kb)] * (n_prev + 1) + [view(vb)] * (n_prev + 1)
    if sink is not None:
        in_specs.append(pl.BlockSpec(memory_space=pltpu.SMEM))
        operands.append(sink.astype(jnp.float32))
    o, lse = pl.pallas_call(
        functools.partial(_band_body, tq=tq, n_prev=n_prev, span=span, r=r, has_sink=sink is not None),
        grid=(n, dil, length // tq),
        in_specs=in_specs,
        out_specs=[pl.BlockSpec((1, tq, qw), cur), pl.BlockSpec((1, tq, LANE), cur)],
        out_shape=[jax.ShapeDtypeStruct((n, length, dil * qw), jnp.float32),
                   jax.ShapeDtypeStruct((n, length, dil * LANE), jnp.float32)],
        compiler_params=pltpu.CompilerParams(
            dimension_semantics=("arbitrary", "arbitrary", "arbitrary"), vmem_limit_bytes=VMEM_LIMIT),
        name="band_attn",
    )(*operands)
    o = o.reshape(n, t, N_HEADS, 2, dh)
    o = jnp.where((half == 0)[None, None, :, None], o[:, :, :, 0], o[:, :, :, 1])
    return o.reshape(n, t, g, r, dh), lse.reshape(n, t, LANE)[:, :, :N_HEADS].reshape(n, t, g, r)


def nsa_prompt_p(z, w_ck, w_cv):
    n, s, _ = z.shape
    pos = jnp.arange(s)
    q, kv, gates = _nsa_project(z, pos)
    ck, cv = nsa_compress_rows(kv.reshape(n, s, -1), w_ck, w_cv)
    o_cmp, p_cmp = _nsa_cmp_attend(q, ck, cv, pos)
    n_slc = -(-s // NSA_SEL_BLOCK)
    blk = _nsa_select_mask(p_cmp, pos, n_slc)
    blk_t = jnp.repeat(jnp.transpose(blk, (0, 2, 3, 1)), NSA_SEL_BLOCK, axis=2)[:, :, :s]
    allow_t = blk_t & (pos[:, None] <= pos[None, :])
    o_slc = masked_flash(q, kv[:, :, 2], kv[:, :, 3], allow_t)
    o_win, _ = band_attn(q, kv[:, :, 4], kv[:, :, 5], 1, NSA_WINDOW)
    return _nsa_gate(gates, o_cmp, o_slc, o_win), kv[:, :, 0:4], _tail(kv[:, :, 4:6], NSA_WINDOW)


def dsa_prompt_p(z):
    n, s, _ = z.shape
    pos = jnp.arange(s)
    q, kv, q_idx, k_idx, w_idx = _dsa_project(z, pos)
    top = min(DSA_TOPK, s // 4)
    bias = dsa_select_bias(jnp.transpose(q_idx, (0, 2, 1, 3)).astype(MXU_DTYPE),
                           k_idx.astype(MXU_DTYPE), w_idx, top)
    allow_t = (jnp.transpose(bias, (0, 1, 3, 2)) == 0).reshape(n, 1, s, s)
    o = masked_flash(q, kv[:, :, 0], kv[:, :, 1], allow_t)
    return o.reshape(n, s, Q_DIM), kv, k_idx


def dil_prompt_p(z):
    n, s, _ = z.shape
    q, kv = _dil_project(z, jnp.arange(s))
    outs, lses, bufs = [], [], []
    for gi, (win, dil) in enumerate(DIL_GROUPS):
        o, lse = band_attn(q, kv[:, :, gi, 0], kv[:, :, gi, 1], dil, win // dil + 1)
        outs.append(o)
        lses.append(lse)
        bufs.append(_tail(kv[:, :, gi], win))
    return _dil_mix(outs, lses).astype(z.dtype).reshape(n, s, Q_DIM), bufs


def swa_prompt_p(z, sink):
    n, s, _ = z.shape
    q, kv = _swa_project(z, jnp.arange(s))
    o, _ = band_attn(q, kv[:, :, 0], kv[:, :, 1], 1, SWA_WINDOW, sink)
    return o.reshape(n, s, Q_DIM), _tail(kv, SWA_WINDOW)


def _flash_update(s, pv, m_ref, l_ref, acc_ref):
    m_old = m_ref[...]
    m_new = jnp.maximum(m_old, jnp.max(s, axis=1, keepdims=True))
    p = jnp.exp(s - m_new)
    alpha = jnp.exp(m_old - m_new)
    l_ref[...] = alpha * l_ref[...] + jnp.sum(p, axis=1, keepdims=True)
    acc_ref[...] = alpha * acc_ref[...] + pv(p.astype(MXU_DTYPE))
    m_ref[...] = m_new


def _paged_attn_body(pt_ref, q_ref, bn_ref, kn_ref, vn_ref, b_ref, *rest, ppc):
    kp, vp = rest[:ppc], rest[ppc:2 * ppc]
    o_ref, m_ref, l_ref, acc_ref = rest[2 * ppc:]
    c = pl.program_id(1)
    q = q_ref[0]

    @pl.when(c == 0)
    def _():
        m_ref[...] = jnp.full(m_ref.shape, NEG_INF, jnp.float32)
        l_ref[...] = jnp.zeros(l_ref.shape, jnp.float32)
        acc_ref[...] = jnp.zeros(acc_ref.shape, jnp.float32)
        s = lax.dot_general(q, kn_ref[0], _AT_BT, preferred_element_type=jnp.float32) + bn_ref[0].astype(jnp.float32)
        _flash_update(s, lambda p: jnp.dot(p, vn_ref[0], preferred_element_type=jnp.float32), m_ref, l_ref, acc_ref)

    parts = [lax.dot_general(q, kp[i][...].astype(MXU_DTYPE), _AT_BT, preferred_element_type=jnp.float32)
             for i in range(ppc)]
    s = jnp.concatenate(parts, axis=1) + b_ref[0].astype(jnp.float32)

    def pv(p):
        acc = jnp.dot(p[:, 0:PAGE_SIZE], vp[0][...].astype(MXU_DTYPE), preferred_element_type=jnp.float32)
        for i in range(1, ppc):
            acc = acc + jnp.dot(p[:, i * PAGE_SIZE:(i + 1) * PAGE_SIZE], vp[i][...].astype(MXU_DTYPE),
                                preferred_element_type=jnp.float32)
        return acc

    _flash_update(s, pv, m_ref, l_ref, acc_ref)

    @pl.when(c == pl.num_programs(1) - 1)
    def _():
        o_ref[0] = acc_ref[...] / jnp.maximum(l_ref[...], 1e-30)


def paged_attn(q2, pool, layer, page_table, k_blk, v_blk, bias, k_new, v_new, bias_new):
    n, rows, lw = q2.shape
    n_pages = page_table.shape[1]
    ppc = min(PAGES_PER_CHUNK, n_pages)
    n_chunks = n_pages // ppc

    def page_spec(i, blk):
        return pl.BlockSpec((None, None, PAGE_SIZE, lw),
                            lambda a, c, pt: (layer, pt[a * n_pages + c * ppc + i], 0, blk))

    per_seq = lambda a, c, pt: (a, 0, 0)
    grid_spec = pltpu.PrefetchScalarGridSpec(
        num_scalar_prefetch=1,
        grid=(n, n_chunks),
        in_specs=[pl.BlockSpec((1, rows, lw), per_seq),
                  pl.BlockSpec((1, rows, PAGE_SIZE), per_seq),
                  pl.BlockSpec((1, PAGE_SIZE, lw), per_seq),
                  pl.BlockSpec((1, PAGE_SIZE, lw), per_seq),
                  pl.BlockSpec((1, rows, ppc * PAGE_SIZE), lambda a, c, pt: (a, 0, c))]
                 + [page_spec(i, k_blk) for i in range(ppc)]
                 + [page_spec(i, v_blk) for i in range(ppc)],
        out_specs=pl.BlockSpec((1, rows, lw), per_seq),
        scratch_shapes=[pltpu.VMEM((rows, 1), jnp.float32),
                        pltpu.VMEM((rows, 1), jnp.float32),
                        pltpu.VMEM((rows, lw), jnp.float32)],
    )
    return pl.pallas_call(
        functools.partial(_paged_attn_body, ppc=ppc),
        grid_spec=grid_spec,
        out_shape=jax.ShapeDtypeStruct((n, rows, lw), jnp.float32),
        compiler_params=pltpu.CompilerParams(
            dimension_semantics=("arbitrary", "arbitrary"), vmem_limit_bytes=VMEM_LIMIT),
        name="paged_attn",
    )(page_table.reshape(-1), q2, bias_new, k_new, v_new, bias, *([pool] * (2 * ppc)))


def _group_lane_queries(q):
    n, t, g, r, dh = q.shape
    qs = jnp.transpose(q * (dh ** -0.5), (0, 2, 1, 3, 4))
    onehot = jnp.eye(g, dtype=qs.dtype)
    q2 = qs[:, :, :, :, None, :] * onehot[None, :, None, None, :, None]
    return q2.reshape(n, g * t * r, g * dh).astype(MXU_DTYPE)


def _ungroup_lanes(o, t, g, r, dh):
    n = o.shape[0]
    o6 = o.reshape(n, g, t, r, g, dh)
    return jnp.stack([o6[:, gi, :, :, gi] for gi in range(g)], axis=2)


def _rows_bias(allow, g, r):
    n, t, ga, k = allow.shape
    a = jnp.broadcast_to(jnp.transpose(allow, (0, 2, 1, 3))[:, :, :, None, :], (n, ga, t, r, k))
    if ga != g:
        a = jnp.broadcast_to(a, (n, g, t, r, k))
    return jnp.where(a, 0.0, NEG_INF).astype(BIAS_DTYPE).reshape(n, g * t * r, k)


def _new_rows_page(x, lw):
    n, t, _ = x.shape
    return jnp.pad(x, ((0, 0), (0, PAGE_SIZE - t), (0, 0))).astype(MXU_DTYPE)


def nsa_sample_p(z, pool, layer, win_buf, page_table, w_ck, w_cv, past):
    n, t, _ = z.shape
    g, r = NSA_KV_HEADS, N_HEADS // NSA_KV_HEADS
    pos = past + jnp.arange(t)
    q, kv, gates = _nsa_project(z, pos)
    ck_p, cv_p = nsa_compress_paged(pool, layer, page_table, w_ck, w_cv)
    ck_n, cv_n = _nsa_compress(kv[:, :, 0:2], w_ck, w_cv)
    o_cmp, p_cmp = _nsa_cmp_attend(q, jnp.concatenate([ck_p, ck_n], 1),
                                   jnp.concatenate([cv_p, cv_n], 1), pos)
    n_slc = -(-(past + t) // NSA_SEL_BLOCK)
    blk = _nsa_select_mask(p_cmp, pos, n_slc)
    n_past_blk = past // NSA_SEL_BLOCK
    allow_past = jnp.repeat(blk[..., :n_past_blk], NSA_SEL_BLOCK, axis=-1)
    j = jnp.arange(PAGE_SIZE)
    new_blk = (past + j) // NSA_SEL_BLOCK
    allow_new = (jnp.take(blk, jnp.minimum(new_blk, n_slc - 1), axis=-1)
                 & (j[None, :] <= jnp.arange(t)[:, None])[None, :, None, :] & (j < t))
    lw = g * HEAD_DIM
    pool_flat = pool.reshape(pool.shape[0], pool.shape[1], PAGE_SIZE, 4 * lw)
    o = paged_attn(_group_lane_queries(q), pool_flat, layer, page_table, 2, 3,
                   _rows_bias(allow_past, g, r),
                   _new_rows_page(kv[:, :, 2].reshape(n, t, lw), lw),
                   _new_rows_page(kv[:, :, 3].reshape(n, t, lw), lw),
                   _rows_bias(allow_new, g, r))
    o_slc = _ungroup_lanes(o, t, g, r, HEAD_DIM)
    keys = jnp.concatenate([win_buf, kv[:, :, 4:6]], axis=1)
    kp = past - win_buf.shape[1] + jnp.arange(keys.shape[1])
    diff = pos[:, None] - kp[None, :]
    o_win, _ = _attend_dense(q, keys[:, :, 0], keys[:, :, 1], ((diff >= 0) & (diff < NSA_WINDOW))[None])
    return _nsa_gate(gates, o_cmp, o_slc, o_win), kv[:, :, 0:4], _tail(keys, NSA_WINDOW)


def _dsa_sample_select_body(pt_ref, qx_ref, w_ref, kn_ref, *rest, n_pages, t_new, top):
    pages = rest[:n_pages]
    o_ref, key_ref = rest[n_pages:]
    qx = qx_ref[0]
    w = w_ref[0]
    row = lax.broadcasted_iota(jnp.int32, (8, 1), 0)
    col = lax.broadcasted_iota(jnp.int32, (1, PAGE_SIZE), 1)
    new_ok = (col <= row) & (col < t_new)

    for i in range(n_pages + 1):
        kx = kn_ref[0] if i == n_pages else pages[i][...].astype(MXU_DTYPE)
        d = lax.dot_general(qx, kx, _AT_BT, preferred_element_type=jnp.float32)
        d = jnp.maximum(d, 0.0).reshape(IDX_HEADS, 8, PAGE_SIZE)
        sc = w[:, 0:1] * d[0]
        for hh in range(1, IDX_HEADS):
            sc = sc + w[:, hh:hh + 1] * d[hh]
        if i == n_pages:
            sc = jnp.where(new_ok, sc, NEG_INF)
        bits = pltpu.bitcast(sc, jnp.int32)
        key = jnp.where(bits >= 0, bits, bits ^ 0x7FFFFFFF)
        key_ref[i] = jnp.where(key == -1, 0, key)

    def count(pred):
        hit = jnp.where(pred(key_ref[...]), 1.0, 0.0)
        return jnp.sum(jnp.sum(hit, axis=0), axis=1, keepdims=True)

    c0 = count(lambda kk: kk >= 0)
    tau = jnp.where(c0 >= top, 0, INT32_MIN).astype(jnp.int32)

    def bit_step(i, tau):
        cand = tau | jnp.left_shift(jnp.int32(1), 30 - i)
        return jnp.where(count(lambda kk: kk >= cand) >= top, cand, tau)

    tau = lax.fori_loop(0, 31, bit_step, tau)
    need = top - count(lambda kk: kk > tau)
    ri = lax.broadcasted_iota(jnp.int32, (PAGE_SIZE, PAGE_SIZE), 0)
    ci = lax.broadcasted_iota(jnp.int32, (PAGE_SIZE, PAGE_SIZE), 1)
    tri = jnp.where(ri <= ci, 1.0, 0.0).astype(MXU_DTYPE)
    seen = jnp.zeros((8, 1), jnp.float32)
    for i in range(n_pages + 1):
        kk = key_ref[i]
        tie = kk == tau
        tie_f = jnp.where(tie, 1.0, 0.0)
        rank = seen + jnp.dot(tie_f.astype(MXU_DTYPE), tri, preferred_element_type=jnp.float32)
        sel = (kk > tau) | (tie & (rank <= need))
        if i == n_pages:
            sel = sel & new_ok
        o_ref[0, :, i * PAGE_SIZE:(i + 1) * PAGE_SIZE] = jnp.where(sel, 0.0, NEG_INF)
        seen = seen + jnp.sum(tie_f, axis=1, keepdims=True)


def dsa_sample_select(q_idx, k_idx_new, w_idx, pool_idx, layer, page_table, top):
    n, t, hh, di = q_idx.shape
    n_pages = page_table.shape[1]
    qx = jnp.pad(jnp.transpose(q_idx, (0, 2, 1, 3)), ((0, 0), (0, 0), (0, 8 - t), (0, 0)))
    qx = qx.reshape(n, hh * 8, di).astype(MXU_DTYPE)
    w8 = jnp.pad(w_idx, ((0, 0), (0, 8 - t), (0, 0)))
    kn = _new_rows_page(k_idx_new, di)
    per_seq = lambda a, pt: (a, 0, 0)
    total = (n_pages + 1) * PAGE_SIZE
    grid_spec = pltpu.PrefetchScalarGridSpec(
        num_scalar_prefetch=1,
        grid=(n,),
        in_specs=[pl.BlockSpec((1, hh * 8, di), per_seq),
                  pl.BlockSpec((1, 8, hh), per_seq),
                  pl.BlockSpec((1, PAGE_SIZE, di), per_seq)]
                 + [pl.BlockSpec((None, None, PAGE_SIZE, di),
                                 functools.partial(lambda a, pt, i: (layer, pt[a * n_pages + i], 0, 0), i=i))
                    for i in range(n_pages)],
        out_specs=pl.BlockSpec((1, 8, total), per_seq),
        scratch_shapes=[pltpu.VMEM((n_pages + 1, 8, PAGE_SIZE), jnp.int32)],
    )
    return pl.pallas_call(
        functools.partial(_dsa_sample_select_body, n_pages=n_pages, t_new=t, top=top),
        grid_spec=grid_spec,
        out_shape=jax.ShapeDtypeStruct((n, 8, total), jnp.float32),
        compiler_params=pltpu.CompilerParams(
            dimension_semantics=("arbitrary",), vmem_limit_bytes=VMEM_LIMIT),
        name="dsa_sample_select",
    )(page_table.reshape(-1), qx, w8, kn, *([pool_idx] * n_pages))


def dsa_sample_p(z, pool_kv, pool_idx, layer, page_table, past):
    n, t, _ = z.shape
    g, r = DSA_KV_HEADS, N_HEADS // DSA_KV_HEADS
    pos = past + jnp.arange(t)
    q, kv, q_idx, k_idx, w_idx = _dsa_project(z, pos)
    total = past + t
    bias8 = dsa_sample_select(q_idx, k_idx, w_idx, pool_idx, layer, page_table, min(DSA_TOPK, total // 4))
    allow = (bias8[:, :t] == 0.0)[:, :, None, :]
    lw = g * HEAD_DIM
    pool_flat = pool_kv.reshape(pool_kv.shape[0], pool_kv.shape[1], PAGE_SIZE, 2 * lw)
    o = paged_attn(_group_lane_queries(q), pool_flat, layer, page_table, 0, 1,
                   _rows_bias(allow[..., :past], g, r),
                   _new_rows_page(kv[:, :, 0].reshape(n, t, lw), lw),
                   _new_rows_page(kv[:, :, 1].reshape(n, t, lw), lw),
                   _rows_bias(allow[..., past:], g, r))
    return _ungroup_lanes(o, t, g, r, HEAD_DIM).reshape(n, t, Q_DIM), kv, k_idx


def dil_sample(z, bufs, past):
    n, t, _ = z.shape
    pos = past + jnp.arange(t)
    q, kv = _dil_project(z, pos)
    outs, lses, new_bufs = [], [], []
    for gi, ((win, dil), buf) in enumerate(zip(DIL_GROUPS, bufs)):
        keys = jnp.concatenate([buf, kv[:, :, gi]], axis=1)
        base = past - buf.shape[1]
        local = pos[:, None] - (jnp.arange(win // dil + 1) * dil)[None, :] - base
        lc = jnp.maximum(local, 0)
        o, lse = _attend_gathered(q, keys[:, lc, 0], keys[:, lc, 1], (local >= 0)[None, :, :, None])
        outs.append(o)
        lses.append(lse)
        new_bufs.append(_tail(keys, win))
    return _dil_mix(outs, lses).astype(z.dtype).reshape(n, t, Q_DIM), new_bufs


def swa_sample(z, buf, sink, past):
    n, t, _ = z.shape
    pos = past + jnp.arange(t)
    q, kv = _swa_project(z, pos)
    keys = jnp.concatenate([buf, kv], axis=1)
    kp = past - buf.shape[1] + jnp.arange(keys.shape[1])
    diff = pos[:, None] - kp[None, :]
    o, _ = _attend_dense(q, keys[:, :, 0], keys[:, :, 1], ((diff >= 0) & (diff < SWA_WINDOW))[None],
                         sink.reshape(SWA_KV_HEADS, -1))
    return o.reshape(n, t, Q_DIM), _tail(keys, SWA_WINDOW)


def _pad_cols(w):
    n_out = w.shape[1]
    return jnp.pad(w, ((0, 0), (0, -n_out % LANE))).astype(MXU_DTYPE)


def _project(h, g, w_in):
    n, t, _ = h.shape
    z = norm_proj(h.reshape(n * t, D_MODEL), g, _pad_cols(w_in))
    return z[:, :w_in.shape[1]].reshape(n, t, w_in.shape[1])


def _add_out_proj(h, o, w_out):
    n, t, _ = h.shape
    return proj_residual(o.reshape(n * t, Q_DIM), w_out.astype(MXU_DTYPE), h.reshape(n * t, D_MODEL)).reshape(h.shape)


def _add_ffn(h, g, w_in_b, w_out_b):
    return ffn_residual(h.reshape(-1, D_MODEL), g, w_in_b, w_out_b).reshape(h.shape)


def kernel(x_prompt, x_sample, cache_nsa_kv, cache_nsa_win, cache_dil1, cache_dil2, cache_dil3,
           cache_dsa_kv, cache_dsa_idx, cache_swa, page_table,
           norm_mix, norm_ffn, norm_final, ffn_in, ffn_out,
           nsa_w_in, nsa_w_cmp_k, nsa_w_cmp_v, nsa_w_out,
           dil_w_in, dil_w_out, dsa_w_in, dsa_w_out,
           swa_w_in, swa_sink, swa_w_out):
    past = page_table.shape[1] * PAGE_SIZE
    hp, hs = x_prompt, x_sample
    ffn_in_b = ffn_in.astype(MXU_DTYPE)
    ffn_out_b = ffn_out.astype(MXU_DTYPE)
    st = {name: [] for name in ("nsa_kv_p", "nsa_kv_s", "nsa_win_p", "nsa_win_s",
                                "dil1_p", "dil1_s", "dil2_p", "dil2_s", "dil3_p", "dil3_s",
                                "dsa_kv_p", "dsa_kv_s", "dsa_idx_p", "dsa_idx_s", "swa_p", "swa_s")}
    for i in range(DEPTH):
        kind, j = i % N_MIXERS, i // N_MIXERS
        w_in, w_out = ((nsa_w_in, nsa_w_out), (dil_w_in, dil_w_out), (dsa_w_in, dsa_w_out), (swa_w_in, swa_w_out))[kind]
        zp, zs = _project(hp, norm_mix[i], w_in[j]), _project(hs, norm_mix[i], w_in[j])
        if kind == 0:
            op, kv_p, win_p = nsa_prompt_p(zp, nsa_w_cmp_k[j], nsa_w_cmp_v[j])
            os_, kv_s, win_s = nsa_sample_p(zs, cache_nsa_kv, j, cache_nsa_win[j], page_table,
                                            nsa_w_cmp_k[j], nsa_w_cmp_v[j], past)
            st["nsa_kv_p"].append(kv_p)
            st["nsa_kv_s"].append(kv_s)
            st["nsa_win_p"].append(win_p)
            st["nsa_win_s"].append(win_s)
        elif kind == 1:
            op, bufs_p = dil_prompt_p(zp)
            os_, bufs_s = dil_sample(zs, [cache_dil1[j], cache_dil2[j], cache_dil3[j]], past)
            for gi in range(len(DIL_GROUPS)):
                st["dil%d_p" % (gi + 1)].append(bufs_p[gi])
                st["dil%d_s" % (gi + 1)].append(bufs_s[gi])
        elif kind == 2:
            op, kv_p, idx_p = dsa_prompt_p(zp)
            os_, kv_s, idx_s = dsa_sample_p(zs, cache_dsa_kv, cache_dsa_idx, j, page_table, past)
            st["dsa_kv_p"].append(kv_p)
            st["dsa_kv_s"].append(kv_s)
            st["dsa_idx_p"].append(idx_p)
            st["dsa_idx_s"].append(idx_s)
        else:
            op, buf_p = swa_prompt_p(zp, swa_sink[j])
            os_, buf_s = swa_sample(zs, cache_swa[j], swa_sink[j], past)
            st["swa_p"].append(buf_p)
            st["swa_s"].append(buf_s)
        hp = _add_out_proj(hp, op, w_out[j])
        hs = _add_out_proj(hs, os_, w_out[j])
        hp = _add_ffn(hp, norm_ffn[i], ffn_in_b[i], ffn_out_b[i])
        hs = _add_ffn(hs, norm_ffn[i], ffn_in_b[i], ffn_out_b[i])
    y_prompt = rms_norm(hp, norm_final)
    y_sample = rms_norm(hs, norm_final)
    return (y_prompt, y_sample,
            jnp.stack(st["nsa_kv_p"]), jnp.stack(st["nsa_kv_s"]),
            jnp.stack(st["nsa_win_p"]), jnp.stack(st["nsa_win_s"]),
            jnp.stack(st["dil1_p"]), jnp.stack(st["dil1_s"]),
            jnp.stack(st["dil2_p"]), jnp.stack(st["dil2_s"]),
            jnp.stack(st["dil3_p"]), jnp.stack(st["dil3_s"]),
            jnp.stack(st["dsa_kv_p"]), jnp.stack(st["dsa_kv_s"]),
            jnp.stack(st["dsa_idx_p"]), jnp.stack(st["dsa_idx_s"]),
            jnp.stack(st["swa_p"]), jnp.stack(st["swa_s"]))
```

```python
import functools

import jax, jax.numpy as jnp
from jax import lax
import numpy as np
from jax.experimental import pallas as pl
from jax.experimental.pallas import tpu as pltpu

D_MODEL = 1024
BATCH = 2
SEQ = 8192
DEPTH = 4
DEC_BATCH = 128
DEC_SEQ = 4
PAST_LEN = 8192
PAGE_SIZE = 128

HEAD_DIM = 64
N_HEADS = D_MODEL // HEAD_DIM
Q_DIM = N_HEADS * HEAD_DIM
ROPE_THETA = 10000.0
NORM_EPS = 1e-6
N_MIXERS = 4
D_FF = -(-(8 * D_MODEL) // (3 * 256)) * 256
NEG_INF = -1e30

NSA_KV_HEADS = 2
NSA_CMP_BLOCK = 32
NSA_SEL_BLOCK = 64
NSA_N_SEL = 16
NSA_WINDOW = 512
NSA_FORCE_BONUS = 1e4
NSA_IN = Q_DIM + 6 * NSA_KV_HEADS * HEAD_DIM + 3 * N_HEADS

DIL_KV_HEADS = 4
DIL_GROUPS = ((128, 1), (512, 4), (2048, 16))
DIL_IN = Q_DIM + len(DIL_GROUPS) * 2 * DIL_KV_HEADS * HEAD_DIM

DSA_KV_HEADS = 4
IDX_HEADS = 8
IDX_DIM = 64
DSA_TOPK = 256
IDX_SCALE = (IDX_DIM * IDX_HEADS) ** -0.5
DSA_IN = Q_DIM + 2 * DSA_KV_HEADS * HEAD_DIM + IDX_HEADS * IDX_DIM + IDX_DIM + IDX_HEADS

SWA_KV_HEADS = 2
SWA_WINDOW = 128
SWA_IN = Q_DIM + 2 * SWA_KV_HEADS * HEAD_DIM


def rms_norm(x, g):
    xf = x.astype(jnp.float32)
    y = xf * lax.rsqrt(jnp.mean(xf * xf, axis=-1, keepdims=True) + NORM_EPS)
    return (y * g.astype(jnp.float32)).astype(x.dtype)


def rope(x, pos):
    half = x.shape[-1] // 2
    inv = ROPE_THETA ** (-jnp.arange(half, dtype=jnp.float32) / half)
    ang = pos.astype(jnp.float32)[:, None] * inv[None, :]
    shape = (pos.shape[0],) + (1,) * (x.ndim - 3) + (half,)
    cos, sin = jnp.cos(ang).reshape(shape), jnp.sin(ang).reshape(shape)
    xf = x.astype(jnp.float32)
    x1, x2 = xf[..., :half], xf[..., half:]
    return jnp.concatenate([x1 * cos - x2 * sin, x2 * cos + x1 * sin], axis=-1).astype(x.dtype)


def _tail(rows, window):
    n = rows.shape[1]
    return rows[:, n - min(window, n):]


def _masked_softmax(s, mask, sink=None):
    s = jnp.where(mask, s, NEG_INF)
    m = jnp.max(s, axis=-1, keepdims=True)
    if sink is not None:
        m = jnp.maximum(m, sink)
    e = jnp.where(mask, jnp.exp(s - m), 0.0)
    den = jnp.sum(e, axis=-1, keepdims=True)
    if sink is not None:
        den = den + jnp.exp(sink - m)
    den = jnp.maximum(den, 1e-30)
    return e / den, (m + jnp.log(den))[..., 0]


def _attend_dense(q, k, v, mask, sink=None):
    s = jnp.einsum('nqgrd,nkgd->ngrqk', q, k, preferred_element_type=jnp.float32) * (q.shape[-1] ** -0.5)
    sk = None if sink is None else sink.astype(jnp.float32)[None, :, :, None, None]
    p, lse = _masked_softmax(s, mask[:, None, None], sk)
    o = jnp.einsum('ngrqk,nkgd->nqgrd', p.astype(v.dtype), v)
    return o, jnp.transpose(lse, (0, 3, 1, 2))


def _attend_gathered(q, k, v, mask):
    s = jnp.einsum('nqgrd,nqkgd->nqgrk', q, k, preferred_element_type=jnp.float32) * (q.shape[-1] ** -0.5)
    p, lse = _masked_softmax(s, jnp.swapaxes(mask, 2, 3)[:, :, :, None, :])
    o = jnp.einsum('nqgrk,nqkgd->nqgrd', p.astype(v.dtype), v)
    return o, lse


def _nsa_project(z, pos):
    n, t, _ = z.shape
    g, r = NSA_KV_HEADS, N_HEADS // NSA_KV_HEADS
    kv_end = Q_DIM + 6 * g * HEAD_DIM
    q = rope(z[..., :Q_DIM].reshape(n, t, g, r, HEAD_DIM), pos)
    kv = z[..., Q_DIM:kv_end].reshape(n, t, 3, 2, g, HEAD_DIM)
    kv = jnp.stack([rope(kv[:, :, :, 0], pos), kv[:, :, :, 1]], axis=3).reshape(n, t, 6, g, HEAD_DIM)
    gates = jax.nn.sigmoid(z[..., kv_end:].astype(jnp.float32)).reshape(n, t, g, r, 3)
    return q, kv, gates


def _nsa_compress(rows, w_ck, w_cv):
    n, length, _, g, dh = rows.shape
    nb = length // NSA_CMP_BLOCK
    r = rows[:, :nb * NSA_CMP_BLOCK].reshape(n, nb, NSA_CMP_BLOCK, 2, g, dh)
    ck = jnp.einsum('nbcgd,gcde->nbge', r[:, :, :, 0], w_ck)
    cv = jnp.einsum('nbcgd,gcde->nbge', r[:, :, :, 1], w_cv)
    return ck, cv


def _nsa_cmp_attend(q, ck, cv, qpos):
    blk_end = (jnp.arange(ck.shape[1]) + 1) * NSA_CMP_BLOCK - 1
    mask = (blk_end[None, :] <= qpos[:, None])[None, :, None, None, :]
    s = jnp.einsum('nqgrd,ncgd->nqgrc', q, ck, preferred_element_type=jnp.float32) * (q.shape[-1] ** -0.5)
    p, _ = _masked_softmax(s, mask)
    o = jnp.einsum('nqgrc,ncgd->nqgrd', p.astype(cv.dtype), cv)
    return o, p


def _nsa_gate(gates, o_cmp, o_slc, o_win):
    f = jnp.float32
    o = (gates[..., 0:1] * o_cmp.astype(f) + gates[..., 1:2] * o_slc.astype(f)
         + gates[..., 2:3] * o_win.astype(f))
    n, t = o.shape[:2]
    return o.reshape(n, t, Q_DIM)


def _dil_project(z, pos):
    n, t, _ = z.shape
    g = DIL_KV_HEADS
    q = rope(z[..., :Q_DIM].reshape(n, t, g, N_HEADS // g, HEAD_DIM), pos)
    kv = z[..., Q_DIM:].reshape(n, t, len(DIL_GROUPS), 2, g, HEAD_DIM)
    return q, jnp.stack([rope(kv[:, :, :, 0], pos), kv[:, :, :, 1]], axis=3)


def _dil_mix(outs, lses):
    w = jax.nn.softmax(jnp.stack(lses), axis=0)
    return jnp.sum(w[..., None] * jnp.stack(outs).astype(jnp.float32), axis=0)


def _dsa_project(z, pos):
    n, t, _ = z.shape
    g = DSA_KV_HEADS
    o1 = Q_DIM
    o2 = o1 + 2 * g * HEAD_DIM
    o3 = o2 + IDX_HEADS * IDX_DIM
    o4 = o3 + IDX_DIM
    q = rope(z[..., :o1].reshape(n, t, g, N_HEADS // g, HEAD_DIM), pos)
    kv = z[..., o1:o2].reshape(n, t, 2, g, HEAD_DIM)
    kv = jnp.stack([rope(kv[:, :, 0], pos), kv[:, :, 1]], axis=2)
    q_idx = rope(z[..., o2:o3].reshape(n, t, IDX_HEADS, IDX_DIM), pos)
    k_idx = rope(z[..., o3:o4], pos)
    w_idx = z[..., o4:].astype(jnp.float32) * IDX_SCALE
    return q, kv, q_idx, k_idx, w_idx


def _swa_project(z, pos):
    n, t, _ = z.shape
    g = SWA_KV_HEADS
    q = rope(z[..., :Q_DIM].reshape(n, t, g, N_HEADS // g, HEAD_DIM), pos)
    kv = z[..., Q_DIM:].reshape(n, t, 2, g, HEAD_DIM)
    return q, jnp.stack([rope(kv[:, :, 0], pos), kv[:, :, 1]], axis=2)


MXU_DTYPE = jnp.bfloat16
BIAS_DTYPE = jnp.bfloat16
ROW_TILE = 512
FF_CHUNK = 256
VMEM_LIMIT = 56 * 1024 * 1024
LANE = 128
PAGES_PER_CHUNK = 32
Q_TILE = 128
K_TILE = 512
FLASH_ROWS = 1024
TOPK_ROWS = 256
REMOVED = -3e38
INT32_MIN = -2 ** 31
_AT_BT = (((1,), (1,)), ((), ()))


def _ffn_body(x_ref, g_ref, win_ref, wout_ref, o_ref):
    x = x_ref[...]
    xn = x * lax.rsqrt(jnp.mean(x * x, axis=-1, keepdims=True) + NORM_EPS) * g_ref[...]
    xb = xn.astype(MXU_DTYPE)
    acc = x
    for j in range(D_FF // FF_CHUNK):
        lo = j * FF_CHUNK
        gate = jnp.dot(xb, win_ref[:, lo:lo + FF_CHUNK], preferred_element_type=jnp.float32)
        up = jnp.dot(xb, win_ref[:, D_FF + lo:D_FF + lo + FF_CHUNK], preferred_element_type=jnp.float32)
        a = (gate * jax.nn.sigmoid(gate) * up).astype(MXU_DTYPE)
        acc = acc + jnp.dot(a, wout_ref[lo:lo + FF_CHUNK, :], preferred_element_type=jnp.float32)
    o_ref[...] = acc


def ffn_residual(x, g, w_in_bf16, w_out_bf16):
    rows = x.shape[0]
    return pl.pallas_call(
        _ffn_body,
        grid=(rows // ROW_TILE,),
        in_specs=[
            pl.BlockSpec((ROW_TILE, D_MODEL), lambda i: (i, 0)),
            pl.BlockSpec((1, D_MODEL), lambda i: (0, 0)),
            pl.BlockSpec((D_MODEL, 2 * D_FF), lambda i: (0, 0)),
            pl.BlockSpec((D_FF, D_MODEL), lambda i: (0, 0)),
        ],
        out_specs=pl.BlockSpec((ROW_TILE, D_MODEL), lambda i: (i, 0)),
        out_shape=jax.ShapeDtypeStruct((rows, D_MODEL), jnp.float32),
        compiler_params=pltpu.CompilerParams(
            dimension_semantics=("arbitrary",), vmem_limit_bytes=VMEM_LIMIT),
        name="ffn_residual",
    )(x, g.reshape(1, D_MODEL), w_in_bf16, w_out_bf16)


def _norm_proj_body(x_ref, g_ref, w_ref, o_ref):
    x = x_ref[...]
    xn = x * lax.rsqrt(jnp.mean(x * x, axis=-1, keepdims=True) + NORM_EPS) * g_ref[...]
    o_ref[...] = jnp.dot(xn.astype(MXU_DTYPE), w_ref[...], preferred_element_type=jnp.float32)


def norm_proj(x, g, w_b):
    rows, n_out = x.shape[0], w_b.shape[1]
    return pl.pallas_call(
        _norm_proj_body,
        grid=(rows // ROW_TILE,),
        in_specs=[pl.BlockSpec((ROW_TILE, D_MODEL), lambda i: (i, 0)),
                  pl.BlockSpec((1, D_MODEL), lambda i: (0, 0)),
                  pl.BlockSpec((D_MODEL, n_out), lambda i: (0, 0))],
        out_specs=pl.BlockSpec((ROW_TILE, n_out), lambda i: (i, 0)),
        out_shape=jax.ShapeDtypeStruct((rows, n_out), jnp.float32),
        compiler_params=pltpu.CompilerParams(dimension_semantics=("arbitrary",), vmem_limit_bytes=VMEM_LIMIT),
        name="norm_proj",
    )(x, g.reshape(1, D_MODEL), w_b)


def _proj_residual_body(a_ref, w_ref, r_ref, o_ref):
    o_ref[...] = r_ref[...] + jnp.dot(a_ref[...].astype(MXU_DTYPE), w_ref[...], preferred_element_type=jnp.float32)


def proj_residual(a, w_b, res):
    rows = a.shape[0]
    return pl.pallas_call(
        _proj_residual_body,
        grid=(rows // ROW_TILE,),
        in_specs=[pl.BlockSpec((ROW_TILE, Q_DIM), lambda i: (i, 0)),
                  pl.BlockSpec((Q_DIM, D_MODEL), lambda i: (0, 0)),
                  pl.BlockSpec((ROW_TILE, D_MODEL), lambda i: (i, 0))],
        out_specs=pl.BlockSpec((ROW_TILE, D_MODEL), lambda i: (i, 0)),
        out_shape=jax.ShapeDtypeStruct((rows, D_MODEL), jnp.float32),
        compiler_params=pltpu.CompilerParams(dimension_semantics=("arbitrary",), vmem_limit_bytes=VMEM_LIMIT),
        name="proj_residual",
    )(a, w_b, res)


def _causal_tiles(qi, tq, tk):
    return (qi * tq + tq + tk - 1) // tk


def _mflash_body(qt_ref, k_ref, vt_ref, bt_ref, o_ref, m_ref, l_ref, acc_ref, *, r, tq, tk):
    qi = pl.program_id(2)
    qt = qt_ref[0, 0, 0]
    m_ref[...] = jnp.full(m_ref.shape, NEG_INF, jnp.float32)
    l_ref[...] = jnp.zeros(l_ref.shape, jnp.float32)
    acc_ref[...] = jnp.zeros(acc_ref.shape, jnp.float32)

    def step(kt, carry):
        ks = pl.multiple_of(kt * tk, tk)
        k = k_ref[0, 0, pl.ds(ks, tk), :]
        vt = vt_ref[0, 0, :, pl.ds(ks, tk)]
        bias = bt_ref[0, 0, kt, 0].astype(jnp.float32)
        s = jnp.dot(k, qt, preferred_element_type=jnp.float32) + jnp.concatenate([bias] * r, axis=1)
        m_old = m_ref[...]
        m_new = jnp.maximum(m_old, jnp.max(s, axis=0, keepdims=True))
        p = jnp.exp(s - m_new)
        alpha = jnp.exp(m_old - m_new)
        l_ref[...] = alpha * l_ref[...] + jnp.sum(p, axis=0, keepdims=True)
        acc_ref[...] = alpha * acc_ref[...] + jnp.dot(vt, p.astype(MXU_DTYPE), preferred_element_type=jnp.float32)
        m_ref[...] = m_new
        return carry

    lax.fori_loop(0, _causal_tiles(qi, tq, tk), step, 0)
    o_ref[0, 0, 0] = acc_ref[...] / jnp.maximum(l_ref[...], 1e-30)


def masked_flash(q, k, v, allow_t):
    n, t, g, r, dh = q.shape
    gb = allow_t.shape[1]
    tq, tk = FLASH_ROWS // r, K_TILE
    nq, nkt = t // tq, t // tk
    qs = (q * (dh ** -0.5)).reshape(n, nq, tq, g, r, dh)
    qt = jnp.transpose(qs, (0, 3, 1, 5, 4, 2)).reshape(n, g, nq, dh, r * tq).astype(MXU_DTYPE)
    kb = jnp.transpose(k, (0, 2, 1, 3)).astype(MXU_DTYPE)
    vt = jnp.transpose(v, (0, 2, 3, 1)).astype(MXU_DTYPE)
    bt = jnp.where(allow_t, 0.0, NEG_INF).astype(BIAS_DTYPE).reshape(n, gb, nkt, tk, nq, tq)
    bt = jnp.transpose(bt, (0, 1, 2, 4, 3, 5))
    bmap = (lambda a, b, c: (a, b, 0, c, 0, 0)) if gb == g else (lambda a, b, c: (a, 0, 0, c, 0, 0))
    o = pl.pallas_call(
        functools.partial(_mflash_body, r=r, tq=tq, tk=tk),
        grid=(n, g, nq),
        in_specs=[
            pl.BlockSpec((1, 1, 1, dh, r * tq), lambda a, b, c: (a, b, c, 0, 0)),
            pl.BlockSpec((1, 1, t, dh), lambda a, b, c: (a, b, 0, 0)),
            pl.BlockSpec((1, 1, dh, t), lambda a, b, c: (a, b, 0, 0)),
            pl.BlockSpec((1, 1, nkt, 1, tk, tq), bmap),
        ],
        out_specs=pl.BlockSpec((1, 1, 1, dh, r * tq), lambda a, b, c: (a, b, c, 0, 0)),
        out_shape=jax.ShapeDtypeStruct((n, g, nq, dh, r * tq), jnp.float32),
        scratch_shapes=[pltpu.VMEM((1, r * tq), jnp.float32),
                        pltpu.VMEM((1, r * tq), jnp.float32),
                        pltpu.VMEM((dh, r * tq), jnp.float32)],
        compiler_params=pltpu.CompilerParams(
            dimension_semantics=("arbitrary", "arbitrary", "arbitrary"), vmem_limit_bytes=VMEM_LIMIT),
        name="masked_flash",
    )(qt, kb, vt, bt)
    o = o.reshape(n, g, nq, dh, r, tq)
    return jnp.transpose(o, (0, 2, 5, 1, 4, 3)).reshape(n, t, g, r, dh)


def _dsa_select_body(qx_ref, kx_ref, w_ref, o_ref, key_ref, *, tq, tk, top):
    qi = pl.program_id(1)
    nkt = key_ref.shape[0]
    n_valid = _causal_tiles(qi, tq, tk)
    qx = qx_ref[0].reshape(IDX_HEADS * tq, IDX_DIM)
    w = w_ref[0]
    row = qi * tq + lax.broadcasted_iota(jnp.int32, (tq, 1), 0)

    def causal(kt):
        col = kt * tk + lax.broadcasted_iota(jnp.int32, (1, tk), 1)
        return col <= row

    def score_step(kt, carry):
        ks = pl.multiple_of(kt * tk, tk)
        d = lax.dot_general(qx, kx_ref[0, pl.ds(ks, tk), :], _AT_BT, preferred_element_type=jnp.float32)
        d = jnp.maximum(d, 0.0).reshape(IDX_HEADS, tq, tk)
        sc = w[:, 0:1] * d[0]
        for h in range(1, IDX_HEADS):
            sc = sc + w[:, h:h + 1] * d[h]
        sc = jnp.where(causal(kt), sc, NEG_INF)
        bits = pltpu.bitcast(sc, jnp.int32)
        key = jnp.where(bits >= 0, bits, bits ^ 0x7FFFFFFF)
        key_ref[kt] = jnp.where(key == -1, 0, key)
        return carry

    lax.fori_loop(0, n_valid, score_step, 0)

    def count(bound, strict):
        bb = jnp.broadcast_to(bound, (tq, LANE))

        def body(kt, c):
            for b in range(tk // LANE):
                kk = key_ref[kt, :, b * LANE:(b + 1) * LANE]
                c = c + jnp.where((kk > bb) if strict else (kk >= bb), 1.0, 0.0)
            return c
        c = lax.fori_loop(0, n_valid, body, jnp.zeros((tq, LANE), jnp.float32))
        return jnp.sum(c, axis=1, keepdims=True)

    c0 = count(jnp.zeros((tq, 1), jnp.int32), False)
    tau = jnp.where(c0 >= top, 0, INT32_MIN).astype(jnp.int32)

    def bit_step(i, tau):
        cand = tau | jnp.left_shift(jnp.int32(1), 30 - i)
        return jnp.where(count(cand, False) >= top, cand, tau)

    tau = lax.fori_loop(0, 31, bit_step, tau)
    need = top - count(tau, True)
    ri = lax.broadcasted_iota(jnp.int32, (tk, tk), 0)
    ci = lax.broadcasted_iota(jnp.int32, (tk, tk), 1)
    tri = jnp.where(ri <= ci, 1.0, 0.0).astype(MXU_DTYPE)

    def out_step(kt, seen):
        kk = key_ref[kt]
        tie = kk == tau
        tie_f = jnp.where(tie, 1.0, 0.0)
        rank = seen + jnp.dot(tie_f.astype(MXU_DTYPE), tri, preferred_element_type=jnp.float32)
        sel = (kk > tau) | (tie & (rank <= need))
        o_ref[0, kt] = jnp.where(sel & causal(kt), 0.0, NEG_INF).astype(o_ref.dtype)
        return seen + jnp.sum(tie_f, axis=1, keepdims=True)

    lax.fori_loop(0, n_valid, out_step, jnp.zeros((tq, 1), jnp.float32))

    def fill_step(kt, carry):
        o_ref[0, kt] = jnp.full((tq, tk), NEG_INF, o_ref.dtype)
        return carry

    lax.fori_loop(n_valid, nkt, fill_step, 0)


def dsa_select_bias(q_idx, k_idx, w_idx, top):
    n, h, t, di = q_idx.shape
    nkt = t // K_TILE
    return pl.pallas_call(
        functools.partial(_dsa_select_body, tq=Q_TILE, tk=K_TILE, top=top),
        grid=(n, t // Q_TILE),
        in_specs=[
            pl.BlockSpec((1, h, Q_TILE, di), lambda a, c: (a, 0, c, 0)),
            pl.BlockSpec((1, t, di), lambda a, c: (a, 0, 0)),
            pl.BlockSpec((1, Q_TILE, h), lambda a, c: (a, c, 0)),
        ],
        out_specs=pl.BlockSpec((1, nkt, Q_TILE, K_TILE), lambda a, c: (a, 0, c, 0)),
        out_shape=jax.ShapeDtypeStruct((n, nkt, t, K_TILE), BIAS_DTYPE),
        scratch_shapes=[pltpu.VMEM((nkt, Q_TILE, K_TILE), jnp.int32)],
        compiler_params=pltpu.CompilerParams(
            dimension_semantics=("arbitrary", "arbitrary"), vmem_limit_bytes=VMEM_LIMIT),
        name="dsa_select",
    )(q_idx, k_idx, w_idx)


def _topk_mask_body(s_ref, o_ref, *, k):
    s0 = s_ref[...]
    lane = lax.broadcasted_iota(jnp.int32, s0.shape, 1).astype(jnp.float32)
    width = float(s0.shape[1])

    def step(_, carry):
        s, sel = carry
        m = jnp.max(s, axis=1, keepdims=True)
        first = jnp.min(jnp.where(s == m, lane, width), axis=1, keepdims=True)
        hit = lane == first
        return jnp.where(hit, REMOVED, s), jnp.where(hit, 1.0, sel)

    _, sel = lax.fori_loop(0, k, step, (s0, jnp.zeros(s0.shape, jnp.float32)))
    o_ref[...] = sel


def topk_mask(score, k):
    lead, n = score.shape[:-1], score.shape[-1]
    rows = int(np.prod(lead))
    width = -(-n // LANE) * LANE
    s2 = jnp.pad(score.reshape(rows, n), ((0, 0), (0, width - n)), constant_values=REMOVED)
    tm = min(TOPK_ROWS, rows)
    sel = pl.pallas_call(
        functools.partial(_topk_mask_body, k=k),
        grid=(rows // tm,),
        in_specs=[pl.BlockSpec((tm, width), lambda i: (i, 0))],
        out_specs=pl.BlockSpec((tm, width), lambda i: (i, 0)),
        out_shape=jax.ShapeDtypeStruct((rows, width), jnp.float32),
        compiler_params=pltpu.CompilerParams(dimension_semantics=("arbitrary",)),
        name="topk_mask",
    )(s2)
    return (sel[:, :n] > 0.5).reshape(lead + (n,))


def _nsa_select_mask(p_cmp, qpos, n_slc):
    n, t, g = p_cmp.shape[:3]
    ratio = NSA_SEL_BLOCK // NSA_CMP_BLOCK
    imp = p_cmp.sum(axis=3)
    imp = jnp.pad(imp, ((0, 0), (0, 0), (0, 0), (0, n_slc * ratio - imp.shape[-1])))
    imp = imp.reshape(n, t, g, n_slc, ratio).sum(-1)
    j = jnp.arange(n_slc)[None, :]
    cur = (qpos // NSA_SEL_BLOCK)[:, None]
    forced = ((j == 0) | (j == cur) | (j == cur - 1))[None, :, None, :]
    future = (j > cur)[None, :, None, :]
    score = jnp.where(future, NEG_INF, jnp.where(forced, imp + NSA_FORCE_BONUS, imp))
    return topk_mask(score, min(NSA_N_SEL, n_slc))


def _pad_heads(q):
    n, t, g, r, dh = q.shape
    low = ((jnp.arange(N_HEADS) // r) % 2 == 0)[None, None, :, None]
    qs = (q * (dh ** -0.5)).reshape(n, t, N_HEADS, dh)
    q_pad = jnp.concatenate([jnp.where(low, qs, 0.0), jnp.where(low, 0.0, qs)], axis=-1)
    return q_pad.reshape(n, t, N_HEADS * LANE).astype(MXU_DTYPE), low


def _unpad_heads(o, low, g, r):
    n, t, _ = o.shape
    o = o.reshape(n, t, N_HEADS, LANE)
    return jnp.where(low, o[..., :HEAD_DIM], o[..., HEAD_DIM:]).reshape(n, t, g, r, HEAD_DIM)


def _nsa_cmp_body(q_ref, ck_ref, cv_ref, o_ref, blk_ref, *, tq, r, n_sel):
    qi = pl.program_id(1)
    nb = ck_ref.shape[1]
    n_slc = nb // 2
    t = qi * tq + lax.broadcasted_iota(jnp.int32, (tq, 1), 0)
    c = lax.broadcasted_iota(jnp.int32, (1, nb), 1)
    blk_id = jnp.where(c < n_slc, 2 * c, 2 * (c - n_slc) + 1)
    visible = (blk_id + 1) * NSA_CMP_BLOCK - 1 <= t
    ck, cv = ck_ref[0], cv_ref[0]
    imps = [jnp.zeros((tq, nb), jnp.float32) for _ in range(N_HEADS // r)]
    for h in range(N_HEADS):
        s = lax.dot_general(q_ref[0, :, h * LANE:(h + 1) * LANE], ck, _AT_BT, preferred_element_type=jnp.float32)
        s = jnp.where(visible, s, NEG_INF)
        m = jnp.max(s, axis=1, keepdims=True)
        e = jnp.where(visible, jnp.exp(s - m), 0.0)
        p = e / jnp.maximum(jnp.sum(e, axis=1, keepdims=True), 1e-30)
        o_ref[0, :, h * LANE:(h + 1) * LANE] = jnp.dot(p.astype(MXU_DTYPE), cv, preferred_element_type=jnp.float32)
        imps[h // r] = imps[h // r] + p
    j = lax.broadcasted_iota(jnp.int32, (1, n_slc), 1)
    lane = j.astype(jnp.float32)
    cur = t // NSA_SEL_BLOCK
    forced = (j == 0) | (j == cur) | (j == cur - 1)
    future = j > cur
    for gi, imp2 in enumerate(imps):
        imp = imp2[:, :n_slc] + imp2[:, n_slc:]
        score = jnp.where(future, NEG_INF, jnp.where(forced, imp + NSA_FORCE_BONUS, imp))

        def step(_, carry):
            s, sel = carry
            m = jnp.max(s, axis=1, keepdims=True)
            first = jnp.min(jnp.where(s == m, lane, float(n_slc)), axis=1, keepdims=True)
            hit = lane == first
            return jnp.where(hit, REMOVED, s), jnp.where(hit, 1.0, sel)

        _, sel = lax.fori_loop(0, n_sel, step, (score, jnp.zeros((tq, n_slc), jnp.float32)))
        blk_ref[0, :, gi * n_slc:(gi + 1) * n_slc] = sel


def nsa_cmp_select(q, ck, cv):
    n, t, g, r, dh = q.shape
    n_slc = -(-t // NSA_SEL_BLOCK)
    n_slc_pad = -(-n_slc // LANE) * LANE
    nb = ck.shape[1]

    def arrange(x):
        x = jnp.pad(x.reshape(n, nb, g * dh), ((0, 0), (0, 2 * n_slc_pad - nb), (0, 0)))
        return jnp.concatenate([x[:, 0::2], x[:, 1::2]], axis=1).astype(MXU_DTYPE)

    q_pad, low = _pad_heads(q)
    qw = N_HEADS * LANE
    o, blk = pl.pallas_call(
        functools.partial(_nsa_cmp_body, tq=Q_TILE, r=r, n_sel=min(NSA_N_SEL, n_slc)),
        grid=(n, t // Q_TILE),
        in_specs=[pl.BlockSpec((1, Q_TILE, qw), lambda a, c: (a, c, 0)),
                  pl.BlockSpec((1, 2 * n_slc_pad, g * dh), lambda a, c: (a, 0, 0)),
                  pl.BlockSpec((1, 2 * n_slc_pad, g * dh), lambda a, c: (a, 0, 0))],
        out_specs=[pl.BlockSpec((1, Q_TILE, qw), lambda a, c: (a, c, 0)),
                   pl.BlockSpec((1, Q_TILE, g * n_slc_pad), lambda a, c: (a, c, 0))],
        out_shape=[jax.ShapeDtypeStruct((n, t, qw), jnp.float32),
                   jax.ShapeDtypeStruct((n, t, g * n_slc_pad), jnp.float32)],
        compiler_params=pltpu.CompilerParams(
            dimension_semantics=("arbitrary", "arbitrary"), vmem_limit_bytes=VMEM_LIMIT),
        name="nsa_cmp_select",
    )(q_pad, arrange(ck), arrange(cv))
    return _unpad_heads(o, low, g, r), blk.reshape(n, t, g, n_slc_pad)[..., :n_slc] > 0.5


def _compress_body(*refs, n_pages):
    if n_pages:
        pages, (w_ref, o_ref, x_ref) = refs[1:1 + n_pages], refs[1 + n_pages:]
        for i in range(n_pages):
            for pi in range(x_ref.shape[0]):
                x_ref[pi, i * PAGE_SIZE:(i + 1) * PAGE_SIZE, :] = pages[i][pi * LANE:(pi + 1) * LANE, :].T
    else:
        in_ref, w_ref, o_ref, x_ref = refs
        for pi in range(x_ref.shape[0]):
            x_ref[pi] = in_ref[0, :, pi * LANE:(pi + 1) * LANE]
    n_planes = x_ref.shape[0]
    nb = o_ref.shape[1]
    acc = jnp.zeros(o_ref.shape[1:], jnp.float32)
    for c in range(NSA_CMP_BLOCK):
        rows_c = jnp.concatenate([x_ref[pi, pl.ds(c, nb, stride=NSA_CMP_BLOCK), :] for pi in range(n_planes)],
                                 axis=1).astype(MXU_DTYPE)
        acc = acc + jnp.dot(rows_c, w_ref[c], preferred_element_type=jnp.float32)
    o_ref[0] = acc


def _compress_weights(w_ck, w_cv):
    w2 = jnp.stack([w_ck, w_cv])
    g = w_ck.shape[0]
    eye = jnp.eye(2 * g, dtype=w2.dtype).reshape(2, g, 2, g)
    wbd = jnp.einsum('sgcde,sgtk->csgdtke', w2, eye)
    width = 2 * g * HEAD_DIM
    return wbd.reshape(NSA_CMP_BLOCK, width, width).astype(MXU_DTYPE)


def _split_summaries(out, g):
    n, nb, _ = out.shape
    half = g * HEAD_DIM
    return out[..., :half].reshape(n, nb, g, HEAD_DIM), out[..., half:].reshape(n, nb, g, HEAD_DIM)


def nsa_compress_rows(kv_rows, w_ck, w_cv):
    n, length, _ = kv_rows.shape
    g = w_ck.shape[0]
    width = 2 * g * HEAD_DIM
    nb = length // NSA_CMP_BLOCK
    out = pl.pallas_call(
        functools.partial(_compress_body, n_pages=0),
        grid=(n,),
        in_specs=[pl.BlockSpec((1, length, width), lambda a: (a, 0, 0)),
                  pl.BlockSpec((NSA_CMP_BLOCK, width, width), lambda a: (0, 0, 0))],
        out_specs=pl.BlockSpec((1, nb, width), lambda a: (a, 0, 0)),
        out_shape=jax.ShapeDtypeStruct((n, nb, width), jnp.float32),
        scratch_shapes=[pltpu.VMEM((width // LANE, length, LANE), jnp.float32)],
        compiler_params=pltpu.CompilerParams(dimension_semantics=("arbitrary",), vmem_limit_bytes=VMEM_LIMIT),
        name="nsa_compress_rows",
    )(kv_rows, _compress_weights(w_ck, w_cv))
    return _split_summaries(out, g)


def nsa_compress_paged(pool, layer, page_table, w_ck, w_cv):
    n, n_pages = page_table.shape
    g = w_ck.shape[0]
    width = 2 * g * HEAD_DIM
    nb = n_pages * PAGE_SIZE // NSA_CMP_BLOCK
    pool_flat = _feature_major_pages(pool)
    grid_spec = pltpu.PrefetchScalarGridSpec(
        num_scalar_prefetch=1,
        grid=(n,),
        in_specs=[pl.BlockSpec((None, None, width, PAGE_SIZE),
                               functools.partial(lambda a, pt, i: (layer, pt[a * n_pages + i], 0, 0), i=i))
                  for i in range(n_pages)]
                 + [pl.BlockSpec((NSA_CMP_BLOCK, width, width), lambda a, pt: (0, 0, 0))],
        out_specs=pl.BlockSpec((1, nb, width), lambda a, pt: (a, 0, 0)),
        scratch_shapes=[pltpu.VMEM((width // LANE, n_pages * PAGE_SIZE, LANE), jnp.float32)],
    )
    out = pl.pallas_call(
        functools.partial(_compress_body, n_pages=n_pages),
        grid_spec=grid_spec,
        out_shape=jax.ShapeDtypeStruct((n, nb, width), jnp.float32),
        compiler_params=pltpu.CompilerParams(dimension_semantics=("arbitrary",), vmem_limit_bytes=VMEM_LIMIT),
        name="nsa_compress_paged",
    )(page_table.reshape(-1), *([pool_flat] * n_pages), _compress_weights(w_ck, w_cv))
    return _split_summaries(out, g)


def _band_body(*refs, tq, n_prev, span, r, has_sink):
    nk = n_prev + 1
    q_ref, k_refs, v_refs = refs[0], refs[1:1 + nk], refs[1 + nk:1 + 2 * nk]
    sink_ref = refs[1 + 2 * nk] if has_sink else None
    o_ref, lse_ref = refs[-2:]
    ui = pl.program_id(2)
    i = lax.broadcasted_iota(jnp.int32, (tq, tq), 0)
    j = lax.broadcasted_iota(jnp.int32, (tq, tq), 1)
    parts = []
    for p in range(n_prev, -1, -1):
        diff = p * tq + i - j
        parts.append((diff >= 0) & (diff < span) & (ui >= p))
    allow = jnp.concatenate(parts, axis=1)
    lane = lax.broadcasted_iota(jnp.int32, (tq, LANE), 1)
    lse_tile = jnp.zeros((tq, LANE), jnp.float32)
    for h in range(N_HEADS):
        gp = (h // r) // 2
        sl = slice(gp * LANE, (gp + 1) * LANE)
        k2 = jnp.concatenate([kr[0, :, sl] for kr in k_refs], axis=0)
        v2 = jnp.concatenate([vr[0, :, sl] for vr in v_refs], axis=0)
        s = lax.dot_general(q_ref[0, :, h * LANE:(h + 1) * LANE], k2, _AT_BT, preferred_element_type=jnp.float32)
        s = jnp.where(allow, s, NEG_INF)
        m = jnp.max(s, axis=1, keepdims=True)
        if has_sink:
            m = jnp.maximum(m, sink_ref[h])
        e = jnp.exp(s - m)
        den = jnp.sum(e, axis=1, keepdims=True)
        if has_sink:
            den = den + jnp.exp(sink_ref[h] - m)
        den = jnp.maximum(den, 1e-30)
        o_ref[0, :, h * LANE:(h + 1) * LANE] = jnp.dot(e.astype(MXU_DTYPE), v2, preferred_element_type=jnp.float32) / den
        lse_tile = jnp.where(lane == h, m + jnp.log(den), lse_tile)
    lse_ref[0] = lse_tile


def band_attn(q, k, v, dil, span, sink=None):
    n, t, g, r, dh = q.shape
    tq = Q_TILE
    n_prev = -(-(span - 1) // tq)
    q_pad, low = _pad_heads(q)
    kb = k.reshape(n, t, g * dh).astype(MXU_DTYPE)
    vb = v.reshape(n, t, g * dh).astype(MXU_DTYPE)
    qw, kw = N_HEADS * LANE, g * dh
    length = t // dil
    view = lambda x: x.reshape(n, length, dil * x.shape[-1])
    cur = lambda a, c, u: (a, u, c)
    back = [functools.partial(lambda a, c, u, p: (a, jnp.maximum(u - p, 0), c), p=p) for p in range(n_prev, -1, -1)]
    in_specs = ([pl.BlockSpec((1, tq, qw), cur)] + [pl.BlockSpec((1, tq, kw), b) for b in back] * 2)
    operands = [view(q_pad)] + [view(kb)] * (n_prev + 1) + [view(vb)] * (n_prev + 1)
    if sink is not None:
        in_specs.append(pl.BlockSpec(memory_space=pltpu.SMEM))
        operands.append(sink.astype(jnp.float32))
    o, lse = pl.pallas_call(
        functools.partial(_band_body, tq=tq, n_prev=n_prev, span=span, r=r, has_sink=sink is not None),
        grid=(n, dil, length // tq),
        in_specs=in_specs,
        out_specs=[pl.BlockSpec((1, tq, qw), cur), pl.BlockSpec((1, tq, LANE), cur)],
        out_shape=[jax.ShapeDtypeStruct((n, length, dil * qw), jnp.float32),
                   jax.ShapeDtypeStruct((n, length, dil * LANE), jnp.float32)],
        compiler_params=pltpu.CompilerParams(
            dimension_semantics=("arbitrary", "arbitrary", "arbitrary"), vmem_limit_bytes=VMEM_LIMIT),
        name="band_attn",
    )(*operands)
    return (_unpad_heads(o.reshape(n, t, qw), low, g, r),
            lse.reshape(n, t, LANE)[:, :, :N_HEADS].reshape(n, t, g, r))


def nsa_prompt_p(z, w_ck, w_cv):
    n, s, _ = z.shape
    pos = jnp.arange(s)
    q, kv, gates = _nsa_project(z, pos)
    ck, cv = nsa_compress_rows(kv.reshape(n, s, -1), w_ck, w_cv)
    o_cmp, blk = nsa_cmp_select(q, ck, cv)
    blk_t = jnp.repeat(jnp.transpose(blk, (0, 2, 3, 1)), NSA_SEL_BLOCK, axis=2)[:, :, :s]
    allow_t = blk_t & (pos[:, None] <= pos[None, :])
    o_slc = masked_flash(q, kv[:, :, 2], kv[:, :, 3], allow_t)
    o_win, _ = band_attn(q, kv[:, :, 4], kv[:, :, 5], 1, NSA_WINDOW)
    return _nsa_gate(gates, o_cmp, o_slc, o_win), kv[:, :, 0:4], _tail(kv[:, :, 4:6], NSA_WINDOW)


def dsa_prompt_p(z):
    n, s, _ = z.shape
    pos = jnp.arange(s)
    q, kv, q_idx, k_idx, w_idx = _dsa_project(z, pos)
    top = min(DSA_TOPK, s // 4)
    bias = dsa_select_bias(jnp.transpose(q_idx, (0, 2, 1, 3)).astype(MXU_DTYPE),
                           k_idx.astype(MXU_DTYPE), w_idx, top)
    allow_t = (jnp.transpose(bias, (0, 1, 3, 2)) == 0).reshape(n, 1, s, s)
    o = masked_flash(q, kv[:, :, 0], kv[:, :, 1], allow_t)
    return o.reshape(n, s, Q_DIM), kv, k_idx


def dil_prompt_p(z):
    n, s, _ = z.shape
    q, kv = _dil_project(z, jnp.arange(s))
    outs, lses, bufs = [], [], []
    for gi, (win, dil) in enumerate(DIL_GROUPS):
        o, lse = band_attn(q, kv[:, :, gi, 0], kv[:, :, gi, 1], dil, win // dil + 1)
        outs.append(o)
        lses.append(lse)
        bufs.append(_tail(kv[:, :, gi], win))
    return _dil_mix(outs, lses).astype(z.dtype).reshape(n, s, Q_DIM), bufs


def swa_prompt_p(z, sink):
    n, s, _ = z.shape
    q, kv = _swa_project(z, jnp.arange(s))
    o, _ = band_attn(q, kv[:, :, 0], kv[:, :, 1], 1, SWA_WINDOW, sink)
    return o.reshape(n, s, Q_DIM), _tail(kv, SWA_WINDOW)


def _flash_update(s, pv, m_ref, l_ref, acc_ref):
    m_old = m_ref[...]
    m_new = jnp.maximum(m_old, jnp.max(s, axis=1, keepdims=True))
    p = jnp.exp(s - m_new)
    alpha = jnp.exp(m_old - m_new)
    l_ref[...] = alpha * l_ref[...] + jnp.sum(p, axis=1, keepdims=True)
    acc_ref[...] = alpha * acc_ref[...] + pv(p.astype(MXU_DTYPE))
    m_ref[...] = m_new


def _paged_attn_body(pt_ref, q_ref, bn_ref, kn_ref, vn_ref, b_ref, *rest, ppc):
    kp, vp = rest[:ppc], rest[ppc:2 * ppc]
    o_ref, m_ref, l_ref, acc_ref = rest[2 * ppc:]
    c = pl.program_id(1)
    q = q_ref[0]

    @pl.when(c == 0)
    def _():
        m_ref[...] = jnp.full(m_ref.shape, NEG_INF, jnp.float32)
        l_ref[...] = jnp.zeros(l_ref.shape, jnp.float32)
        acc_ref[...] = jnp.zeros(acc_ref.shape, jnp.float32)
        s = lax.dot_general(q, kn_ref[0], _AT_BT, preferred_element_type=jnp.float32) + bn_ref[0].astype(jnp.float32)
        _flash_update(s, lambda p: jnp.dot(p, vn_ref[0], preferred_element_type=jnp.float32), m_ref, l_ref, acc_ref)

    parts = [jnp.dot(q, kp[i][...].astype(MXU_DTYPE), preferred_element_type=jnp.float32) for i in range(ppc)]
    s = jnp.concatenate(parts, axis=1) + b_ref[0].astype(jnp.float32)

    def pv(p):
        acc = lax.dot_general(p[:, 0:PAGE_SIZE], vp[0][...].astype(MXU_DTYPE), _AT_BT,
                              preferred_element_type=jnp.float32)
        for i in range(1, ppc):
            acc = acc + lax.dot_general(p[:, i * PAGE_SIZE:(i + 1) * PAGE_SIZE], vp[i][...].astype(MXU_DTYPE), _AT_BT,
                                        preferred_element_type=jnp.float32)
        return acc

    _flash_update(s, pv, m_ref, l_ref, acc_ref)

    @pl.when(c == pl.num_programs(1) - 1)
    def _():
        o_ref[0] = acc_ref[...] / jnp.maximum(l_ref[...], 1e-30)


def paged_attn(q2, pool, layer, page_table, k_blk, v_blk, bias, k_new, v_new, bias_new):
    n, rows, lw = q2.shape
    n_pages = page_table.shape[1]
    ppc = min(PAGES_PER_CHUNK, n_pages)
    n_chunks = n_pages // ppc

    def page_spec(i, blk):
        return pl.BlockSpec((None, None, lw, PAGE_SIZE),
                            lambda a, c, pt: (layer, pt[a * n_pages + c * ppc + i], blk, 0))

    per_seq = lambda a, c, pt: (a, 0, 0)
    grid_spec = pltpu.PrefetchScalarGridSpec(
        num_scalar_prefetch=1,
        grid=(n, n_chunks),
        in_specs=[pl.BlockSpec((1, rows, lw), per_seq),
                  pl.BlockSpec((1, rows, PAGE_SIZE), per_seq),
                  pl.BlockSpec((1, PAGE_SIZE, lw), per_seq),
                  pl.BlockSpec((1, PAGE_SIZE, lw), per_seq),
                  pl.BlockSpec((1, rows, ppc * PAGE_SIZE), lambda a, c, pt: (a, 0, c))]
                 + [page_spec(i, k_blk) for i in range(ppc)]
                 + [page_spec(i, v_blk) for i in range(ppc)],
        out_specs=pl.BlockSpec((1, rows, lw), per_seq),
        scratch_shapes=[pltpu.VMEM((rows, 1), jnp.float32),
                        pltpu.VMEM((rows, 1), jnp.float32),
                        pltpu.VMEM((rows, lw), jnp.float32)],
    )
    return pl.pallas_call(
        functools.partial(_paged_attn_body, ppc=ppc),
        grid_spec=grid_spec,
        out_shape=jax.ShapeDtypeStruct((n, rows, lw), jnp.float32),
        compiler_params=pltpu.CompilerParams(
            dimension_semantics=("arbitrary", "arbitrary"), vmem_limit_bytes=VMEM_LIMIT),
        name="paged_attn",
    )(page_table.reshape(-1), q2, bias_new, k_new, v_new, bias, *([pool] * (2 * ppc)))


def _feature_major_pages(pool):
    nd = pool.ndim
    t = jnp.transpose(pool, (0, 1) + tuple(range(3, nd)) + (2,))
    return t.reshape(pool.shape[0], pool.shape[1], -1, pool.shape[2])


def _group_lane_queries(q):
    n, t, g, r, dh = q.shape
    qs = jnp.transpose(q * (dh ** -0.5), (0, 2, 1, 3, 4))
    onehot = jnp.eye(g, dtype=qs.dtype)
    q2 = qs[:, :, :, :, None, :] * onehot[None, :, None, None, :, None]
    return q2.reshape(n, g * t * r, g * dh).astype(MXU_DTYPE)


def _ungroup_lanes(o, t, g, r, dh):
    n = o.shape[0]
    o6 = o.reshape(n, g, t, r, g, dh)
    return jnp.stack([o6[:, gi, :, :, gi] for gi in range(g)], axis=2)


def _rows_bias(allow, g, r):
    n, t, ga, k = allow.shape
    a = jnp.broadcast_to(jnp.transpose(allow, (0, 2, 1, 3))[:, :, :, None, :], (n, ga, t, r, k))
    if ga != g:
        a = jnp.broadcast_to(a, (n, g, t, r, k))
    return jnp.where(a, 0.0, NEG_INF).astype(BIAS_DTYPE).reshape(n, g * t * r, k)


def _new_rows_page(x, lw):
    n, t, _ = x.shape
    return jnp.pad(x, ((0, 0), (0, PAGE_SIZE - t), (0, 0))).astype(MXU_DTYPE)


def nsa_sample_p(z, pool, layer, win_buf, page_table, w_ck, w_cv, past):
    n, t, _ = z.shape
    g, r = NSA_KV_HEADS, N_HEADS // NSA_KV_HEADS
    pos = past + jnp.arange(t)
    q, kv, gates = _nsa_project(z, pos)
    ck_p, cv_p = nsa_compress_paged(pool, layer, page_table, w_ck, w_cv)
    ck_n, cv_n = _nsa_compress(kv[:, :, 0:2], w_ck, w_cv)
    o_cmp, p_cmp = _nsa_cmp_attend(q, jnp.concatenate([ck_p, ck_n], 1),
                                   jnp.concatenate([cv_p, cv_n], 1), pos)
    n_slc = -(-(past + t) // NSA_SEL_BLOCK)
    blk = _nsa_select_mask(p_cmp, pos, n_slc)
    n_past_blk = past // NSA_SEL_BLOCK
    allow_past = jnp.repeat(blk[..., :n_past_blk], NSA_SEL_BLOCK, axis=-1)
    j = jnp.arange(PAGE_SIZE)
    new_blk = (past + j) // NSA_SEL_BLOCK
    allow_new = (jnp.take(blk, jnp.minimum(new_blk, n_slc - 1), axis=-1)
                 & (j[None, :] <= jnp.arange(t)[:, None])[None, :, None, :] & (j < t))
    lw = g * HEAD_DIM
    pool_flat = _feature_major_pages(pool)
    o = paged_attn(_group_lane_queries(q), pool_flat, layer, page_table, 2, 3,
                   _rows_bias(allow_past, g, r),
                   _new_rows_page(kv[:, :, 2].reshape(n, t, lw), lw),
                   _new_rows_page(kv[:, :, 3].reshape(n, t, lw), lw),
                   _rows_bias(allow_new, g, r))
    o_slc = _ungroup_lanes(o, t, g, r, HEAD_DIM)
    keys = jnp.concatenate([win_buf, kv[:, :, 4:6]], axis=1)
    kp = past - win_buf.shape[1] + jnp.arange(keys.shape[1])
    diff = pos[:, None] - kp[None, :]
    o_win, _ = _attend_dense(q, keys[:, :, 0], keys[:, :, 1], ((diff >= 0) & (diff < NSA_WINDOW))[None])
    return _nsa_gate(gates, o_cmp, o_slc, o_win), kv[:, :, 0:4], _tail(keys, NSA_WINDOW)


def _dsa_sample_select_body(pt_ref, qx_ref, w_ref, kn_ref, *rest, n_pages, t_new, top):
    pages = rest[:n_pages]
    o_ref, key_ref = rest[n_pages:]
    qx = qx_ref[0]
    w = w_ref[0]
    row = lax.broadcasted_iota(jnp.int32, (8, 1), 0)
    col = lax.broadcasted_iota(jnp.int32, (1, PAGE_SIZE), 1)
    new_ok = (col <= row) & (col < t_new)

    for i in range(n_pages + 1):
        if i == n_pages:
            d = lax.dot_general(qx, kn_ref[0], _AT_BT, preferred_element_type=jnp.float32)
        else:
            d = jnp.dot(qx, pages[i][...].astype(MXU_DTYPE), preferred_element_type=jnp.float32)
        d = jnp.maximum(d, 0.0).reshape(IDX_HEADS, 8, PAGE_SIZE)
        sc = w[:, 0:1] * d[0]
        for hh in range(1, IDX_HEADS):
            sc = sc + w[:, hh:hh + 1] * d[hh]
        if i == n_pages:
            sc = jnp.where(new_ok, sc, NEG_INF)
        bits = pltpu.bitcast(sc, jnp.int32)
        key = jnp.where(bits >= 0, bits, bits ^ 0x7FFFFFFF)
        key_ref[i] = jnp.where(key == -1, 0, key)

    def count(pred):
        hit = jnp.where(pred(key_ref[...]), 1.0, 0.0)
        return jnp.sum(jnp.sum(hit, axis=0), axis=1, keepdims=True)

    c0 = count(lambda kk: kk >= 0)
    tau = jnp.where(c0 >= top, 0, INT32_MIN).astype(jnp.int32)

    def bit_step(i, tau):
        cand = tau | jnp.left_shift(jnp.int32(1), 30 - i)
        return jnp.where(count(lambda kk: kk >= cand) >= top, cand, tau)

    tau = lax.fori_loop(0, 31, bit_step, tau)
    need = top - count(lambda kk: kk > tau)
    ri = lax.broadcasted_iota(jnp.int32, (PAGE_SIZE, PAGE_SIZE), 0)
    ci = lax.broadcasted_iota(jnp.int32, (PAGE_SIZE, PAGE_SIZE), 1)
    tri = jnp.where(ri <= ci, 1.0, 0.0).astype(MXU_DTYPE)
    seen = jnp.zeros((8, 1), jnp.float32)
    for i in range(n_pages + 1):
        kk = key_ref[i]
        tie = kk == tau
        tie_f = jnp.where(tie, 1.0, 0.0)
        rank = seen + jnp.dot(tie_f.astype(MXU_DTYPE), tri, preferred_element_type=jnp.float32)
        sel = (kk > tau) | (tie & (rank <= need))
        if i == n_pages:
            sel = sel & new_ok
        o_ref[0, :, i * PAGE_SIZE:(i + 1) * PAGE_SIZE] = jnp.where(sel, 0.0, NEG_INF)
        seen = seen + jnp.sum(tie_f, axis=1, keepdims=True)


def dsa_sample_select(q_idx, k_idx_new, w_idx, pool_idx, layer, page_table, top):
    n, t, hh, di = q_idx.shape
    n_pages = page_table.shape[1]
    qx = jnp.pad(jnp.transpose(q_idx, (0, 2, 1, 3)), ((0, 0), (0, 0), (0, 8 - t), (0, 0)))
    qx = qx.reshape(n, hh * 8, di).astype(MXU_DTYPE)
    w8 = jnp.pad(w_idx, ((0, 0), (0, 8 - t), (0, 0)))
    kn = _new_rows_page(k_idx_new, di)
    per_seq = lambda a, pt: (a, 0, 0)
    total = (n_pages + 1) * PAGE_SIZE
    grid_spec = pltpu.PrefetchScalarGridSpec(
        num_scalar_prefetch=1,
        grid=(n,),
        in_specs=[pl.BlockSpec((1, hh * 8, di), per_seq),
                  pl.BlockSpec((1, 8, hh), per_seq),
                  pl.BlockSpec((1, PAGE_SIZE, di), per_seq)]
                 + [pl.BlockSpec((None, None, di, PAGE_SIZE),
                                 functools.partial(lambda a, pt, i: (layer, pt[a * n_pages + i], 0, 0), i=i))
                    for i in range(n_pages)],
        out_specs=pl.BlockSpec((1, 8, total), per_seq),
        scratch_shapes=[pltpu.VMEM((n_pages + 1, 8, PAGE_SIZE), jnp.int32)],
    )
    return pl.pallas_call(
        functools.partial(_dsa_sample_select_body, n_pages=n_pages, t_new=t, top=top),
        grid_spec=grid_spec,
        out_shape=jax.ShapeDtypeStruct((n, 8, total), jnp.float32),
        compiler_params=pltpu.CompilerParams(
            dimension_semantics=("arbitrary",), vmem_limit_bytes=VMEM_LIMIT),
        name="dsa_sample_select",
    )(page_table.reshape(-1), qx, w8, kn, *([_feature_major_pages(pool_idx)] * n_pages))


def dsa_sample_p(z, pool_kv, pool_idx, layer, page_table, past):
    n, t, _ = z.shape
    g, r = DSA_KV_HEADS, N_HEADS // DSA_KV_HEADS
    pos = past + jnp.arange(t)
    q, kv, q_idx, k_idx, w_idx = _dsa_project(z, pos)
    total = past + t
    bias8 = dsa_sample_select(q_idx, k_idx, w_idx, pool_idx, layer, page_table, min(DSA_TOPK, total // 4))
    allow = (bias8[:, :t] == 0.0)[:, :, None, :]
    lw = g * HEAD_DIM
    pool_flat = _feature_major_pages(pool_kv)
    o = paged_attn(_group_lane_queries(q), pool_flat, layer, page_table, 0, 1,
                   _rows_bias(allow[..., :past], g, r),
                   _new_rows_page(kv[:, :, 0].reshape(n, t, lw), lw),
                   _new_rows_page(kv[:, :, 1].reshape(n, t, lw), lw),
                   _rows_bias(allow[..., past:], g, r))
    return _ungroup_lanes(o, t, g, r, HEAD_DIM).reshape(n, t, Q_DIM), kv, k_idx


def dil_sample(z, bufs, past):
    n, t, _ = z.shape
    pos = past + jnp.arange(t)
    q, kv = _dil_project(z, pos)
    outs, lses, new_bufs = [], [], []
    for gi, ((win, dil), buf) in enumerate(zip(DIL_GROUPS, bufs)):
        keys = jnp.concatenate([buf, kv[:, :, gi]], axis=1)
        base = past - buf.shape[1]
        local = pos[:, None] - (jnp.arange(win // dil + 1) * dil)[None, :] - base
        lc = jnp.maximum(local, 0)
        o, lse = _attend_gathered(q, keys[:, lc, 0], keys[:, lc, 1], (local >= 0)[None, :, :, None])
        outs.append(o)
        lses.append(lse)
        new_bufs.append(_tail(keys, win))
    return _dil_mix(outs, lses).astype(z.dtype).reshape(n, t, Q_DIM), new_bufs


def swa_sample(z, buf, sink, past):
    n, t, _ = z.shape
    pos = past + jnp.arange(t)
    q, kv = _swa_project(z, pos)
    keys = jnp.concatenate([buf, kv], axis=1)
    kp = past - buf.shape[1] + jnp.arange(keys.shape[1])
    diff = pos[:, None] - kp[None, :]
    o, _ = _attend_dense(q, keys[:, :, 0], keys[:, :, 1], ((diff >= 0) & (diff < SWA_WINDOW))[None],
                         sink.reshape(SWA_KV_HEADS, -1))
    return o.reshape(n, t, Q_DIM), _tail(keys, SWA_WINDOW)


def _pad_cols(w):
    n_out = w.shape[1]
    return jnp.pad(w, ((0, 0), (0, -n_out % LANE))).astype(MXU_DTYPE)


def _project(h, g, w_in):
    n, t, _ = h.shape
    z = norm_proj(h.reshape(n * t, D_MODEL), g, _pad_cols(w_in))
    return z[:, :w_in.shape[1]].reshape(n, t, w_in.shape[1])


def _add_out_proj(h, o, w_out):
    n, t, _ = h.shape
    return proj_residual(o.reshape(n * t, Q_DIM), w_out.astype(MXU_DTYPE), h.reshape(n * t, D_MODEL)).reshape(h.shape)


def _add_ffn(h, g, w_in_b, w_out_b):
    return ffn_residual(h.reshape(-1, D_MODEL), g, w_in_b, w_out_b).reshape(h.shape)


def kernel(x_prompt, x_sample, cache_nsa_kv, cache_nsa_win, cache_dil1, cache_dil2, cache_dil3,
           cache_dsa_kv, cache_dsa_idx, cache_swa, page_table,
           norm_mix, norm_ffn, norm_final, ffn_in, ffn_out,
           nsa_w_in, nsa_w_cmp_k, nsa_w_cmp_v, nsa_w_out,
           dil_w_in, dil_w_out, dsa_w_in, dsa_w_out,
           swa_w_in, swa_sink, swa_w_out):
    past = page_table.shape[1] * PAGE_SIZE
    hp, hs = x_prompt, x_sample
    ffn_in_b = ffn_in.astype(MXU_DTYPE)
    ffn_out_b = ffn_out.astype(MXU_DTYPE)
    st = {name: [] for name in ("nsa_kv_p", "nsa_kv_s", "nsa_win_p", "nsa_win_s",
                                "dil1_p", "dil1_s", "dil2_p", "dil2_s", "dil3_p", "dil3_s",
                                "dsa_kv_p", "dsa_kv_s", "dsa_idx_p", "dsa_idx_s", "swa_p", "swa_s")}
    for i in range(DEPTH):
        kind, j = i % N_MIXERS, i // N_MIXERS
        w_in, w_out = ((nsa_w_in, nsa_w_out), (dil_w_in, dil_w_out), (dsa_w_in, dsa_w_out), (swa_w_in, swa_w_out))[kind]
        zp, zs = _project(hp, norm_mix[i], w_in[j]), _project(hs, norm_mix[i], w_in[j])
        if kind == 0:
            op, kv_p, win_p = nsa_prompt_p(zp, nsa_w_cmp_k[j], nsa_w_cmp_v[j])
            os_, kv_s, win_s = nsa_sample_p(zs, cache_nsa_kv, j, cache_nsa_win[j], page_table,
                                            nsa_w_cmp_k[j], nsa_w_cmp_v[j], past)
            st["nsa_kv_p"].append(kv_p)
            st["nsa_kv_s"].append(kv_s)
            st["nsa_win_p"].append(win_p)
            st["nsa_win_s"].append(win_s)
        elif kind == 1:
            op, bufs_p = dil_prompt_p(zp)
            os_, bufs_s = dil_sample(zs, [cache_dil1[j], cache_dil2[j], cache_dil3[j]], past)
            for gi in range(len(DIL_GROUPS)):
                st["dil%d_p" % (gi + 1)].append(bufs_p[gi])
                st["dil%d_s" % (gi + 1)].append(bufs_s[gi])
        elif kind == 2:
            op, kv_p, idx_p = dsa_prompt_p(zp)
            os_, kv_s, idx_s = dsa_sample_p(zs, cache_dsa_kv, cache_dsa_idx, j, page_table, past)
            st["dsa_kv_p"].append(kv_p)
            st["dsa_kv_s"].append(kv_s)
            st["dsa_idx_p"].append(idx_p)
            st["dsa_idx_s"].append(idx_s)
        else:
            op, buf_p = swa_prompt_p(zp, swa_sink[j])
            os_, buf_s = swa_sample(zs, cache_swa[j], swa_sink[j], past)
            st["swa_p"].append(buf_p)
            st["swa_s"].append(buf_s)
        hp = _add_out_proj(hp, op, w_out[j])
        hs = _add_out_proj(hs, os_, w_out[j])
        hp = _add_ffn(hp, norm_ffn[i], ffn_in_b[i], ffn_out_b[i])
        hs = _add_ffn(hs, norm_ffn[i], ffn_in_b[i], ffn_out_b[i])
    y_prompt = rms_norm(hp, norm_final)
    y_sample = rms_norm(hs, norm_final)
    return (y_prompt, y_sample,
            jnp.stack(st["nsa_kv_p"]), jnp.stack(st["nsa_kv_s"]),
            jnp.stack(st["nsa_win_p"]), jnp.stack(st["nsa_win_s"]),
            jnp.stack(st["dil1_p"]), jnp.stack(st["dil1_s"]),
            jnp.stack(st["dil2_p"]), jnp.stack(st["dil2_s"]),
            jnp.stack(st["dil3_p"]), jnp.stack(st["dil3_s"]),
            jnp.stack(st["dsa_kv_p"]), jnp.stack(st["dsa_kv_s"]),
            jnp.stack(st["dsa_idx_p"]), jnp.stack(st["dsa_idx_s"]),
            jnp.stack(st["swa_p"]), jnp.stack(st["swa_s"]))
```

```python
import functools

import jax, jax.numpy as jnp
from jax import lax
import numpy as np
from jax.experimental import pallas as pl
from jax.experimental.pallas import tpu as pltpu

D_MODEL = 1024
BATCH = 2
SEQ = 8192
DEPTH = 4
DEC_BATCH = 128
DEC_SEQ = 4
PAST_LEN = 8192
PAGE_SIZE = 128

HEAD_DIM = 64
N_HEADS = D_MODEL // HEAD_DIM
Q_DIM = N_HEADS * HEAD_DIM
ROPE_THETA = 10000.0
NORM_EPS = 1e-6
N_MIXERS = 4
D_FF = -(-(8 * D_MODEL) // (3 * 256)) * 256
NEG_INF = -1e30

NSA_KV_HEADS = 2
NSA_CMP_BLOCK = 32
NSA_SEL_BLOCK = 64
NSA_N_SEL = 16
NSA_WINDOW = 512
NSA_FORCE_BONUS = 1e4
NSA_IN = Q_DIM + 6 * NSA_KV_HEADS * HEAD_DIM + 3 * N_HEADS

DIL_KV_HEADS = 4
DIL_GROUPS = ((128, 1), (512, 4), (2048, 16))
DIL_IN = Q_DIM + len(DIL_GROUPS) * 2 * DIL_KV_HEADS * HEAD_DIM

DSA_KV_HEADS = 4
IDX_HEADS = 8
IDX_DIM = 64
DSA_TOPK = 256
IDX_SCALE = (IDX_DIM * IDX_HEADS) ** -0.5
DSA_IN = Q_DIM + 2 * DSA_KV_HEADS * HEAD_DIM + IDX_HEADS * IDX_DIM + IDX_DIM + IDX_HEADS

SWA_KV_HEADS = 2
SWA_WINDOW = 128
SWA_IN = Q_DIM + 2 * SWA_KV_HEADS * HEAD_DIM


def rms_norm(x, g):
    xf = x.astype(jnp.float32)
    y = xf * lax.rsqrt(jnp.mean(xf * xf, axis=-1, keepdims=True) + NORM_EPS)
    return (y * g.astype(jnp.float32)).astype(x.dtype)


def rope(x, pos):
    half = x.shape[-1] // 2
    inv = ROPE_THETA ** (-jnp.arange(half, dtype=jnp.float32) / half)
    ang = pos.astype(jnp.float32)[:, None] * inv[None, :]
    shape = (pos.shape[0],) + (1,) * (x.ndim - 3) + (half,)
    cos, sin = jnp.cos(ang).reshape(shape), jnp.sin(ang).reshape(shape)
    xf = x.astype(jnp.float32)
    x1, x2 = xf[..., :half], xf[..., half:]
    return jnp.concatenate([x1 * cos - x2 * sin, x2 * cos + x1 * sin], axis=-1).astype(x.dtype)


def _tail(rows, window):
    n = rows.shape[1]
    return rows[:, n - min(window, n):]


def _masked_softmax(s, mask, sink=None):
    s = jnp.where(mask, s, NEG_INF)
    m = jnp.max(s, axis=-1, keepdims=True)
    if sink is not None:
        m = jnp.maximum(m, sink)
    e = jnp.where(mask, jnp.exp(s - m), 0.0)
    den = jnp.sum(e, axis=-1, keepdims=True)
    if sink is not None:
        den = den + jnp.exp(sink - m)
    den = jnp.maximum(den, 1e-30)
    return e / den, (m + jnp.log(den))[..., 0]


def _attend_dense(q, k, v, mask, sink=None):
    s = jnp.einsum('nqgrd,nkgd->ngrqk', q, k, preferred_element_type=jnp.float32) * (q.shape[-1] ** -0.5)
    sk = None if sink is None else sink.astype(jnp.float32)[None, :, :, None, None]
    p, lse = _masked_softmax(s, mask[:, None, None], sk)
    o = jnp.einsum('ngrqk,nkgd->nqgrd', p.astype(v.dtype), v)
    return o, jnp.transpose(lse, (0, 3, 1, 2))


def _attend_gathered(q, k, v, mask):
    s = jnp.einsum('nqgrd,nqkgd->nqgrk', q, k, preferred_element_type=jnp.float32) * (q.shape[-1] ** -0.5)
    p, lse = _masked_softmax(s, jnp.swapaxes(mask, 2, 3)[:, :, :, None, :])
    o = jnp.einsum('nqgrk,nqkgd->nqgrd', p.astype(v.dtype), v)
    return o, lse


def _nsa_project(z, pos):
    n, t, _ = z.shape
    g, r = NSA_KV_HEADS, N_HEADS // NSA_KV_HEADS
    kv_end = Q_DIM + 6 * g * HEAD_DIM
    q = rope(z[..., :Q_DIM].reshape(n, t, g, r, HEAD_DIM), pos)
    kv = z[..., Q_DIM:kv_end].reshape(n, t, 3, 2, g, HEAD_DIM)
    kv = jnp.stack([rope(kv[:, :, :, 0], pos), kv[:, :, :, 1]], axis=3).reshape(n, t, 6, g, HEAD_DIM)
    gates = jax.nn.sigmoid(z[..., kv_end:].astype(jnp.float32)).reshape(n, t, g, r, 3)
    return q, kv, gates


def _nsa_compress(rows, w_ck, w_cv):
    n, length, _, g, dh = rows.shape
    nb = length // NSA_CMP_BLOCK
    r = rows[:, :nb * NSA_CMP_BLOCK].reshape(n, nb, NSA_CMP_BLOCK, 2, g, dh)
    ck = jnp.einsum('nbcgd,gcde->nbge', r[:, :, :, 0], w_ck)
    cv = jnp.einsum('nbcgd,gcde->nbge', r[:, :, :, 1], w_cv)
    return ck, cv


def _nsa_cmp_attend(q, ck, cv, qpos):
    blk_end = (jnp.arange(ck.shape[1]) + 1) * NSA_CMP_BLOCK - 1
    mask = (blk_end[None, :] <= qpos[:, None])[None, :, None, None, :]
    s = jnp.einsum('nqgrd,ncgd->nqgrc', q, ck, preferred_element_type=jnp.float32) * (q.shape[-1] ** -0.5)
    p, _ = _masked_softmax(s, mask)
    o = jnp.einsum('nqgrc,ncgd->nqgrd', p.astype(cv.dtype), cv)
    return o, p


def _nsa_gate(gates, o_cmp, o_slc, o_win):
    f = jnp.float32
    o = (gates[..., 0:1] * o_cmp.astype(f) + gates[..., 1:2] * o_slc.astype(f)
         + gates[..., 2:3] * o_win.astype(f))
    n, t = o.shape[:2]
    return o.reshape(n, t, Q_DIM)


def _dil_project(z, pos):
    n, t, _ = z.shape
    g = DIL_KV_HEADS
    q = rope(z[..., :Q_DIM].reshape(n, t, g, N_HEADS // g, HEAD_DIM), pos)
    kv = z[..., Q_DIM:].reshape(n, t, len(DIL_GROUPS), 2, g, HEAD_DIM)
    return q, jnp.stack([rope(kv[:, :, :, 0], pos), kv[:, :, :, 1]], axis=3)


def _dil_mix(outs, lses):
    w = jax.nn.softmax(jnp.stack(lses), axis=0)
    return jnp.sum(w[..., None] * jnp.stack(outs).astype(jnp.float32), axis=0)


def _dsa_project(z, pos):
    n, t, _ = z.shape
    g = DSA_KV_HEADS
    o1 = Q_DIM
    o2 = o1 + 2 * g * HEAD_DIM
    o3 = o2 + IDX_HEADS * IDX_DIM
    o4 = o3 + IDX_DIM
    q = rope(z[..., :o1].reshape(n, t, g, N_HEADS // g, HEAD_DIM), pos)
    kv = z[..., o1:o2].reshape(n, t, 2, g, HEAD_DIM)
    kv = jnp.stack([rope(kv[:, :, 0], pos), kv[:, :, 1]], axis=2)
    q_idx = rope(z[..., o2:o3].reshape(n, t, IDX_HEADS, IDX_DIM), pos)
    k_idx = rope(z[..., o3:o4], pos)
    w_idx = z[..., o4:].astype(jnp.float32) * IDX_SCALE
    return q, kv, q_idx, k_idx, w_idx


def _swa_project(z, pos):
    n, t, _ = z.shape
    g = SWA_KV_HEADS
    q = rope(z[..., :Q_DIM].reshape(n, t, g, N_HEADS // g, HEAD_DIM), pos)
    kv = z[..., Q_DIM:].reshape(n, t, 2, g, HEAD_DIM)
    return q, jnp.stack([rope(kv[:, :, 0], pos), kv[:, :, 1]], axis=2)


MXU_DTYPE = jnp.bfloat16
BIAS_DTYPE = jnp.bfloat16
ROW_TILE = 512
FF_CHUNK = 256
VMEM_LIMIT = 56 * 1024 * 1024
LANE = 128
PAGES_PER_CHUNK = 32
Q_TILE = 128
K_TILE = 512
FLASH_ROWS = 1024
TOPK_ROWS = 256
REMOVED = -3e38
INT32_MIN = -2 ** 31
_AT_BT = (((1,), (1,)), ((), ()))


def _ffn_body(x_ref, g_ref, win_ref, wout_ref, o_ref):
    x = x_ref[...]
    xn = x * lax.rsqrt(jnp.mean(x * x, axis=-1, keepdims=True) + NORM_EPS) * g_ref[...]
    xb = xn.astype(MXU_DTYPE)
    acc = x
    for j in range(D_FF // FF_CHUNK):
        lo = j * FF_CHUNK
        gate = jnp.dot(xb, win_ref[:, lo:lo + FF_CHUNK], preferred_element_type=jnp.float32)
        up = jnp.dot(xb, win_ref[:, D_FF + lo:D_FF + lo + FF_CHUNK], preferred_element_type=jnp.float32)
        a = (gate * jax.nn.sigmoid(gate) * up).astype(MXU_DTYPE)
        acc = acc + jnp.dot(a, wout_ref[lo:lo + FF_CHUNK, :], preferred_element_type=jnp.float32)
    o_ref[...] = acc


def ffn_residual(x, g, w_in_bf16, w_out_bf16):
    rows = x.shape[0]
    return pl.pallas_call(
        _ffn_body,
        grid=(rows // ROW_TILE,),
        in_specs=[
            pl.BlockSpec((ROW_TILE, D_MODEL), lambda i: (i, 0)),
            pl.BlockSpec((1, D_MODEL), lambda i: (0, 0)),
            pl.BlockSpec((D_MODEL, 2 * D_FF), lambda i: (0, 0)),
            pl.BlockSpec((D_FF, D_MODEL), lambda i: (0, 0)),
        ],
        out_specs=pl.BlockSpec((ROW_TILE, D_MODEL), lambda i: (i, 0)),
        out_shape=jax.ShapeDtypeStruct((rows, D_MODEL), jnp.float32),
        compiler_params=pltpu.CompilerParams(
            dimension_semantics=("arbitrary",), vmem_limit_bytes=VMEM_LIMIT),
        name="ffn_residual",
    )(x, g.reshape(1, D_MODEL), w_in_bf16, w_out_bf16)


def _norm_proj_body(x_ref, g_ref, w_ref, o_ref):
    x = x_ref[...]
    xn = x * lax.rsqrt(jnp.mean(x * x, axis=-1, keepdims=True) + NORM_EPS) * g_ref[...]
    o_ref[...] = jnp.dot(xn.astype(MXU_DTYPE), w_ref[...], preferred_element_type=jnp.float32)


def norm_proj(x, g, w_b):
    rows, n_out = x.shape[0], w_b.shape[1]
    return pl.pallas_call(
        _norm_proj_body,
        grid=(rows // ROW_TILE,),
        in_specs=[pl.BlockSpec((ROW_TILE, D_MODEL), lambda i: (i, 0)),
                  pl.BlockSpec((1, D_MODEL), lambda i: (0, 0)),
                  pl.BlockSpec((D_MODEL, n_out), lambda i: (0, 0))],
        out_specs=pl.BlockSpec((ROW_TILE, n_out), lambda i: (i, 0)),
        out_shape=jax.ShapeDtypeStruct((rows, n_out), jnp.float32),
        compiler_params=pltpu.CompilerParams(dimension_semantics=("arbitrary",), vmem_limit_bytes=VMEM_LIMIT),
        name="norm_proj",
    )(x, g.reshape(1, D_MODEL), w_b)


def _proj_residual_body(a_ref, w_ref, r_ref, o_ref):
    o_ref[...] = r_ref[...] + jnp.dot(a_ref[...].astype(MXU_DTYPE), w_ref[...], preferred_element_type=jnp.float32)


def proj_residual(a, w_b, res):
    rows = a.shape[0]
    return pl.pallas_call(
        _proj_residual_body,
        grid=(rows // ROW_TILE,),
        in_specs=[pl.BlockSpec((ROW_TILE, Q_DIM), lambda i: (i, 0)),
                  pl.BlockSpec((Q_DIM, D_MODEL), lambda i: (0, 0)),
                  pl.BlockSpec((ROW_TILE, D_MODEL), lambda i: (i, 0))],
        out_specs=pl.BlockSpec((ROW_TILE, D_MODEL), lambda i: (i, 0)),
        out_shape=jax.ShapeDtypeStruct((rows, D_MODEL), jnp.float32),
        compiler_params=pltpu.CompilerParams(dimension_semantics=("arbitrary",), vmem_limit_bytes=VMEM_LIMIT),
        name="proj_residual",
    )(a, w_b, res)


def _causal_tiles(qi, tq, tk):
    return (qi * tq + tq + tk - 1) // tk


def _mflash_body(qt_ref, k_ref, vt_ref, bt_ref, o_ref, m_ref, l_ref, acc_ref, *, r, tq, tk):
    qi = pl.program_id(2)
    qt = qt_ref[0, 0, 0]
    m_ref[...] = jnp.full(m_ref.shape, NEG_INF, jnp.float32)
    l_ref[...] = jnp.zeros(l_ref.shape, jnp.float32)
    acc_ref[...] = jnp.zeros(acc_ref.shape, jnp.float32)

    def step(kt, carry):
        ks = pl.multiple_of(kt * tk, tk)
        k = k_ref[0, 0, pl.ds(ks, tk), :]
        vt = vt_ref[0, 0, :, pl.ds(ks, tk)]
        bias = bt_ref[0, 0, kt, 0].astype(jnp.float32)
        s = jnp.dot(k, qt, preferred_element_type=jnp.float32) + jnp.concatenate([bias] * r, axis=1)
        m_old = m_ref[...]
        m_new = jnp.maximum(m_old, jnp.max(s, axis=0, keepdims=True))
        p = jnp.exp(s - m_new)
        alpha = jnp.exp(m_old - m_new)
        l_ref[...] = alpha * l_ref[...] + jnp.sum(p, axis=0, keepdims=True)
        acc_ref[...] = alpha * acc_ref[...] + jnp.dot(vt, p.astype(MXU_DTYPE), preferred_element_type=jnp.float32)
        m_ref[...] = m_new
        return carry

    lax.fori_loop(0, _causal_tiles(qi, tq, tk), step, 0)
    o_ref[0, 0, 0] = acc_ref[...] / jnp.maximum(l_ref[...], 1e-30)


def masked_flash(q, k, v, allow_t):
    n, t, g, r, dh = q.shape
    gb = allow_t.shape[1]
    tq, tk = FLASH_ROWS // r, K_TILE
    nq, nkt = t // tq, t // tk
    qs = (q * (dh ** -0.5)).reshape(n, nq, tq, g, r, dh)
    qt = jnp.transpose(qs, (0, 3, 1, 5, 4, 2)).reshape(n, g, nq, dh, r * tq).astype(MXU_DTYPE)
    kb = jnp.transpose(k, (0, 2, 1, 3)).astype(MXU_DTYPE)
    vt = jnp.transpose(v, (0, 2, 3, 1)).astype(MXU_DTYPE)
    bt = jnp.where(allow_t, 0.0, NEG_INF).astype(BIAS_DTYPE).reshape(n, gb, nkt, tk, nq, tq)
    bt = jnp.transpose(bt, (0, 1, 2, 4, 3, 5))
    bmap = (lambda a, b, c: (a, b, 0, c, 0, 0)) if gb == g else (lambda a, b, c: (a, 0, 0, c, 0, 0))
    o = pl.pallas_call(
        functools.partial(_mflash_body, r=r, tq=tq, tk=tk),
        grid=(n, g, nq),
        in_specs=[
            pl.BlockSpec((1, 1, 1, dh, r * tq), lambda a, b, c: (a, b, c, 0, 0)),
            pl.BlockSpec((1, 1, t, dh), lambda a, b, c: (a, b, 0, 0)),
            pl.BlockSpec((1, 1, dh, t), lambda a, b, c: (a, b, 0, 0)),
            pl.BlockSpec((1, 1, nkt, 1, tk, tq), bmap),
        ],
        out_specs=pl.BlockSpec((1, 1, 1, dh, r * tq), lambda a, b, c: (a, b, c, 0, 0)),
        out_shape=jax.ShapeDtypeStruct((n, g, nq, dh, r * tq), jnp.float32),
        scratch_shapes=[pltpu.VMEM((1, r * tq), jnp.float32),
                        pltpu.VMEM((1, r * tq), jnp.float32),
                        pltpu.VMEM((dh, r * tq), jnp.float32)],
        compiler_params=pltpu.CompilerParams(
            dimension_semantics=("arbitrary", "arbitrary", "arbitrary"), vmem_limit_bytes=VMEM_LIMIT),
        name="masked_flash",
    )(qt, kb, vt, bt)
    o = o.reshape(n, g, nq, dh, r, tq)
    return jnp.transpose(o, (0, 2, 5, 1, 4, 3)).reshape(n, t, g, r, dh)


def _dsa_select_body(qx_ref, kx_ref, w_ref, o_ref, key_ref, *, tq, tk, top):
    qi = pl.program_id(1)
    nkt = key_ref.shape[0]
    n_valid = _causal_tiles(qi, tq, tk)
    qx = qx_ref[0].reshape(IDX_HEADS * tq, IDX_DIM)
    w = w_ref[0]
    row = qi * tq + lax.broadcasted_iota(jnp.int32, (tq, 1), 0)

    def causal(kt):
        col = kt * tk + lax.broadcasted_iota(jnp.int32, (1, tk), 1)
        return col <= row

    def score_step(kt, carry):
        ks = pl.multiple_of(kt * tk, tk)
        d = lax.dot_general(qx, kx_ref[0, pl.ds(ks, tk), :], _AT_BT, preferred_element_type=jnp.float32)
        d = jnp.maximum(d, 0.0).reshape(IDX_HEADS, tq, tk)
        sc = w[:, 0:1] * d[0]
        for h in range(1, IDX_HEADS):
            sc = sc + w[:, h:h + 1] * d[h]
        sc = jnp.where(causal(kt), sc, NEG_INF)
        bits = pltpu.bitcast(sc, jnp.int32)
        key = jnp.where(bits >= 0, bits, bits ^ 0x7FFFFFFF)
        key_ref[kt] = jnp.where(key == -1, 0, key)
        return carry

    lax.fori_loop(0, n_valid, score_step, 0)

    def count(bound, strict):
        bb = jnp.broadcast_to(bound, (tq, LANE))

        def body(kt, c):
            for b in range(tk // LANE):
                kk = key_ref[kt, :, b * LANE:(b + 1) * LANE]
                c = c + jnp.where((kk > bb) if strict else (kk >= bb), 1.0, 0.0)
            return c
        c = lax.fori_loop(0, n_valid, body, jnp.zeros((tq, LANE), jnp.float32))
        return jnp.sum(c, axis=1, keepdims=True)

    c0 = count(jnp.zeros((tq, 1), jnp.int32), False)
    tau = jnp.where(c0 >= top, 0, INT32_MIN).astype(jnp.int32)

    def bit_step(i, tau):
        cand = tau | jnp.left_shift(jnp.int32(1), 30 - i)
        return jnp.where(count(cand, False) >= top, cand, tau)

    tau = lax.fori_loop(0, 31, bit_step, tau)
    need = top - count(tau, True)
    ri = lax.broadcasted_iota(jnp.int32, (tk, tk), 0)
    ci = lax.broadcasted_iota(jnp.int32, (tk, tk), 1)
    tri = jnp.where(ri <= ci, 1.0, 0.0).astype(MXU_DTYPE)

    def out_step(kt, seen):
        kk = key_ref[kt]
        tie = kk == tau
        tie_f = jnp.where(tie, 1.0, 0.0)
        rank = seen + jnp.dot(tie_f.astype(MXU_DTYPE), tri, preferred_element_type=jnp.float32)
        sel = (kk > tau) | (tie & (rank <= need))
        o_ref[0, kt] = jnp.where(sel & causal(kt), 0.0, NEG_INF).astype(o_ref.dtype)
        return seen + jnp.sum(tie_f, axis=1, keepdims=True)

    lax.fori_loop(0, n_valid, out_step, jnp.zeros((tq, 1), jnp.float32))

    def fill_step(kt, carry):
        o_ref[0, kt] = jnp.full((tq, tk), NEG_INF, o_ref.dtype)
        return carry

    lax.fori_loop(n_valid, nkt, fill_step, 0)


def dsa_select_bias(q_idx, k_idx, w_idx, top):
    n, h, t, di = q_idx.shape
    nkt = t // K_TILE
    return pl.pallas_call(
        functools.partial(_dsa_select_body, tq=Q_TILE, tk=K_TILE, top=top),
        grid=(n, t // Q_TILE),
        in_specs=[
            pl.BlockSpec((1, h, Q_TILE, di), lambda a, c: (a, 0, c, 0)),
            pl.BlockSpec((1, t, di), lambda a, c: (a, 0, 0)),
            pl.BlockSpec((1, Q_TILE, h), lambda a, c: (a, c, 0)),
        ],
        out_specs=pl.BlockSpec((1, nkt, Q_TILE, K_TILE), lambda a, c: (a, 0, c, 0)),
        out_shape=jax.ShapeDtypeStruct((n, nkt, t, K_TILE), BIAS_DTYPE),
        scratch_shapes=[pltpu.VMEM((nkt, Q_TILE, K_TILE), jnp.int32)],
        compiler_params=pltpu.CompilerParams(
            dimension_semantics=("arbitrary", "arbitrary"), vmem_limit_bytes=VMEM_LIMIT),
        name="dsa_select",
    )(q_idx, k_idx, w_idx)


def _topk_mask_body(s_ref, o_ref, *, k):
    s0 = s_ref[...]
    lane = lax.broadcasted_iota(jnp.int32, s0.shape, 1).astype(jnp.float32)
    width = float(s0.shape[1])

    def step(_, carry):
        s, sel = carry
        m = jnp.max(s, axis=1, keepdims=True)
        first = jnp.min(jnp.where(s == m, lane, width), axis=1, keepdims=True)
        hit = lane == first
        return jnp.where(hit, REMOVED, s), jnp.where(hit, 1.0, sel)

    _, sel = lax.fori_loop(0, k, step, (s0, jnp.zeros(s0.shape, jnp.float32)))
    o_ref[...] = sel


def topk_mask(score, k):
    lead, n = score.shape[:-1], score.shape[-1]
    rows = int(np.prod(lead))
    width = -(-n // LANE) * LANE
    s2 = jnp.pad(score.reshape(rows, n), ((0, 0), (0, width - n)), constant_values=REMOVED)
    tm = min(TOPK_ROWS, rows)
    sel = pl.pallas_call(
        functools.partial(_topk_mask_body, k=k),
        grid=(rows // tm,),
        in_specs=[pl.BlockSpec((tm, width), lambda i: (i, 0))],
        out_specs=pl.BlockSpec((tm, width), lambda i: (i, 0)),
        out_shape=jax.ShapeDtypeStruct((rows, width), jnp.float32),
        compiler_params=pltpu.CompilerParams(dimension_semantics=("arbitrary",)),
        name="topk_mask",
    )(s2)
    return (sel[:, :n] > 0.5).reshape(lead + (n,))


def _nsa_select_mask(p_cmp, qpos, n_slc):
    n, t, g = p_cmp.shape[:3]
    ratio = NSA_SEL_BLOCK // NSA_CMP_BLOCK
    imp = p_cmp.sum(axis=3)
    imp = jnp.pad(imp, ((0, 0), (0, 0), (0, 0), (0, n_slc * ratio - imp.shape[-1])))
    imp = imp.reshape(n, t, g, n_slc, ratio).sum(-1)
    j = jnp.arange(n_slc)[None, :]
    cur = (qpos // NSA_SEL_BLOCK)[:, None]
    forced = ((j == 0) | (j == cur) | (j == cur - 1))[None, :, None, :]
    future = (j > cur)[None, :, None, :]
    score = jnp.where(future, NEG_INF, jnp.where(forced, imp + NSA_FORCE_BONUS, imp))
    return topk_mask(score, min(NSA_N_SEL, n_slc))


def _slab_query(q_ref, h, r, lane_lo):
    x = q_ref[0, :, (h // 2) * LANE:(h // 2 + 1) * LANE]
    half = (h // r) % 2
    if h % 2 != half:
        x = pltpu.roll(x, HEAD_DIM, axis=1)
    keep = lane_lo if half == 0 else jnp.logical_not(lane_lo)
    return jnp.where(keep, x, 0.0).astype(MXU_DTYPE)


def _pack_head_pair(o_a, o_b, h_a, r, lane_lo):
    half = (h_a // r) % 2
    a = o_a if half == 0 else pltpu.roll(o_a, HEAD_DIM, axis=1)
    b = o_b if half == 1 else pltpu.roll(o_b, HEAD_DIM, axis=1)
    return jnp.where(lane_lo, a, b)


def _nsa_cmp_body(q_ref, ck_ref, cv_ref, o_ref, blk_ref, *, tq, r, n_sel):
    qi = pl.program_id(1)
    nb = ck_ref.shape[1]
    n_slc = nb // 2
    t = qi * tq + lax.broadcasted_iota(jnp.int32, (tq, 1), 0)
    c = lax.broadcasted_iota(jnp.int32, (1, nb), 1)
    blk_id = jnp.where(c < n_slc, 2 * c, 2 * (c - n_slc) + 1)
    visible = (blk_id + 1) * NSA_CMP_BLOCK - 1 <= t
    ck, cv = ck_ref[0], cv_ref[0]
    imps = [jnp.zeros((tq, nb), jnp.float32) for _ in range(N_HEADS // r)]
    lane_lo = lax.broadcasted_iota(jnp.int32, (tq, LANE), 1) < HEAD_DIM
    pair = []
    for h in range(N_HEADS):
        s = lax.dot_general(_slab_query(q_ref, h, r, lane_lo), ck, _AT_BT, preferred_element_type=jnp.float32)
        s = jnp.where(visible, s, NEG_INF)
        m = jnp.max(s, axis=1, keepdims=True)
        e = jnp.where(visible, jnp.exp(s - m), 0.0)
        p = e / jnp.maximum(jnp.sum(e, axis=1, keepdims=True), 1e-30)
        pair.append(jnp.dot(p.astype(MXU_DTYPE), cv, preferred_element_type=jnp.float32))
        imps[h // r] = imps[h // r] + p
        if h % 2 == 1:
            o_ref[0, :, (h // 2) * LANE:(h // 2 + 1) * LANE] = _pack_head_pair(pair[0], pair[1], h - 1, r, lane_lo)
            pair = []
    j = lax.broadcasted_iota(jnp.int32, (1, n_slc), 1)
    lane = j.astype(jnp.float32)
    cur = t // NSA_SEL_BLOCK
    forced = (j == 0) | (j == cur) | (j == cur - 1)
    future = j > cur
    for gi, imp2 in enumerate(imps):
        imp = imp2[:, :n_slc] + imp2[:, n_slc:]
        score = jnp.where(future, NEG_INF, jnp.where(forced, imp + NSA_FORCE_BONUS, imp))

        def step(_, carry):
            s, sel = carry
            m = jnp.max(s, axis=1, keepdims=True)
            first = jnp.min(jnp.where(s == m, lane, float(n_slc)), axis=1, keepdims=True)
            hit = lane == first
            return jnp.where(hit, REMOVED, s), jnp.where(hit, 1.0, sel)

        _, sel = lax.fori_loop(0, n_sel, step, (score, jnp.zeros((tq, n_slc), jnp.float32)))
        blk_ref[0, :, gi * n_slc:(gi + 1) * n_slc] = sel


def nsa_cmp_select(q, ck, cv):
    n, t, g, r, dh = q.shape
    n_slc = -(-t // NSA_SEL_BLOCK)
    n_slc_pad = -(-n_slc // LANE) * LANE
    nb = ck.shape[1]

    def arrange(x):
        x = jnp.pad(x.reshape(n, nb, g * dh), ((0, 0), (0, 2 * n_slc_pad - nb), (0, 0)))
        return jnp.concatenate([x[:, 0::2], x[:, 1::2]], axis=1).astype(MXU_DTYPE)

    qw = g * r * dh
    o, blk = pl.pallas_call(
        functools.partial(_nsa_cmp_body, tq=Q_TILE, r=r, n_sel=min(NSA_N_SEL, n_slc)),
        grid=(n, t // Q_TILE),
        in_specs=[pl.BlockSpec((1, Q_TILE, qw), lambda a, c: (a, c, 0)),
                  pl.BlockSpec((1, 2 * n_slc_pad, g * dh), lambda a, c: (a, 0, 0)),
                  pl.BlockSpec((1, 2 * n_slc_pad, g * dh), lambda a, c: (a, 0, 0))],
        out_specs=[pl.BlockSpec((1, Q_TILE, qw), lambda a, c: (a, c, 0)),
                   pl.BlockSpec((1, Q_TILE, g * n_slc_pad), lambda a, c: (a, c, 0))],
        out_shape=[jax.ShapeDtypeStruct((n, t, qw), jnp.float32),
                   jax.ShapeDtypeStruct((n, t, g * n_slc_pad), jnp.float32)],
        compiler_params=pltpu.CompilerParams(
            dimension_semantics=("arbitrary", "arbitrary"), vmem_limit_bytes=VMEM_LIMIT),
        name="nsa_cmp_select",
    )((q * (dh ** -0.5)).reshape(n, t, qw), arrange(ck), arrange(cv))
    return o.reshape(n, t, g, r, dh), blk.reshape(n, t, g, n_slc_pad)[..., :n_slc] > 0.5


def _compress_body(*refs, n_pages):
    if n_pages:
        pages, (w_ref, o_ref, x_ref) = refs[1:1 + n_pages], refs[1 + n_pages:]
        for i in range(n_pages):
            for pi in range(x_ref.shape[0]):
                x_ref[pi, i * PAGE_SIZE:(i + 1) * PAGE_SIZE, :] = pages[i][pi * LANE:(pi + 1) * LANE, :].T
    else:
        in_ref, w_ref, o_ref, x_ref = refs
        for pi in range(x_ref.shape[0]):
            x_ref[pi] = in_ref[0, :, pi * LANE:(pi + 1) * LANE]
    n_planes = x_ref.shape[0]
    nb = o_ref.shape[1]
    acc = jnp.zeros(o_ref.shape[1:], jnp.float32)
    for c in range(NSA_CMP_BLOCK):
        rows_c = jnp.concatenate([x_ref[pi, pl.ds(c, nb, stride=NSA_CMP_BLOCK), :] for pi in range(n_planes)],
                                 axis=1).astype(MXU_DTYPE)
        acc = acc + jnp.dot(rows_c, w_ref[c], preferred_element_type=jnp.float32)
    o_ref[0] = acc


def _compress_weights(w_ck, w_cv):
    w2 = jnp.stack([w_ck, w_cv])
    g = w_ck.shape[0]
    eye = jnp.eye(2 * g, dtype=w2.dtype).reshape(2, g, 2, g)
    wbd = jnp.einsum('sgcde,sgtk->csgdtke', w2, eye)
    width = 2 * g * HEAD_DIM
    return wbd.reshape(NSA_CMP_BLOCK, width, width).astype(MXU_DTYPE)


def _split_summaries(out, g):
    n, nb, _ = out.shape
    half = g * HEAD_DIM
    return out[..., :half].reshape(n, nb, g, HEAD_DIM), out[..., half:].reshape(n, nb, g, HEAD_DIM)


def nsa_compress_rows(kv_rows, w_ck, w_cv):
    n, length, _ = kv_rows.shape
    g = w_ck.shape[0]
    width = 2 * g * HEAD_DIM
    nb = length // NSA_CMP_BLOCK
    out = pl.pallas_call(
        functools.partial(_compress_body, n_pages=0),
        grid=(n,),
        in_specs=[pl.BlockSpec((1, length, width), lambda a: (a, 0, 0)),
                  pl.BlockSpec((NSA_CMP_BLOCK, width, width), lambda a: (0, 0, 0))],
        out_specs=pl.BlockSpec((1, nb, width), lambda a: (a, 0, 0)),
        out_shape=jax.ShapeDtypeStruct((n, nb, width), jnp.float32),
        scratch_shapes=[pltpu.VMEM((width // LANE, length, LANE), jnp.float32)],
        compiler_params=pltpu.CompilerParams(dimension_semantics=("arbitrary",), vmem_limit_bytes=VMEM_LIMIT),
        name="nsa_compress_rows",
    )(kv_rows, _compress_weights(w_ck, w_cv))
    return _split_summaries(out, g)


def nsa_compress_paged(pool, layer, page_table, w_ck, w_cv):
    n, n_pages = page_table.shape
    g = w_ck.shape[0]
    width = 2 * g * HEAD_DIM
    nb = n_pages * PAGE_SIZE // NSA_CMP_BLOCK
    pool_flat = _feature_major_pages(pool)
    grid_spec = pltpu.PrefetchScalarGridSpec(
        num_scalar_prefetch=1,
        grid=(n,),
        in_specs=[pl.BlockSpec((None, None, width, PAGE_SIZE),
                               functools.partial(lambda a, pt, i: (layer, pt[a * n_pages + i], 0, 0), i=i))
                  for i in range(n_pages)]
                 + [pl.BlockSpec((NSA_CMP_BLOCK, width, width), lambda a, pt: (0, 0, 0))],
        out_specs=pl.BlockSpec((1, nb, width), lambda a, pt: (a, 0, 0)),
        scratch_shapes=[pltpu.VMEM((width // LANE, n_pages * PAGE_SIZE, LANE), jnp.float32)],
    )
    out = pl.pallas_call(
        functools.partial(_compress_body, n_pages=n_pages),
        grid_spec=grid_spec,
        out_shape=jax.ShapeDtypeStruct((n, nb, width), jnp.float32),
        compiler_params=pltpu.CompilerParams(dimension_semantics=("arbitrary",), vmem_limit_bytes=VMEM_LIMIT),
        name="nsa_compress_paged",
    )(page_table.reshape(-1), *([pool_flat] * n_pages), _compress_weights(w_ck, w_cv))
    return _split_summaries(out, g)


def _band_body(*refs, tq, n_prev, span, r, has_sink):
    nk = n_prev + 1
    q_ref, k_refs, v_refs = refs[0], refs[1:1 + nk], refs[1 + nk:1 + 2 * nk]
    sink_ref = refs[1 + 2 * nk] if has_sink else None
    o_ref, lse_ref = refs[-2:]
    ui = pl.program_id(2)
    i = lax.broadcasted_iota(jnp.int32, (tq, tq), 0)
    j = lax.broadcasted_iota(jnp.int32, (tq, tq), 1)
    parts = []
    for p in range(n_prev, -1, -1):
        diff = p * tq + i - j
        parts.append((diff >= 0) & (diff < span) & (ui >= p))
    allow = jnp.concatenate(parts, axis=1)
    lane = lax.broadcasted_iota(jnp.int32, (tq, LANE), 1)
    lse_tile = jnp.zeros((tq, LANE), jnp.float32)
    lane_lo = lane < HEAD_DIM
    pair = []
    for h in range(N_HEADS):
        gp = (h // r) // 2
        sl = slice(gp * LANE, (gp + 1) * LANE)
        k2 = jnp.concatenate([kr[0, :, sl] for kr in k_refs], axis=0)
        v2 = jnp.concatenate([vr[0, :, sl] for vr in v_refs], axis=0)
        s = lax.dot_general(_slab_query(q_ref, h, r, lane_lo), k2, _AT_BT, preferred_element_type=jnp.float32)
        s = jnp.where(allow, s, NEG_INF)
        m = jnp.max(s, axis=1, keepdims=True)
        if has_sink:
            m = jnp.maximum(m, sink_ref[h])
        e = jnp.exp(s - m)
        den = jnp.sum(e, axis=1, keepdims=True)
        if has_sink:
            den = den + jnp.exp(sink_ref[h] - m)
        den = jnp.maximum(den, 1e-30)
        pair.append(jnp.dot(e.astype(MXU_DTYPE), v2, preferred_element_type=jnp.float32) / den)
        if h % 2 == 1:
            o_ref[0, :, (h // 2) * LANE:(h // 2 + 1) * LANE] = _pack_head_pair(pair[0], pair[1], h - 1, r, lane_lo)
            pair = []
        lse_tile = jnp.where(lane == h, m + jnp.log(den), lse_tile)
    lse_ref[0] = lse_tile


def band_attn(q, k, v, dil, span, sink=None):
    n, t, g, r, dh = q.shape
    tq = Q_TILE
    n_prev = -(-(span - 1) // tq)
    assert r % 2 == 0 and g * r == N_HEADS
    qf = (q * (dh ** -0.5)).reshape(n, t, g * r * dh)
    kb = k.reshape(n, t, g * dh).astype(MXU_DTYPE)
    vb = v.reshape(n, t, g * dh).astype(MXU_DTYPE)
    qw, kw = g * r * dh, g * dh
    length = t // dil
    view = lambda x: x.reshape(n, length, dil * x.shape[-1])
    cur = lambda a, c, u: (a, u, c)
    back = [functools.partial(lambda a, c, u, p: (a, jnp.maximum(u - p, 0), c), p=p) for p in range(n_prev, -1, -1)]
    in_specs = ([pl.BlockSpec((1, tq, qw), cur)] + [pl.BlockSpec((1, tq, kw), b) for b in back] * 2)
    operands = [view(qf)] + [view(kb)] * (n_prev + 1) + [view(vb)] * (n_prev + 1)
    if sink is not None:
        in_specs.append(pl.BlockSpec(memory_space=pltpu.SMEM))
        operands.append(sink.astype(jnp.float32))
    o, lse = pl.pallas_call(
        functools.partial(_band_body, tq=tq, n_prev=n_prev, span=span, r=r, has_sink=sink is not None),
        grid=(n, dil, length // tq),
        in_specs=in_specs,
        out_specs=[pl.BlockSpec((1, tq, qw), cur), pl.BlockSpec((1, tq, LANE), cur)],
        out_shape=[jax.ShapeDtypeStruct((n, length, dil * qw), jnp.float32),
                   jax.ShapeDtypeStruct((n, length, dil * LANE), jnp.float32)],
        compiler_params=pltpu.CompilerParams(
            dimension_semantics=("arbitrary", "arbitrary", "arbitrary"), vmem_limit_bytes=VMEM_LIMIT),
        name="band_attn",
    )(*operands)
    return o.reshape(n, t, g, r, dh), lse.reshape(n, t, LANE)[:, :, :N_HEADS].reshape(n, t, g, r)


def nsa_prompt_p(z, w_ck, w_cv):
    n, s, _ = z.shape
    pos = jnp.arange(s)
    q, kv, gates = _nsa_project(z, pos)
    ck, cv = nsa_compress_rows(kv.reshape(n, s, -1), w_ck, w_cv)
    o_cmp, blk = nsa_cmp_select(q, ck, cv)
    blk_t = jnp.repeat(jnp.transpose(blk, (0, 2, 3, 1)), NSA_SEL_BLOCK, axis=2)[:, :, :s]
    allow_t = blk_t & (pos[:, None] <= pos[None, :])
    o_slc = masked_flash(q, kv[:, :, 2], kv[:, :, 3], allow_t)
    o_win, _ = band_attn(q, kv[:, :, 4], kv[:, :, 5], 1, NSA_WINDOW)
    return _nsa_gate(gates, o_cmp, o_slc, o_win), kv[:, :, 0:4], _tail(kv[:, :, 4:6], NSA_WINDOW)


def dsa_prompt_p(z):
    n, s, _ = z.shape
    pos = jnp.arange(s)
    q, kv, q_idx, k_idx, w_idx = _dsa_project(z, pos)
    top = min(DSA_TOPK, s // 4)
    bias = dsa_select_bias(jnp.transpose(q_idx, (0, 2, 1, 3)).astype(MXU_DTYPE),
                           k_idx.astype(MXU_DTYPE), w_idx, top)
    allow_t = (jnp.transpose(bias, (0, 1, 3, 2)) == 0).reshape(n, 1, s, s)
    o = masked_flash(q, kv[:, :, 0], kv[:, :, 1], allow_t)
    return o.reshape(n, s, Q_DIM), kv, k_idx


def dil_prompt_p(z):
    n, s, _ = z.shape
    q, kv = _dil_project(z, jnp.arange(s))
    outs, lses, bufs = [], [], []
    for gi, (win, dil) in enumerate(DIL_GROUPS):
        o, lse = band_attn(q, kv[:, :, gi, 0], kv[:, :, gi, 1], dil, win // dil + 1)
        outs.append(o)
        lses.append(lse)
        bufs.append(_tail(kv[:, :, gi], win))
    return _dil_mix(outs, lses).astype(z.dtype).reshape(n, s, Q_DIM), bufs


def swa_prompt_p(z, sink):
    n, s, _ = z.shape
    q, kv = _swa_project(z, jnp.arange(s))
    o, _ = band_attn(q, kv[:, :, 0], kv[:, :, 1], 1, SWA_WINDOW, sink)
    return o.reshape(n, s, Q_DIM), _tail(kv, SWA_WINDOW)


def _flash_update(s, pv, m_ref, l_ref, acc_ref):
    m_old = m_ref[...]
    m_new = jnp.maximum(m_old, jnp.max(s, axis=1, keepdims=True))
    p = jnp.exp(s - m_new)
    alpha = jnp.exp(m_old - m_new)
    l_ref[...] = alpha * l_ref[...] + jnp.sum(p, axis=1, keepdims=True)
    acc_ref[...] = alpha * acc_ref[...] + pv(p.astype(MXU_DTYPE))
    m_ref[...] = m_new


def _paged_attn_body(pt_ref, q_ref, bn_ref, kn_ref, vn_ref, b_ref, *rest, ppc):
    kp, vp = rest[:ppc], rest[ppc:2 * ppc]
    o_ref, m_ref, l_ref, acc_ref = rest[2 * ppc:]
    c = pl.program_id(1)
    q = q_ref[0]

    @pl.when(c == 0)
    def _():
        m_ref[...] = jnp.full(m_ref.shape, NEG_INF, jnp.float32)
        l_ref[...] = jnp.zeros(l_ref.shape, jnp.float32)
        acc_ref[...] = jnp.zeros(acc_ref.shape, jnp.float32)
        s = lax.dot_general(q, kn_ref[0], _AT_BT, preferred_element_type=jnp.float32) + bn_ref[0].astype(jnp.float32)
        _flash_update(s, lambda p: jnp.dot(p, vn_ref[0], preferred_element_type=jnp.float32), m_ref, l_ref, acc_ref)

    parts = [jnp.dot(q, kp[i][...].astype(MXU_DTYPE), preferred_element_type=jnp.float32) for i in range(ppc)]
    s = jnp.concatenate(parts, axis=1) + b_ref[0].astype(jnp.float32)

    def pv(p):
        acc = lax.dot_general(p[:, 0:PAGE_SIZE], vp[0][...].astype(MXU_DTYPE), _AT_BT,
                              preferred_element_type=jnp.float32)
        for i in range(1, ppc):
            acc = acc + lax.dot_general(p[:, i * PAGE_SIZE:(i + 1) * PAGE_SIZE], vp[i][...].astype(MXU_DTYPE), _AT_BT,
                                        preferred_element_type=jnp.float32)
        return acc

    _flash_update(s, pv, m_ref, l_ref, acc_ref)

    @pl.when(c == pl.num_programs(1) - 1)
    def _():
        o_ref[0] = acc_ref[...] / jnp.maximum(l_ref[...], 1e-30)


def paged_attn(q2, pool, layer, page_table, k_blk, v_blk, bias, k_new, v_new, bias_new):
    n, rows, lw = q2.shape
    n_pages = page_table.shape[1]
    ppc = min(PAGES_PER_CHUNK, n_pages)
    n_chunks = n_pages // ppc

    def page_spec(i, blk):
        return pl.BlockSpec((None, None, lw, PAGE_SIZE),
                            lambda a, c, pt: (layer, pt[a * n_pages + c * ppc + i], blk, 0))

    per_seq = lambda a, c, pt: (a, 0, 0)
    grid_spec = pltpu.PrefetchScalarGridSpec(
        num_scalar_prefetch=1,
        grid=(n, n_chunks),
        in_specs=[pl.BlockSpec((1, rows, lw), per_seq),
                  pl.BlockSpec((1, rows, PAGE_SIZE), per_seq),
                  pl.BlockSpec((1, PAGE_SIZE, lw), per_seq),
                  pl.BlockSpec((1, PAGE_SIZE, lw), per_seq),
                  pl.BlockSpec((1, rows, ppc * PAGE_SIZE), lambda a, c, pt: (a, 0, c))]
                 + [page_spec(i, k_blk) for i in range(ppc)]
                 + [page_spec(i, v_blk) for i in range(ppc)],
        out_specs=pl.BlockSpec((1, rows, lw), per_seq),
        scratch_shapes=[pltpu.VMEM((rows, 1), jnp.float32),
                        pltpu.VMEM((rows, 1), jnp.float32),
                        pltpu.VMEM((rows, lw), jnp.float32)],
    )
    return pl.pallas_call(
        functools.partial(_paged_attn_body, ppc=ppc),
        grid_spec=grid_spec,
        out_shape=jax.ShapeDtypeStruct((n, rows, lw), jnp.float32),
        compiler_params=pltpu.CompilerParams(
            dimension_semantics=("arbitrary", "arbitrary"), vmem_limit_bytes=VMEM_LIMIT),
        name="paged_attn",
    )(page_table.reshape(-1), q2, bias_new, k_new, v_new, bias, *([pool] * (2 * ppc)))


def _feature_major_pages(pool):
    nd = pool.ndim
    t = jnp.transpose(pool, (0, 1) + tuple(range(3, nd)) + (2,))
    return t.reshape(pool.shape[0], pool.shape[1], -1, pool.shape[2])


def _group_lane_queries(q):
    n, t, g, r, dh = q.shape
    qs = jnp.transpose(q * (dh ** -0.5), (0, 2, 1, 3, 4))
    onehot = jnp.eye(g, dtype=qs.dtype)
    q2 = qs[:, :, :, :, None, :] * onehot[None, :, None, None, :, None]
    return q2.reshape(n, g * t * r, g * dh).astype(MXU_DTYPE)


def _ungroup_lanes(o, t, g, r, dh):
    n = o.shape[0]
    o6 = o.reshape(n, g, t, r, g, dh)
    return jnp.stack([o6[:, gi, :, :, gi] for gi in range(g)], axis=2)


def _rows_bias(allow, g, r):
    n, t, ga, k = allow.shape
    a = jnp.broadcast_to(jnp.transpose(allow, (0, 2, 1, 3))[:, :, :, None, :], (n, ga, t, r, k))
    if ga != g:
        a = jnp.broadcast_to(a, (n, g, t, r, k))
    return jnp.where(a, 0.0, NEG_INF).astype(BIAS_DTYPE).reshape(n, g * t * r, k)


def _new_rows_page(x, lw):
    n, t, _ = x.shape
    return jnp.pad(x, ((0, 0), (0, PAGE_SIZE - t), (0, 0))).astype(MXU_DTYPE)


def nsa_sample_p(z, pool, layer, win_buf, page_table, w_ck, w_cv, past):
    n, t, _ = z.shape
    g, r = NSA_KV_HEADS, N_HEADS // NSA_KV_HEADS
    pos = past + jnp.arange(t)
    q, kv, gates = _nsa_project(z, pos)
    ck_p, cv_p = nsa_compress_paged(pool, layer, page_table, w_ck, w_cv)
    ck_n, cv_n = _nsa_compress(kv[:, :, 0:2], w_ck, w_cv)
    o_cmp, p_cmp = _nsa_cmp_attend(q, jnp.concatenate([ck_p, ck_n], 1),
                                   jnp.concatenate([cv_p, cv_n], 1), pos)
    n_slc = -(-(past + t) // NSA_SEL_BLOCK)
    blk = _nsa_select_mask(p_cmp, pos, n_slc)
    n_past_blk = past // NSA_SEL_BLOCK
    allow_past = jnp.repeat(blk[..., :n_past_blk], NSA_SEL_BLOCK, axis=-1)
    j = jnp.arange(PAGE_SIZE)
    new_blk = (past + j) // NSA_SEL_BLOCK
    allow_new = (jnp.take(blk, jnp.minimum(new_blk, n_slc - 1), axis=-1)
                 & (j[None, :] <= jnp.arange(t)[:, None])[None, :, None, :] & (j < t))
    lw = g * HEAD_DIM
    pool_flat = _feature_major_pages(pool)
    o = paged_attn(_group_lane_queries(q), pool_flat, layer, page_table, 2, 3,
                   _rows_bias(allow_past, g, r),
                   _new_rows_page(kv[:, :, 2].reshape(n, t, lw), lw),
                   _new_rows_page(kv[:, :, 3].reshape(n, t, lw), lw),
                   _rows_bias(allow_new, g, r))
    o_slc = _ungroup_lanes(o, t, g, r, HEAD_DIM)
    keys = jnp.concatenate([win_buf, kv[:, :, 4:6]], axis=1)
    kp = past - win_buf.shape[1] + jnp.arange(keys.shape[1])
    diff = pos[:, None] - kp[None, :]
    o_win, _ = _attend_dense(q, keys[:, :, 0], keys[:, :, 1], ((diff >= 0) & (diff < NSA_WINDOW))[None])
    return _nsa_gate(gates, o_cmp, o_slc, o_win), kv[:, :, 0:4], _tail(keys, NSA_WINDOW)


def _dsa_sample_select_body(pt_ref, qx_ref, w_ref, kn_ref, *rest, n_pages, t_new, top):
    pages = rest[:n_pages]
    o_ref, key_ref = rest[n_pages:]
    qx = qx_ref[0]
    w = w_ref[0]
    row = lax.broadcasted_iota(jnp.int32, (8, 1), 0)
    col = lax.broadcasted_iota(jnp.int32, (1, PAGE_SIZE), 1)
    new_ok = (col <= row) & (col < t_new)

    for i in range(n_pages + 1):
        if i == n_pages:
            d = lax.dot_general(qx, kn_ref[0], _AT_BT, preferred_element_type=jnp.float32)
        else:
            d = jnp.dot(qx, pages[i][...].astype(MXU_DTYPE), preferred_element_type=jnp.float32)
        d = jnp.maximum(d, 0.0).reshape(IDX_HEADS, 8, PAGE_SIZE)
        sc = w[:, 0:1] * d[0]
        for hh in range(1, IDX_HEADS):
            sc = sc + w[:, hh:hh + 1] * d[hh]
        if i == n_pages:
            sc = jnp.where(new_ok, sc, NEG_INF)
        bits = pltpu.bitcast(sc, jnp.int32)
        key = jnp.where(bits >= 0, bits, bits ^ 0x7FFFFFFF)
        key_ref[i] = jnp.where(key == -1, 0, key)

    def count(pred):
        hit = jnp.where(pred(key_ref[...]), 1.0, 0.0)
        return jnp.sum(jnp.sum(hit, axis=0), axis=1, keepdims=True)

    c0 = count(lambda kk: kk >= 0)
    tau = jnp.where(c0 >= top, 0, INT32_MIN).astype(jnp.int32)

    def bit_step(i, tau):
        cand = tau | jnp.left_shift(jnp.int32(1), 30 - i)
        return jnp.where(count(lambda kk: kk >= cand) >= top, cand, tau)

    tau = lax.fori_loop(0, 31, bit_step, tau)
    need = top - count(lambda kk: kk > tau)
    ri = lax.broadcasted_iota(jnp.int32, (PAGE_SIZE, PAGE_SIZE), 0)
    ci = lax.broadcasted_iota(jnp.int32, (PAGE_SIZE, PAGE_SIZE), 1)
    tri = jnp.where(ri <= ci, 1.0, 0.0).astype(MXU_DTYPE)
    seen = jnp.zeros((8, 1), jnp.float32)
    for i in range(n_pages + 1):
        kk = key_ref[i]
        tie = kk == tau
        tie_f = jnp.where(tie, 1.0, 0.0)
        rank = seen + jnp.dot(tie_f.astype(MXU_DTYPE), tri, preferred_element_type=jnp.float32)
        sel = (kk > tau) | (tie & (rank <= need))
        if i == n_pages:
            sel = sel & new_ok
        o_ref[0, :, i * PAGE_SIZE:(i + 1) * PAGE_SIZE] = jnp.where(sel, 0.0, NEG_INF)
        seen = seen + jnp.sum(tie_f, axis=1, keepdims=True)


def dsa_sample_select(q_idx, k_idx_new, w_idx, pool_idx, layer, page_table, top):
    n, t, hh, di = q_idx.shape
    n_pages = page_table.shape[1]
    qx = jnp.pad(jnp.transpose(q_idx, (0, 2, 1, 3)), ((0, 0), (0, 0), (0, 8 - t), (0, 0)))
    qx = qx.reshape(n, hh * 8, di).astype(MXU_DTYPE)
    w8 = jnp.pad(w_idx, ((0, 0), (0, 8 - t), (0, 0)))
    kn = _new_rows_page(k_idx_new, di)
    per_seq = lambda a, pt: (a, 0, 0)
    total = (n_pages + 1) * PAGE_SIZE
    grid_spec = pltpu.PrefetchScalarGridSpec(
        num_scalar_prefetch=1,
        grid=(n,),
        in_specs=[pl.BlockSpec((1, hh * 8, di), per_seq),
                  pl.BlockSpec((1, 8, hh), per_seq),
                  pl.BlockSpec((1, PAGE_SIZE, di), per_seq)]
                 + [pl.BlockSpec((None, None, di, PAGE_SIZE),
                                 functools.partial(lambda a, pt, i: (layer, pt[a * n_pages + i], 0, 0), i=i))
                    for i in range(n_pages)],
        out_specs=pl.BlockSpec((1, 8, total), per_seq),
        scratch_shapes=[pltpu.VMEM((n_pages + 1, 8, PAGE_SIZE), jnp.int32)],
    )
    return pl.pallas_call(
        functools.partial(_dsa_sample_select_body, n_pages=n_pages, t_new=t, top=top),
        grid_spec=grid_spec,
        out_shape=jax.ShapeDtypeStruct((n, 8, total), jnp.float32),
        compiler_params=pltpu.CompilerParams(
            dimension_semantics=("arbitrary",), vmem_limit_bytes=VMEM_LIMIT),
        name="dsa_sample_select",
    )(page_table.reshape(-1), qx, w8, kn, *([_feature_major_pages(pool_idx)] * n_pages))


def dsa_sample_p(z, pool_kv, pool_idx, layer, page_table, past):
    n, t, _ = z.shape
    g, r = DSA_KV_HEADS, N_HEADS // DSA_KV_HEADS
    pos = past + jnp.arange(t)
    q, kv, q_idx, k_idx, w_idx = _dsa_project(z, pos)
    total = past + t
    bias8 = dsa_sample_select(q_idx, k_idx, w_idx, pool_idx, layer, page_table, min(DSA_TOPK, total // 4))
    allow = (bias8[:, :t] == 0.0)[:, :, None, :]
    lw = g * HEAD_DIM
    pool_flat = _feature_major_pages(pool_kv)
    o = paged_attn(_group_lane_queries(q), pool_flat, layer, page_table, 0, 1,
                   _rows_bias(allow[..., :past], g, r),
                   _new_rows_page(kv[:, :, 0].reshape(n, t, lw), lw),
                   _new_rows_page(kv[:, :, 1].reshape(n, t, lw), lw),
                   _rows_bias(allow[..., past:], g, r))
    return _ungroup_lanes(o, t, g, r, HEAD_DIM).reshape(n, t, Q_DIM), kv, k_idx


def dil_sample(z, bufs, past):
    n, t, _ = z.shape
    pos = past + jnp.arange(t)
    q, kv = _dil_project(z, pos)
    outs, lses, new_bufs = [], [], []
    for gi, ((win, dil), buf) in enumerate(zip(DIL_GROUPS, bufs)):
        keys = jnp.concatenate([buf, kv[:, :, gi]], axis=1)
        base = past - buf.shape[1]
        local = pos[:, None] - (jnp.arange(win // dil + 1) * dil)[None, :] - base
        lc = jnp.maximum(local, 0)
        o, lse = _attend_gathered(q, keys[:, lc, 0], keys[:, lc, 1], (local >= 0)[None, :, :, None])
        outs.append(o)
        lses.append(lse)
        new_bufs.append(_tail(keys, win))
    return _dil_mix(outs, lses).astype(z.dtype).reshape(n, t, Q_DIM), new_bufs


def swa_sample(z, buf, sink, past):
    n, t, _ = z.shape
    pos = past + jnp.arange(t)
    q, kv = _swa_project(z, pos)
    keys = jnp.concatenate([buf, kv], axis=1)
    kp = past - buf.shape[1] + jnp.arange(keys.shape[1])
    diff = pos[:, None] - kp[None, :]
    o, _ = _attend_dense(q, keys[:, :, 0], keys[:, :, 1], ((diff >= 0) & (diff < SWA_WINDOW))[None],
                         sink.reshape(SWA_KV_HEADS, -1))
    return o.reshape(n, t, Q_DIM), _tail(keys, SWA_WINDOW)


def _pad_cols(w):
    n_out = w.shape[1]
    return jnp.pad(w, ((0, 0), (0, -n_out % LANE))).astype(MXU_DTYPE)


def _project(h, g, w_in):
    n, t, _ = h.shape
    z = norm_proj(h.reshape(n * t, D_MODEL), g, _pad_cols(w_in))
    return z[:, :w_in.shape[1]].reshape(n, t, w_in.shape[1])


def _add_out_proj(h, o, w_out):
    n, t, _ = h.shape
    return proj_residual(o.reshape(n * t, Q_DIM), w_out.astype(MXU_DTYPE), h.reshape(n * t, D_MODEL)).reshape(h.shape)


def _add_ffn(h, g, w_in_b, w_out_b):
    return ffn_residual(h.reshape(-1, D_MODEL), g, w_in_b, w_out_b).reshape(h.shape)


def kernel(x_prompt, x_sample, cache_nsa_kv, cache_nsa_win, cache_dil1, cache_dil2, cache_dil3,
           cache_dsa_kv, cache_dsa_idx, cache_swa, page_table,
           norm_mix, norm_ffn, norm_final, ffn_in, ffn_out,
           nsa_w_in, nsa_w_cmp_k, nsa_w_cmp_v, nsa_w_out,
           dil_w_in, dil_w_out, dsa_w_in, dsa_w_out,
           swa_w_in, swa_sink, swa_w_out):
    past = page_table.shape[1] * PAGE_SIZE
    hp, hs = x_prompt, x_sample
    ffn_in_b = ffn_in.astype(MXU_DTYPE)
    ffn_out_b = ffn_out.astype(MXU_DTYPE)
    st = {name: [] for name in ("nsa_kv_p", "nsa_kv_s", "nsa_win_p", "nsa_win_s",
                                "dil1_p", "dil1_s", "dil2_p", "dil2_s", "dil3_p", "dil3_s",
                                "dsa_kv_p", "dsa_kv_s", "dsa_idx_p", "dsa_idx_s", "swa_p", "swa_s")}
    for i in range(DEPTH):
        kind, j = i % N_MIXERS, i // N_MIXERS
        w_in, w_out = ((nsa_w_in, nsa_w_out), (dil_w_in, dil_w_out), (dsa_w_in, dsa_w_out), (swa_w_in, swa_w_out))[kind]
        zp, zs = _project(hp, norm_mix[i], w_in[j]), _project(hs, norm_mix[i], w_in[j])
        if kind == 0:
            op, kv_p, win_p = nsa_prompt_p(zp, nsa_w_cmp_k[j], nsa_w_cmp_v[j])
            os_, kv_s, win_s = nsa_sample_p(zs, cache_nsa_kv, j, cache_nsa_win[j], page_table,
                                            nsa_w_cmp_k[j], nsa_w_cmp_v[j], past)
            st["nsa_kv_p"].append(kv_p)
            st["nsa_kv_s"].append(kv_s)
            st["nsa_win_p"].append(win_p)
            st["nsa_win_s"].append(win_s)
        elif kind == 1:
            op, bufs_p = dil_prompt_p(zp)
            os_, bufs_s = dil_sample(zs, [cache_dil1[j], cache_dil2[j], cache_dil3[j]], past)
            for gi in range(len(DIL_GROUPS)):
                st["dil%d_p" % (gi + 1)].append(bufs_p[gi])
                st["dil%d_s" % (gi + 1)].append(bufs_s[gi])
        elif kind == 2:
            op, kv_p, idx_p = dsa_prompt_p(zp)
            os_, kv_s, idx_s = dsa_sample_p(zs, cache_dsa_kv, cache_dsa_idx, j, page_table, past)
            st["dsa_kv_p"].append(kv_p)
            st["dsa_kv_s"].append(kv_s)
            st["dsa_idx_p"].append(idx_p)
            st["dsa_idx_s"].append(idx_s)
        else:
            op, buf_p = swa_prompt_p(zp, swa_sink[j])
            os_, buf_s = swa_sample(zs, cache_swa[j], swa_sink[j], past)
            st["swa_p"].append(buf_p)
            st["swa_s"].append(buf_s)
        hp = _add_out_proj(hp, op, w_out[j])
        hs = _add_out_proj(hs, os_, w_out[j])
        hp = _add_ffn(hp, norm_ffn[i], ffn_in_b[i], ffn_out_b[i])
        hs = _add_ffn(hs, norm_ffn[i], ffn_in_b[i], ffn_out_b[i])
    y_prompt = rms_norm(hp, norm_final)
    y_sample = rms_norm(hs, norm_final)
    return (y_prompt, y_sample,
            jnp.stack(st["nsa_kv_p"]), jnp.stack(st["nsa_kv_s"]),
            jnp.stack(st["nsa_win_p"]), jnp.stack(st["nsa_win_s"]),
            jnp.stack(st["dil1_p"]), jnp.stack(st["dil1_s"]),
            jnp.stack(st["dil2_p"]), jnp.stack(st["dil2_s"]),
            jnp.stack(st["dil3_p"]), jnp.stack(st["dil3_s"]),
            jnp.stack(st["dsa_kv_p"]), jnp.stack(st["dsa_kv_s"]),
            jnp.stack(st["dsa_idx_p"]), jnp.stack(st["dsa_idx_s"]),
            jnp.stack(st["swa_p"]), jnp.stack(st["swa_s"]))
```

```python
import functools

import jax, jax.numpy as jnp
from jax import lax
import numpy as np
from jax.experimental import pallas as pl
from jax.experimental.pallas import tpu as pltpu

D_MODEL = 1024
BATCH = 2
SEQ = 8192
DEPTH = 4
DEC_BATCH = 128
DEC_SEQ = 4
PAST_LEN = 8192
PAGE_SIZE = 128

HEAD_DIM = 64
N_HEADS = D_MODEL // HEAD_DIM
Q_DIM = N_HEADS * HEAD_DIM
ROPE_THETA = 10000.0
NORM_EPS = 1e-6
N_MIXERS = 4
D_FF = -(-(8 * D_MODEL) // (3 * 256)) * 256
NEG_INF = -1e30

NSA_KV_HEADS = 2
NSA_CMP_BLOCK = 32
NSA_SEL_BLOCK = 64
NSA_N_SEL = 16
NSA_WINDOW = 512
NSA_FORCE_BONUS = 1e4
NSA_IN = Q_DIM + 6 * NSA_KV_HEADS * HEAD_DIM + 3 * N_HEADS

DIL_KV_HEADS = 4
DIL_GROUPS = ((128, 1), (512, 4), (2048, 16))
DIL_IN = Q_DIM + len(DIL_GROUPS) * 2 * DIL_KV_HEADS * HEAD_DIM

DSA_KV_HEADS = 4
IDX_HEADS = 8
IDX_DIM = 64
DSA_TOPK = 256
IDX_SCALE = (IDX_DIM * IDX_HEADS) ** -0.5
DSA_IN = Q_DIM + 2 * DSA_KV_HEADS * HEAD_DIM + IDX_HEADS * IDX_DIM + IDX_DIM + IDX_HEADS

SWA_KV_HEADS = 2
SWA_WINDOW = 128
SWA_IN = Q_DIM + 2 * SWA_KV_HEADS * HEAD_DIM


def rms_norm(x, g):
    xf = x.astype(jnp.float32)
    y = xf * lax.rsqrt(jnp.mean(xf * xf, axis=-1, keepdims=True) + NORM_EPS)
    return (y * g.astype(jnp.float32)).astype(x.dtype)


def rope(x, pos):
    half = x.shape[-1] // 2
    inv = ROPE_THETA ** (-jnp.arange(half, dtype=jnp.float32) / half)
    ang = pos.astype(jnp.float32)[:, None] * inv[None, :]
    shape = (pos.shape[0],) + (1,) * (x.ndim - 3) + (half,)
    cos, sin = jnp.cos(ang).reshape(shape), jnp.sin(ang).reshape(shape)
    xf = x.astype(jnp.float32)
    x1, x2 = xf[..., :half], xf[..., half:]
    return jnp.concatenate([x1 * cos - x2 * sin, x2 * cos + x1 * sin], axis=-1).astype(x.dtype)


def _tail(rows, window):
    n = rows.shape[1]
    return rows[:, n - min(window, n):]


def _masked_softmax(s, mask, sink=None):
    s = jnp.where(mask, s, NEG_INF)
    m = jnp.max(s, axis=-1, keepdims=True)
    if sink is not None:
        m = jnp.maximum(m, sink)
    e = jnp.where(mask, jnp.exp(s - m), 0.0)
    den = jnp.sum(e, axis=-1, keepdims=True)
    if sink is not None:
        den = den + jnp.exp(sink - m)
    den = jnp.maximum(den, 1e-30)
    return e / den, (m + jnp.log(den))[..., 0]


def _attend_dense(q, k, v, mask, sink=None):
    s = jnp.einsum('nqgrd,nkgd->ngrqk', q, k, preferred_element_type=jnp.float32) * (q.shape[-1] ** -0.5)
    sk = None if sink is None else sink.astype(jnp.float32)[None, :, :, None, None]
    p, lse = _masked_softmax(s, mask[:, None, None], sk)
    o = jnp.einsum('ngrqk,nkgd->nqgrd', p.astype(v.dtype), v)
    return o, jnp.transpose(lse, (0, 3, 1, 2))


def _attend_gathered(q, k, v, mask):
    s = jnp.einsum('nqgrd,nqkgd->nqgrk', q, k, preferred_element_type=jnp.float32) * (q.shape[-1] ** -0.5)
    p, lse = _masked_softmax(s, jnp.swapaxes(mask, 2, 3)[:, :, :, None, :])
    o = jnp.einsum('nqgrk,nqkgd->nqgrd', p.astype(v.dtype), v)
    return o, lse


def _nsa_project(z, pos):
    n, t, _ = z.shape
    g, r = NSA_KV_HEADS, N_HEADS // NSA_KV_HEADS
    kv_end = Q_DIM + 6 * g * HEAD_DIM
    q = rope(z[..., :Q_DIM].reshape(n, t, g, r, HEAD_DIM), pos)
    kv = z[..., Q_DIM:kv_end].reshape(n, t, 3, 2, g, HEAD_DIM)
    kv = jnp.stack([rope(kv[:, :, :, 0], pos), kv[:, :, :, 1]], axis=3).reshape(n, t, 6, g, HEAD_DIM)
    gates = jax.nn.sigmoid(z[..., kv_end:].astype(jnp.float32)).reshape(n, t, g, r, 3)
    return q, kv, gates


def _nsa_compress(rows, w_ck, w_cv):
    n, length, _, g, dh = rows.shape
    nb = length // NSA_CMP_BLOCK
    r = rows[:, :nb * NSA_CMP_BLOCK].reshape(n, nb, NSA_CMP_BLOCK, 2, g, dh)
    ck = jnp.einsum('nbcgd,gcde->nbge', r[:, :, :, 0], w_ck)
    cv = jnp.einsum('nbcgd,gcde->nbge', r[:, :, :, 1], w_cv)
    return ck, cv


def _nsa_cmp_attend(q, ck, cv, qpos):
    blk_end = (jnp.arange(ck.shape[1]) + 1) * NSA_CMP_BLOCK - 1
    mask = (blk_end[None, :] <= qpos[:, None])[None, :, None, None, :]
    s = jnp.einsum('nqgrd,ncgd->nqgrc', q, ck, preferred_element_type=jnp.float32) * (q.shape[-1] ** -0.5)
    p, _ = _masked_softmax(s, mask)
    o = jnp.einsum('nqgrc,ncgd->nqgrd', p.astype(cv.dtype), cv)
    return o, p


def _nsa_gate(gates, o_cmp, o_slc, o_win):
    f = jnp.float32
    o = (gates[..., 0:1] * o_cmp.astype(f) + gates[..., 1:2] * o_slc.astype(f)
         + gates[..., 2:3] * o_win.astype(f))
    n, t = o.shape[:2]
    return o.reshape(n, t, Q_DIM)


def _dil_project(z, pos):
    n, t, _ = z.shape
    g = DIL_KV_HEADS
    q = rope(z[..., :Q_DIM].reshape(n, t, g, N_HEADS // g, HEAD_DIM), pos)
    kv = z[..., Q_DIM:].reshape(n, t, len(DIL_GROUPS), 2, g, HEAD_DIM)
    return q, jnp.stack([rope(kv[:, :, :, 0], pos), kv[:, :, :, 1]], axis=3)


def _dil_mix(outs, lses):
    w = jax.nn.softmax(jnp.stack(lses), axis=0)
    return jnp.sum(w[..., None] * jnp.stack(outs).astype(jnp.float32), axis=0)


def _dsa_project(z, pos):
    n, t, _ = z.shape
    g = DSA_KV_HEADS
    o1 = Q_DIM
    o2 = o1 + 2 * g * HEAD_DIM
    o3 = o2 + IDX_HEADS * IDX_DIM
    o4 = o3 + IDX_DIM
    q = rope(z[..., :o1].reshape(n, t, g, N_HEADS // g, HEAD_DIM), pos)
    kv = z[..., o1:o2].reshape(n, t, 2, g, HEAD_DIM)
    kv = jnp.stack([rope(kv[:, :, 0], pos), kv[:, :, 1]], axis=2)
    q_idx = rope(z[..., o2:o3].reshape(n, t, IDX_HEADS, IDX_DIM), pos)
    k_idx = rope(z[..., o3:o4], pos)
    w_idx = z[..., o4:].astype(jnp.float32) * IDX_SCALE
    return q, kv, q_idx, k_idx, w_idx


def _swa_project(z, pos):
    n, t, _ = z.shape
    g = SWA_KV_HEADS
    q = rope(z[..., :Q_DIM].reshape(n, t, g, N_HEADS // g, HEAD_DIM), pos)
    kv = z[..., Q_DIM:].reshape(n, t, 2, g, HEAD_DIM)
    return q, jnp.stack([rope(kv[:, :, 0], pos), kv[:, :, 1]], axis=2)


MXU_DTYPE = jnp.bfloat16
BIAS_DTYPE = jnp.bfloat16
ROW_TILE = 512
FF_CHUNK = 256
VMEM_LIMIT = 56 * 1024 * 1024
LANE = 128
PAGES_PER_CHUNK = 32
Q_TILE = 128
K_TILE = 512
FLASH_ROWS = 1024
TOPK_ROWS = 256
REMOVED = -3e38
INT32_MIN = -2 ** 31
_AT_BT = (((1,), (1,)), ((), ()))


def _ffn_body(x_ref, g_ref, win_ref, wout_ref, o_ref):
    x = x_ref[...]
    xn = x * lax.rsqrt(jnp.mean(x * x, axis=-1, keepdims=True) + NORM_EPS) * g_ref[...]
    xb = xn.astype(MXU_DTYPE)
    acc = x
    for j in range(D_FF // FF_CHUNK):
        lo = j * FF_CHUNK
        gate = jnp.dot(xb, win_ref[:, lo:lo + FF_CHUNK], preferred_element_type=jnp.float32)
        up = jnp.dot(xb, win_ref[:, D_FF + lo:D_FF + lo + FF_CHUNK], preferred_element_type=jnp.float32)
        a = (gate * jax.nn.sigmoid(gate) * up).astype(MXU_DTYPE)
        acc = acc + jnp.dot(a, wout_ref[lo:lo + FF_CHUNK, :], preferred_element_type=jnp.float32)
    o_ref[...] = acc


def ffn_residual(x, g, w_in_bf16, w_out_bf16):
    rows = x.shape[0]
    return pl.pallas_call(
        _ffn_body,
        grid=(rows // ROW_TILE,),
        in_specs=[
            pl.BlockSpec((ROW_TILE, D_MODEL), lambda i: (i, 0)),
            pl.BlockSpec((1, D_MODEL), lambda i: (0, 0)),
            pl.BlockSpec((D_MODEL, 2 * D_FF), lambda i: (0, 0)),
            pl.BlockSpec((D_FF, D_MODEL), lambda i: (0, 0)),
        ],
        out_specs=pl.BlockSpec((ROW_TILE, D_MODEL), lambda i: (i, 0)),
        out_shape=jax.ShapeDtypeStruct((rows, D_MODEL), jnp.float32),
        compiler_params=pltpu.CompilerParams(
            dimension_semantics=("arbitrary",), vmem_limit_bytes=VMEM_LIMIT),
        name="ffn_residual",
    )(x, g.reshape(1, D_MODEL), w_in_bf16, w_out_bf16)


def _norm_proj_body(x_ref, g_ref, w_ref, o_ref):
    x = x_ref[...]
    xn = x * lax.rsqrt(jnp.mean(x * x, axis=-1, keepdims=True) + NORM_EPS) * g_ref[...]
    o_ref[...] = jnp.dot(xn.astype(MXU_DTYPE), w_ref[...], preferred_element_type=jnp.float32)


def norm_proj(x, g, w_b):
    rows, n_out = x.shape[0], w_b.shape[1]
    return pl.pallas_call(
        _norm_proj_body,
        grid=(rows // ROW_TILE,),
        in_specs=[pl.BlockSpec((ROW_TILE, D_MODEL), lambda i: (i, 0)),
                  pl.BlockSpec((1, D_MODEL), lambda i: (0, 0)),
                  pl.BlockSpec((D_MODEL, n_out), lambda i: (0, 0))],
        out_specs=pl.BlockSpec((ROW_TILE, n_out), lambda i: (i, 0)),
        out_shape=jax.ShapeDtypeStruct((rows, n_out), jnp.float32),
        compiler_params=pltpu.CompilerParams(dimension_semantics=("arbitrary",), vmem_limit_bytes=VMEM_LIMIT),
        name="norm_proj",
    )(x, g.reshape(1, D_MODEL), w_b)


def _proj_residual_body(a_ref, w_ref, r_ref, o_ref):
    o_ref[...] = r_ref[...] + jnp.dot(a_ref[...].astype(MXU_DTYPE), w_ref[...], preferred_element_type=jnp.float32)


def proj_residual(a, w_b, res):
    rows = a.shape[0]
    return pl.pallas_call(
        _proj_residual_body,
        grid=(rows // ROW_TILE,),
        in_specs=[pl.BlockSpec((ROW_TILE, Q_DIM), lambda i: (i, 0)),
                  pl.BlockSpec((Q_DIM, D_MODEL), lambda i: (0, 0)),
                  pl.BlockSpec((ROW_TILE, D_MODEL), lambda i: (i, 0))],
        out_specs=pl.BlockSpec((ROW_TILE, D_MODEL), lambda i: (i, 0)),
        out_shape=jax.ShapeDtypeStruct((rows, D_MODEL), jnp.float32),
        compiler_params=pltpu.CompilerParams(dimension_semantics=("arbitrary",), vmem_limit_bytes=VMEM_LIMIT),
        name="proj_residual",
    )(a, w_b, res)


def _causal_tiles(qi, tq, tk):
    return (qi * tq + tq + tk - 1) // tk


def _mflash_body(qt_ref, k_ref, vt_ref, bt_ref, o_ref, m_ref, l_ref, acc_ref, *, r, tq, tk, blk):
    qi = pl.program_id(2)
    qt = qt_ref[0, 0, 0]
    m_ref[...] = jnp.full(m_ref.shape, NEG_INF, jnp.float32)
    l_ref[...] = jnp.zeros(l_ref.shape, jnp.float32)
    acc_ref[...] = jnp.zeros(acc_ref.shape, jnp.float32)

    def step(kt, carry):
        ks = pl.multiple_of(kt * tk, tk)
        k = k_ref[0, 0, pl.ds(ks, tk), :]
        vt = vt_ref[0, 0, :, pl.ds(ks, tk)]
        if blk:
            per_tile = tk // blk
            rows = bt_ref[0, 0, pl.ds(pl.multiple_of(kt * per_tile, per_tile), per_tile), :]
            chosen = jnp.concatenate([jnp.broadcast_to(rows[b:b + 1, :], (blk, tq)) for b in range(per_tile)], axis=0)
            key_pos = ks + lax.broadcasted_iota(jnp.int32, (tk, 1), 0)
            q_pos = qi * tq + lax.broadcasted_iota(jnp.int32, (1, tq), 1)
            bias = jnp.where((chosen > 0.5) & (key_pos <= q_pos), 0.0, NEG_INF)
        else:
            bias = bt_ref[0, 0, kt, 0].astype(jnp.float32)
        s = jnp.dot(k, qt, preferred_element_type=jnp.float32) + jnp.concatenate([bias] * r, axis=1)
        m_old = m_ref[...]
        m_new = jnp.maximum(m_old, jnp.max(s, axis=0, keepdims=True))
        p = jnp.exp(s - m_new)
        alpha = jnp.exp(m_old - m_new)
        l_ref[...] = alpha * l_ref[...] + jnp.sum(p, axis=0, keepdims=True)
        acc_ref[...] = alpha * acc_ref[...] + jnp.dot(vt, p.astype(MXU_DTYPE), preferred_element_type=jnp.float32)
        m_ref[...] = m_new
        return carry

    lax.fori_loop(0, _causal_tiles(qi, tq, tk), step, 0)
    o_ref[0, 0, 0] = acc_ref[...] / jnp.maximum(l_ref[...], 1e-30)


def masked_flash(q, k, v, allow_t=None, blk=0, bias_tiles=None):
    n, t, g, r, dh = q.shape
    gb = (allow_t if bias_tiles is None else bias_tiles).shape[1]
    tq, tk = FLASH_ROWS // r, K_TILE
    nq, nkt = t // tq, t // tk
    qs = (q * (dh ** -0.5)).reshape(n, nq, tq, g, r, dh)
    qt = jnp.transpose(qs, (0, 3, 1, 5, 4, 2)).reshape(n, g, nq, dh, r * tq).astype(MXU_DTYPE)
    kb = jnp.transpose(k, (0, 2, 1, 3)).astype(MXU_DTYPE)
    vt = jnp.transpose(v, (0, 2, 3, 1)).astype(MXU_DTYPE)
    if bias_tiles is not None:
        bt = bias_tiles
        bmap = (lambda a, b, c: (a, b, 0, c, 0, 0)) if gb == g else (lambda a, b, c: (a, 0, 0, c, 0, 0))
        bspec = pl.BlockSpec((1, 1, nkt, 1, tk, tq), bmap)
    elif blk:
        bt = allow_t.astype(jnp.float32)
        bspec = pl.BlockSpec((1, 1, t // blk, tq), lambda a, b, c: (a, b, 0, c))
    else:
        bt = jnp.where(allow_t, 0.0, NEG_INF).astype(BIAS_DTYPE).reshape(n, gb, nkt, tk, nq, tq)
        bt = jnp.transpose(bt, (0, 1, 2, 4, 3, 5))
        bmap = (lambda a, b, c: (a, b, 0, c, 0, 0)) if gb == g else (lambda a, b, c: (a, 0, 0, c, 0, 0))
        bspec = pl.BlockSpec((1, 1, nkt, 1, tk, tq), bmap)
    o = pl.pallas_call(
        functools.partial(_mflash_body, r=r, tq=tq, tk=tk, blk=blk),
        grid=(n, g, nq),
        in_specs=[
            pl.BlockSpec((1, 1, 1, dh, r * tq), lambda a, b, c: (a, b, c, 0, 0)),
            pl.BlockSpec((1, 1, t, dh), lambda a, b, c: (a, b, 0, 0)),
            pl.BlockSpec((1, 1, dh, t), lambda a, b, c: (a, b, 0, 0)),
            bspec,
        ],
        out_specs=pl.BlockSpec((1, 1, 1, dh, r * tq), lambda a, b, c: (a, b, c, 0, 0)),
        out_shape=jax.ShapeDtypeStruct((n, g, nq, dh, r * tq), jnp.float32),
        scratch_shapes=[pltpu.VMEM((1, r * tq), jnp.float32),
                        pltpu.VMEM((1, r * tq), jnp.float32),
                        pltpu.VMEM((dh, r * tq), jnp.float32)],
        compiler_params=pltpu.CompilerParams(
            dimension_semantics=("arbitrary", "arbitrary", "arbitrary"), vmem_limit_bytes=VMEM_LIMIT),
        name="masked_flash",
    )(qt, kb, vt, bt)
    o = o.reshape(n, g, nq, dh, r, tq)
    return jnp.transpose(o, (0, 2, 5, 1, 4, 3)).reshape(n, t, g, r, dh)


def _dsa_select_body(qx_ref, kx_ref, w_ref, o_ref, key_ref, *, tq, tk, top):
    qi = pl.program_id(1)
    nkt = key_ref.shape[0]
    n_valid = _causal_tiles(qi, tq, tk)
    qx = qx_ref[0].reshape(IDX_HEADS * tq, IDX_DIM)
    w = w_ref[0]
    row = qi * tq + lax.broadcasted_iota(jnp.int32, (tq, 1), 0)

    def causal(kt):
        col = kt * tk + lax.broadcasted_iota(jnp.int32, (1, tk), 1)
        return col <= row

    def score_step(kt, carry):
        ks = pl.multiple_of(kt * tk, tk)
        d = lax.dot_general(qx, kx_ref[0, pl.ds(ks, tk), :], _AT_BT, preferred_element_type=jnp.float32)
        d = jnp.maximum(d, 0.0).reshape(IDX_HEADS, tq, tk)
        sc = w[:, 0:1] * d[0]
        for h in range(1, IDX_HEADS):
            sc = sc + w[:, h:h + 1] * d[h]
        sc = jnp.where(causal(kt), sc, NEG_INF)
        bits = pltpu.bitcast(sc, jnp.int32)
        key = jnp.where(bits >= 0, bits, bits ^ 0x7FFFFFFF)
        key_ref[kt] = jnp.where(key == -1, 0, key)
        return carry

    lax.fori_loop(0, n_valid, score_step, 0)

    def count(bound, strict):
        bb = jnp.broadcast_to(bound, (tq, LANE))

        def body(kt, c):
            for b in range(tk // LANE):
                kk = key_ref[kt, :, b * LANE:(b + 1) * LANE]
                c = c + jnp.where((kk > bb) if strict else (kk >= bb), 1.0, 0.0)
            return c
        c = lax.fori_loop(0, n_valid, body, jnp.zeros((tq, LANE), jnp.float32))
        return jnp.sum(c, axis=1, keepdims=True)

    c0 = count(jnp.zeros((tq, 1), jnp.int32), False)
    tau = jnp.where(c0 >= top, 0, INT32_MIN).astype(jnp.int32)

    def bit_step(i, tau):
        cand = tau | jnp.left_shift(jnp.int32(1), 30 - i)
        return jnp.where(count(cand, False) >= top, cand, tau)

    tau = lax.fori_loop(0, 31, bit_step, tau)
    need = top - count(tau, True)
    ri = lax.broadcasted_iota(jnp.int32, (tk, tk), 0)
    ci = lax.broadcasted_iota(jnp.int32, (tk, tk), 1)
    tri = jnp.where(ri <= ci, 1.0, 0.0).astype(MXU_DTYPE)

    def out_step(kt, seen):
        kk = key_ref[kt]
        tie = kk == tau
        tie_f = jnp.where(tie, 1.0, 0.0)
        rank = seen + jnp.dot(tie_f.astype(MXU_DTYPE), tri, preferred_element_type=jnp.float32)
        sel = (kk > tau) | (tie & (rank <= need))
        o_ref[0, 0, kt, 0] = jnp.where(sel & causal(kt), 0.0, NEG_INF).T.astype(o_ref.dtype)
        return seen + jnp.sum(tie_f, axis=1, keepdims=True)

    lax.fori_loop(0, n_valid, out_step, jnp.zeros((tq, 1), jnp.float32))

    def fill_step(kt, carry):
        o_ref[0, 0, kt, 0] = jnp.full((tk, tq), NEG_INF, o_ref.dtype)
        return carry

    lax.fori_loop(n_valid, nkt, fill_step, 0)


def dsa_select_bias(q_idx, k_idx, w_idx, top, tq_out):
    n, h, t, di = q_idx.shape
    nkt = t // K_TILE
    per_out = tq_out // Q_TILE
    return pl.pallas_call(
        functools.partial(_dsa_select_body, tq=Q_TILE, tk=K_TILE, top=top),
        grid=(n, t // Q_TILE),
        in_specs=[
            pl.BlockSpec((1, h, Q_TILE, di), lambda a, c: (a, 0, c, 0)),
            pl.BlockSpec((1, t, di), lambda a, c: (a, 0, 0)),
            pl.BlockSpec((1, Q_TILE, h), lambda a, c: (a, c, 0)),
        ],
        out_specs=pl.BlockSpec((1, 1, nkt, 1, K_TILE, Q_TILE), lambda a, c: (a, 0, 0, c // per_out, 0, c % per_out)),
        out_shape=jax.ShapeDtypeStruct((n, 1, nkt, t // tq_out, K_TILE, tq_out), BIAS_DTYPE),
        scratch_shapes=[pltpu.VMEM((nkt, Q_TILE, K_TILE), jnp.int32)],
        compiler_params=pltpu.CompilerParams(
            dimension_semantics=("arbitrary", "arbitrary"), vmem_limit_bytes=VMEM_LIMIT),
        name="dsa_select",
    )(q_idx, k_idx, w_idx)


def _topk_mask_body(s_ref, o_ref, *, k):
    s0 = s_ref[...]
    lane = lax.broadcasted_iota(jnp.int32, s0.shape, 1).astype(jnp.float32)
    width = float(s0.shape[1])

    def step(_, carry):
        s, sel = carry
        m = jnp.max(s, axis=1, keepdims=True)
        first = jnp.min(jnp.where(s == m, lane, width), axis=1, keepdims=True)
        hit = lane == first
        return jnp.where(hit, REMOVED, s), jnp.where(hit, 1.0, sel)

    _, sel = lax.fori_loop(0, k, step, (s0, jnp.zeros(s0.shape, jnp.float32)))
    o_ref[...] = sel


def topk_mask(score, k):
    lead, n = score.shape[:-1], score.shape[-1]
    rows = int(np.prod(lead))
    width = -(-n // LANE) * LANE
    s2 = jnp.pad(score.reshape(rows, n), ((0, 0), (0, width - n)), constant_values=REMOVED)
    tm = min(TOPK_ROWS, rows)
    sel = pl.pallas_call(
        functools.partial(_topk_mask_body, k=k),
        grid=(rows // tm,),
        in_specs=[pl.BlockSpec((tm, width), lambda i: (i, 0))],
        out_specs=pl.BlockSpec((tm, width), lambda i: (i, 0)),
        out_shape=jax.ShapeDtypeStruct((rows, width), jnp.float32),
        compiler_params=pltpu.CompilerParams(dimension_semantics=("arbitrary",)),
        name="topk_mask",
    )(s2)
    return (sel[:, :n] > 0.5).reshape(lead + (n,))


def _nsa_select_mask(p_cmp, qpos, n_slc):
    n, t, g = p_cmp.shape[:3]
    ratio = NSA_SEL_BLOCK // NSA_CMP_BLOCK
    imp = p_cmp.sum(axis=3)
    imp = jnp.pad(imp, ((0, 0), (0, 0), (0, 0), (0, n_slc * ratio - imp.shape[-1])))
    imp = imp.reshape(n, t, g, n_slc, ratio).sum(-1)
    j = jnp.arange(n_slc)[None, :]
    cur = (qpos // NSA_SEL_BLOCK)[:, None]
    forced = ((j == 0) | (j == cur) | (j == cur - 1))[None, :, None, :]
    future = (j > cur)[None, :, None, :]
    score = jnp.where(future, NEG_INF, jnp.where(forced, imp + NSA_FORCE_BONUS, imp))
    return topk_mask(score, min(NSA_N_SEL, n_slc))


def _slab_query(q_ref, h, r, lane_lo):
    x = q_ref[0, :, (h // 2) * LANE:(h // 2 + 1) * LANE]
    half = (h // r) % 2
    if h % 2 != half:
        x = pltpu.roll(x, HEAD_DIM, axis=1)
    keep = lane_lo if half == 0 else jnp.logical_not(lane_lo)
    return jnp.where(keep, x, 0.0).astype(MXU_DTYPE)


def _pack_head_pair(o_a, o_b, h_a, r, lane_lo):
    half = (h_a // r) % 2
    a = o_a if half == 0 else pltpu.roll(o_a, HEAD_DIM, axis=1)
    b = o_b if half == 1 else pltpu.roll(o_b, HEAD_DIM, axis=1)
    return jnp.where(lane_lo, a, b)


def _nsa_cmp_body(q_ref, ck_ref, cv_ref, o_ref, blk_ref, *, tq, r, n_sel):
    qi = pl.program_id(1)
    nb = ck_ref.shape[1]
    n_slc = nb // 2
    t = qi * tq + lax.broadcasted_iota(jnp.int32, (tq, 1), 0)
    c = lax.broadcasted_iota(jnp.int32, (1, nb), 1)
    blk_id = jnp.where(c < n_slc, 2 * c, 2 * (c - n_slc) + 1)
    visible = (blk_id + 1) * NSA_CMP_BLOCK - 1 <= t
    ck, cv = ck_ref[0], cv_ref[0]
    imps = [jnp.zeros((tq, nb), jnp.float32) for _ in range(N_HEADS // r)]
    lane_lo = lax.broadcasted_iota(jnp.int32, (tq, LANE), 1) < HEAD_DIM
    pair = []
    for h in range(N_HEADS):
        s = lax.dot_general(_slab_query(q_ref, h, r, lane_lo), ck, _AT_BT, preferred_element_type=jnp.float32)
        s = jnp.where(visible, s, NEG_INF)
        m = jnp.max(s, axis=1, keepdims=True)
        e = jnp.where(visible, jnp.exp(s - m), 0.0)
        p = e / jnp.maximum(jnp.sum(e, axis=1, keepdims=True), 1e-30)
        pair.append(jnp.dot(p.astype(MXU_DTYPE), cv, preferred_element_type=jnp.float32))
        imps[h // r] = imps[h // r] + p
        if h % 2 == 1:
            o_ref[0, :, (h // 2) * LANE:(h // 2 + 1) * LANE] = _pack_head_pair(pair[0], pair[1], h - 1, r, lane_lo)
            pair = []
    j = lax.broadcasted_iota(jnp.int32, (1, n_slc), 1)
    lane = j.astype(jnp.float32)
    cur = t // NSA_SEL_BLOCK
    forced = (j == 0) | (j == cur) | (j == cur - 1)
    future = j > cur
    scores = []
    for imp2 in imps:
        imp = imp2[:, :n_slc] + imp2[:, n_slc:]
        scores.append(jnp.where(future, NEG_INF, jnp.where(forced, imp + NSA_FORCE_BONUS, imp)))
    score = jnp.concatenate(scores, axis=0)

    def step(_, carry):
        s, sel = carry
        m = jnp.max(s, axis=1, keepdims=True)
        first = jnp.min(jnp.where(s == m, lane, float(n_slc)), axis=1, keepdims=True)
        hit = lane == first
        return jnp.where(hit, REMOVED, s), jnp.where(hit, 1.0, sel)

    _, sel = lax.fori_loop(0, n_sel, step, (score, jnp.zeros(score.shape, jnp.float32)))
    for gi in range(len(imps)):
        blk_ref[0, :, gi * n_slc:(gi + 1) * n_slc] = sel[gi * tq:(gi + 1) * tq]


def nsa_cmp_select(q, ck, cv):
    n, t, g, r, dh = q.shape
    n_slc = -(-t // NSA_SEL_BLOCK)
    n_slc_pad = -(-n_slc // LANE) * LANE
    nb = ck.shape[1]

    def arrange(x):
        x = jnp.pad(x.reshape(n, nb, g * dh), ((0, 0), (0, 2 * n_slc_pad - nb), (0, 0)))
        return jnp.concatenate([x[:, 0::2], x[:, 1::2]], axis=1).astype(MXU_DTYPE)

    qw = g * r * dh
    o, blk = pl.pallas_call(
        functools.partial(_nsa_cmp_body, tq=Q_TILE, r=r, n_sel=min(NSA_N_SEL, n_slc)),
        grid=(n, t // Q_TILE),
        in_specs=[pl.BlockSpec((1, Q_TILE, qw), lambda a, c: (a, c, 0)),
                  pl.BlockSpec((1, 2 * n_slc_pad, g * dh), lambda a, c: (a, 0, 0)),
                  pl.BlockSpec((1, 2 * n_slc_pad, g * dh), lambda a, c: (a, 0, 0))],
        out_specs=[pl.BlockSpec((1, Q_TILE, qw), lambda a, c: (a, c, 0)),
                   pl.BlockSpec((1, Q_TILE, g * n_slc_pad), lambda a, c: (a, c, 0))],
        out_shape=[jax.ShapeDtypeStruct((n, t, qw), jnp.float32),
                   jax.ShapeDtypeStruct((n, t, g * n_slc_pad), jnp.float32)],
        compiler_params=pltpu.CompilerParams(
            dimension_semantics=("arbitrary", "arbitrary"), vmem_limit_bytes=VMEM_LIMIT),
        name="nsa_cmp_select",
    )((q * (dh ** -0.5)).reshape(n, t, qw), arrange(ck), arrange(cv))
    return o.reshape(n, t, g, r, dh), blk.reshape(n, t, g, n_slc_pad)[..., :n_slc] > 0.5


def _compress_body(*refs, n_pages):
    if n_pages:
        pages, (w_ref, o_ref, x_ref) = refs[1:1 + n_pages], refs[1 + n_pages:]
        for i in range(n_pages):
            for pi in range(x_ref.shape[0]):
                x_ref[pi, i * PAGE_SIZE:(i + 1) * PAGE_SIZE, :] = pages[i][pi * LANE:(pi + 1) * LANE, :].T
    else:
        in_ref, w_ref, o_ref, x_ref = refs
        for pi in range(x_ref.shape[0]):
            x_ref[pi] = in_ref[0, :, pi * LANE:(pi + 1) * LANE]
    n_planes = x_ref.shape[0]
    nb = o_ref.shape[1]
    acc = jnp.zeros(o_ref.shape[1:], jnp.float32)
    for c in range(NSA_CMP_BLOCK):
        rows_c = jnp.concatenate([x_ref[pi, pl.ds(c, nb, stride=NSA_CMP_BLOCK), :] for pi in range(n_planes)],
                                 axis=1).astype(MXU_DTYPE)
        acc = acc + jnp.dot(rows_c, w_ref[c], preferred_element_type=jnp.float32)
    o_ref[0] = acc


def _compress_weights(w_ck, w_cv):
    w2 = jnp.stack([w_ck, w_cv])
    g = w_ck.shape[0]
    eye = jnp.eye(2 * g, dtype=w2.dtype).reshape(2, g, 2, g)
    wbd = jnp.einsum('sgcde,sgtk->csgdtke', w2, eye)
    width = 2 * g * HEAD_DIM
    return wbd.reshape(NSA_CMP_BLOCK, width, width).astype(MXU_DTYPE)


def _split_summaries(out, g):
    n, nb, _ = out.shape
    half = g * HEAD_DIM
    return out[..., :half].reshape(n, nb, g, HEAD_DIM), out[..., half:].reshape(n, nb, g, HEAD_DIM)


def nsa_compress_rows(kv_rows, w_ck, w_cv):
    n, length, _ = kv_rows.shape
    g = w_ck.shape[0]
    width = 2 * g * HEAD_DIM
    nb = length // NSA_CMP_BLOCK
    out = pl.pallas_call(
        functools.partial(_compress_body, n_pages=0),
        grid=(n,),
        in_specs=[pl.BlockSpec((1, length, width), lambda a: (a, 0, 0)),
                  pl.BlockSpec((NSA_CMP_BLOCK, width, width), lambda a: (0, 0, 0))],
        out_specs=pl.BlockSpec((1, nb, width), lambda a: (a, 0, 0)),
        out_shape=jax.ShapeDtypeStruct((n, nb, width), jnp.float32),
        scratch_shapes=[pltpu.VMEM((width // LANE, length, LANE), jnp.float32)],
        compiler_params=pltpu.CompilerParams(dimension_semantics=("arbitrary",), vmem_limit_bytes=VMEM_LIMIT),
        name="nsa_compress_rows",
    )(kv_rows, _compress_weights(w_ck, w_cv))
    return _split_summaries(out, g)


def nsa_compress_paged(pool, layer, page_table, w_ck, w_cv):
    n, n_pages = page_table.shape
    g = w_ck.shape[0]
    width = 2 * g * HEAD_DIM
    nb = n_pages * PAGE_SIZE // NSA_CMP_BLOCK
    pool_flat = _feature_major_pages(pool)
    grid_spec = pltpu.PrefetchScalarGridSpec(
        num_scalar_prefetch=1,
        grid=(n,),
        in_specs=[pl.BlockSpec((None, None, width, PAGE_SIZE),
                               functools.partial(lambda a, pt, i: (layer, pt[a * n_pages + i], 0, 0), i=i))
                  for i in range(n_pages)]
                 + [pl.BlockSpec((NSA_CMP_BLOCK, width, width), lambda a, pt: (0, 0, 0))],
        out_specs=pl.BlockSpec((1, nb, width), lambda a, pt: (a, 0, 0)),
        scratch_shapes=[pltpu.VMEM((width // LANE, n_pages * PAGE_SIZE, LANE), jnp.float32)],
    )
    out = pl.pallas_call(
        functools.partial(_compress_body, n_pages=n_pages),
        grid_spec=grid_spec,
        out_shape=jax.ShapeDtypeStruct((n, nb, width), jnp.float32),
        compiler_params=pltpu.CompilerParams(dimension_semantics=("arbitrary",), vmem_limit_bytes=VMEM_LIMIT),
        name="nsa_compress_paged",
    )(page_table.reshape(-1), *([pool_flat] * n_pages), _compress_weights(w_ck, w_cv))
    return _split_summaries(out, g)


def _band_body(*refs, tq, n_prev, span, r, has_sink):
    nk = n_prev + 1
    q_ref, k_refs, v_refs = refs[0], refs[1:1 + nk], refs[1 + nk:1 + 2 * nk]
    sink_ref = refs[1 + 2 * nk] if has_sink else None
    o_ref, lse_ref = refs[-2:]
    ui = pl.program_id(2)
    i = lax.broadcasted_iota(jnp.int32, (tq, tq), 0)
    j = lax.broadcasted_iota(jnp.int32, (tq, tq), 1)
    parts = []
    for p in range(n_prev, -1, -1):
        diff = p * tq + i - j
        parts.append((diff >= 0) & (diff < span) & (ui >= p))
    allow = jnp.concatenate(parts, axis=1)
    lane = lax.broadcasted_iota(jnp.int32, (tq, LANE), 1)
    lse_tile = jnp.zeros((tq, LANE), jnp.float32)
    lane_lo = lane < HEAD_DIM
    pair = []
    for h in range(N_HEADS):
        gp = (h // r) // 2
        sl = slice(gp * LANE, (gp + 1) * LANE)
        k2 = jnp.concatenate([kr[0, :, sl] for kr in k_refs], axis=0)
        v2 = jnp.concatenate([vr[0, :, sl] for vr in v_refs], axis=0)
        s = lax.dot_general(_slab_query(q_ref, h, r, lane_lo), k2, _AT_BT, preferred_element_type=jnp.float32)
        s = jnp.where(allow, s, NEG_INF)
        m = jnp.max(s, axis=1, keepdims=True)
        if has_sink:
            m = jnp.maximum(m, sink_ref[h])
        e = jnp.exp(s - m)
        den = jnp.sum(e, axis=1, keepdims=True)
        if has_sink:
            den = den + jnp.exp(sink_ref[h] - m)
        den = jnp.maximum(den, 1e-30)
        pair.append(jnp.dot(e.astype(MXU_DTYPE), v2, preferred_element_type=jnp.float32) / den)
        if h % 2 == 1:
            o_ref[0, :, (h // 2) * LANE:(h // 2 + 1) * LANE] = _pack_head_pair(pair[0], pair[1], h - 1, r, lane_lo)
            pair = []
        lse_tile = jnp.where(lane == h, m + jnp.log(den), lse_tile)
    lse_ref[0] = lse_tile


def band_attn(q, k, v, dil, span, sink=None):
    n, t, g, r, dh = q.shape
    tq = Q_TILE
    n_prev = -(-(span - 1) // tq)
    assert r % 2 == 0 and g * r == N_HEADS
    qf = (q * (dh ** -0.5)).reshape(n, t, g * r * dh)
    kb = k.reshape(n, t, g * dh).astype(MXU_DTYPE)
    vb = v.reshape(n, t, g * dh).astype(MXU_DTYPE)
    qw, kw = g * r * dh, g * dh
    length = t // dil
    view = lambda x: x.reshape(n, length, dil * x.shape[-1])
    cur = lambda a, c, u: (a, u, c)
    back = [functools.partial(lambda a, c, u, p: (a, jnp.maximum(u - p, 0), c), p=p) for p in range(n_prev, -1, -1)]
    in_specs = ([pl.BlockSpec((1, tq, qw), cur)] + [pl.BlockSpec((1, tq, kw), b) for b in back] * 2)
    operands = [view(qf)] + [view(kb)] * (n_prev + 1) + [view(vb)] * (n_prev + 1)
    if sink is not None:
        in_specs.append(pl.BlockSpec(memory_space=pltpu.SMEM))
        operands.append(sink.astype(jnp.float32))
    o, lse = pl.pallas_call(
        functools.partial(_band_body, tq=tq, n_prev=n_prev, span=span, r=r, has_sink=sink is not None),
        grid=(n, dil, length // tq),
        in_specs=in_specs,
        out_specs=[pl.BlockSpec((1, tq, qw), cur), pl.BlockSpec((1, tq, LANE), cur)],
        out_shape=[jax.ShapeDtypeStruct((n, length, dil * qw), jnp.float32),
                   jax.ShapeDtypeStruct((n, length, dil * LANE), jnp.float32)],
        compiler_params=pltpu.CompilerParams(
            dimension_semantics=("arbitrary", "arbitrary", "arbitrary"), vmem_limit_bytes=VMEM_LIMIT),
        name="band_attn",
    )(*operands)
    return o.reshape(n, t, g, r, dh), lse.reshape(n, t, LANE)[:, :, :N_HEADS].reshape(n, t, g, r)


def nsa_prompt_p(z, w_ck, w_cv):
    n, s, _ = z.shape
    pos = jnp.arange(s)
    q, kv, gates = _nsa_project(z, pos)
    ck, cv = nsa_compress_rows(kv.reshape(n, s, -1), w_ck, w_cv)
    o_cmp, blk = nsa_cmp_select(q, ck, cv)
    o_slc = masked_flash(q, kv[:, :, 2], kv[:, :, 3], jnp.transpose(blk, (0, 2, 3, 1)), NSA_SEL_BLOCK)
    o_win, _ = band_attn(q, kv[:, :, 4], kv[:, :, 5], 1, NSA_WINDOW)
    return _nsa_gate(gates, o_cmp, o_slc, o_win), kv[:, :, 0:4], _tail(kv[:, :, 4:6], NSA_WINDOW)


def dsa_prompt_p(z):
    n, s, _ = z.shape
    pos = jnp.arange(s)
    q, kv, q_idx, k_idx, w_idx = _dsa_project(z, pos)
    top = min(DSA_TOPK, s // 4)
    bias = dsa_select_bias(jnp.transpose(q_idx, (0, 2, 1, 3)).astype(MXU_DTYPE),
                           k_idx.astype(MXU_DTYPE), w_idx, top, FLASH_ROWS // q.shape[3])
    o = masked_flash(q, kv[:, :, 0], kv[:, :, 1], bias_tiles=bias)
    return o.reshape(n, s, Q_DIM), kv, k_idx


def dil_prompt_p(z):
    n, s, _ = z.shape
    q, kv = _dil_project(z, jnp.arange(s))
    outs, lses, bufs = [], [], []
    for gi, (win, dil) in enumerate(DIL_GROUPS):
        o, lse = band_attn(q, kv[:, :, gi, 0], kv[:, :, gi, 1], dil, win // dil + 1)
        outs.append(o)
        lses.append(lse)
        bufs.append(_tail(kv[:, :, gi], win))
    return _dil_mix(outs, lses).astype(z.dtype).reshape(n, s, Q_DIM), bufs


def swa_prompt_p(z, sink):
    n, s, _ = z.shape
    q, kv = _swa_project(z, jnp.arange(s))
    o, _ = band_attn(q, kv[:, :, 0], kv[:, :, 1], 1, SWA_WINDOW, sink)
    return o.reshape(n, s, Q_DIM), _tail(kv, SWA_WINDOW)


def _flash_update(s, pv, m_ref, l_ref, acc_ref):
    m_old = m_ref[...]
    m_new = jnp.maximum(m_old, jnp.max(s, axis=1, keepdims=True))
    p = jnp.exp(s - m_new)
    alpha = jnp.exp(m_old - m_new)
    l_ref[...] = alpha * l_ref[...] + jnp.sum(p, axis=1, keepdims=True)
    acc_ref[...] = alpha * acc_ref[...] + pv(p.astype(MXU_DTYPE))
    m_ref[...] = m_new


def _paged_attn_body(pt_ref, q_ref, bn_ref, kn_ref, vn_ref, b_ref, *rest, ppc):
    pages = rest[:ppc]
    o_ref, m_ref, l_ref, acc_ref = rest[ppc:]
    lw = q_ref.shape[2]
    c = pl.program_id(1)
    q = q_ref[0]

    @pl.when(c == 0)
    def _():
        m_ref[...] = jnp.full(m_ref.shape, NEG_INF, jnp.float32)
        l_ref[...] = jnp.zeros(l_ref.shape, jnp.float32)
        acc_ref[...] = jnp.zeros(acc_ref.shape, jnp.float32)
        s = lax.dot_general(q, kn_ref[0], _AT_BT, preferred_element_type=jnp.float32) + bn_ref[0].astype(jnp.float32)
        _flash_update(s, lambda p: jnp.dot(p, vn_ref[0], preferred_element_type=jnp.float32), m_ref, l_ref, acc_ref)

    parts = [jnp.dot(q, pages[i][0:lw, :].astype(MXU_DTYPE), preferred_element_type=jnp.float32) for i in range(ppc)]
    s = jnp.concatenate(parts, axis=1) + b_ref[0].astype(jnp.float32)

    def pv(p):
        acc = lax.dot_general(p[:, 0:PAGE_SIZE], pages[0][lw:2 * lw, :].astype(MXU_DTYPE), _AT_BT,
                              preferred_element_type=jnp.float32)
        for i in range(1, ppc):
            acc = acc + lax.dot_general(p[:, i * PAGE_SIZE:(i + 1) * PAGE_SIZE],
                                        pages[i][lw:2 * lw, :].astype(MXU_DTYPE), _AT_BT,
                                        preferred_element_type=jnp.float32)
        return acc

    _flash_update(s, pv, m_ref, l_ref, acc_ref)

    @pl.when(c == pl.num_programs(1) - 1)
    def _():
        o_ref[0] = acc_ref[...] / jnp.maximum(l_ref[...], 1e-30)


def paged_attn(q2, pool, layer, page_table, kv_blk, bias, k_new, v_new, bias_new):
    n, rows, lw = q2.shape
    n_pages = page_table.shape[1]
    ppc = min(PAGES_PER_CHUNK, n_pages)
    n_chunks = n_pages // ppc

    def page_spec(i):
        return pl.BlockSpec((None, None, 2 * lw, PAGE_SIZE),
                            lambda a, c, pt: (layer, pt[a * n_pages + c * ppc + i], kv_blk, 0))

    per_seq = lambda a, c, pt: (a, 0, 0)
    grid_spec = pltpu.PrefetchScalarGridSpec(
        num_scalar_prefetch=1,
        grid=(n, n_chunks),
        in_specs=[pl.BlockSpec((1, rows, lw), per_seq),
                  pl.BlockSpec((1, rows, PAGE_SIZE), per_seq),
                  pl.BlockSpec((1, PAGE_SIZE, lw), per_seq),
                  pl.BlockSpec((1, PAGE_SIZE, lw), per_seq),
                  pl.BlockSpec((1, rows, ppc * PAGE_SIZE), lambda a, c, pt: (a, 0, c))]
                 + [page_spec(i) for i in range(ppc)],
        out_specs=pl.BlockSpec((1, rows, lw), per_seq),
        scratch_shapes=[pltpu.VMEM((rows, 1), jnp.float32),
                        pltpu.VMEM((rows, 1), jnp.float32),
                        pltpu.VMEM((rows, lw), jnp.float32)],
    )
    return pl.pallas_call(
        functools.partial(_paged_attn_body, ppc=ppc),
        grid_spec=grid_spec,
        out_shape=jax.ShapeDtypeStruct((n, rows, lw), jnp.float32),
        compiler_params=pltpu.CompilerParams(
            dimension_semantics=("arbitrary", "arbitrary"), vmem_limit_bytes=VMEM_LIMIT),
        name="paged_attn",
    )(page_table.reshape(-1), q2, bias_new, k_new, v_new, bias, *([pool] * ppc))


def _feature_major_pages(pool):
    nd = pool.ndim
    t = jnp.transpose(pool, (0, 1) + tuple(range(3, nd)) + (2,))
    return t.reshape(pool.shape[0], pool.shape[1], -1, pool.shape[2])


def _group_lane_queries(q):
    n, t, g, r, dh = q.shape
    qs = jnp.transpose(q * (dh ** -0.5), (0, 2, 1, 3, 4))
    onehot = jnp.eye(g, dtype=qs.dtype)
    q2 = qs[:, :, :, :, None, :] * onehot[None, :, None, None, :, None]
    return q2.reshape(n, g * t * r, g * dh).astype(MXU_DTYPE)


def _ungroup_lanes(o, t, g, r, dh):
    n = o.shape[0]
    o6 = o.reshape(n, g, t, r, g, dh)
    return jnp.stack([o6[:, gi, :, :, gi] for gi in range(g)], axis=2)


def _rows_bias(allow, g, r):
    n, t, ga, k = allow.shape
    a = jnp.broadcast_to(jnp.transpose(allow, (0, 2, 1, 3))[:, :, :, None, :], (n, ga, t, r, k))
    if ga != g:
        a = jnp.broadcast_to(a, (n, g, t, r, k))
    return jnp.where(a, 0.0, NEG_INF).astype(BIAS_DTYPE).reshape(n, g * t * r, k)


def _new_rows_page(x, lw):
    n, t, _ = x.shape
    return jnp.pad(x, ((0, 0), (0, PAGE_SIZE - t), (0, 0))).astype(MXU_DTYPE)


def nsa_sample_p(z, pool, layer, win_buf, page_table, w_ck, w_cv, past):
    n, t, _ = z.shape
    g, r = NSA_KV_HEADS, N_HEADS // NSA_KV_HEADS
    pos = past + jnp.arange(t)
    q, kv, gates = _nsa_project(z, pos)
    ck_p, cv_p = nsa_compress_paged(pool, layer, page_table, w_ck, w_cv)
    ck_n, cv_n = _nsa_compress(kv[:, :, 0:2], w_ck, w_cv)
    o_cmp, p_cmp = _nsa_cmp_attend(q, jnp.concatenate([ck_p, ck_n], 1),
                                   jnp.concatenate([cv_p, cv_n], 1), pos)
    n_slc = -(-(past + t) // NSA_SEL_BLOCK)
    blk = _nsa_select_mask(p_cmp, pos, n_slc)
    n_past_blk = past // NSA_SEL_BLOCK
    allow_past = jnp.repeat(blk[..., :n_past_blk], NSA_SEL_BLOCK, axis=-1)
    j = jnp.arange(PAGE_SIZE)
    new_blk = (past + j) // NSA_SEL_BLOCK
    allow_new = (jnp.take(blk, jnp.minimum(new_blk, n_slc - 1), axis=-1)
                 & (j[None, :] <= jnp.arange(t)[:, None])[None, :, None, :] & (j < t))
    lw = g * HEAD_DIM
    pool_flat = _feature_major_pages(pool)
    o = paged_attn(_group_lane_queries(q), pool_flat, layer, page_table, 1,
                   _rows_bias(allow_past, g, r),
                   _new_rows_page(kv[:, :, 2].reshape(n, t, lw), lw),
                   _new_rows_page(kv[:, :, 3].reshape(n, t, lw), lw),
                   _rows_bias(allow_new, g, r))
    o_slc = _ungroup_lanes(o, t, g, r, HEAD_DIM)
    keys = jnp.concatenate([win_buf, kv[:, :, 4:6]], axis=1)
    kp = past - win_buf.shape[1] + jnp.arange(keys.shape[1])
    diff = pos[:, None] - kp[None, :]
    o_win, _ = _attend_dense(q, keys[:, :, 0], keys[:, :, 1], ((diff >= 0) & (diff < NSA_WINDOW))[None])
    return _nsa_gate(gates, o_cmp, o_slc, o_win), kv[:, :, 0:4], _tail(keys, NSA_WINDOW)


def _dsa_sample_select_body(pt_ref, qx_ref, w_ref, kn_ref, *rest, n_pages, t_new, top):
    pages = rest[:n_pages]
    o_ref, key_ref = rest[n_pages:]
    qx = qx_ref[0]
    w = w_ref[0]
    row = lax.broadcasted_iota(jnp.int32, (8, 1), 0)
    col = lax.broadcasted_iota(jnp.int32, (1, PAGE_SIZE), 1)
    new_ok = (col <= row) & (col < t_new)

    for i in range(n_pages + 1):
        if i == n_pages:
            d = lax.dot_general(qx, kn_ref[0], _AT_BT, preferred_element_type=jnp.float32)
        else:
            d = jnp.dot(qx, pages[i][...].astype(MXU_DTYPE), preferred_element_type=jnp.float32)
        d = jnp.maximum(d, 0.0).reshape(IDX_HEADS, 8, PAGE_SIZE)
        sc = w[:, 0:1] * d[0]
        for hh in range(1, IDX_HEADS):
            sc = sc + w[:, hh:hh + 1] * d[hh]
        if i == n_pages:
            sc = jnp.where(new_ok, sc, NEG_INF)
        bits = pltpu.bitcast(sc, jnp.int32)
        key = jnp.where(bits >= 0, bits, bits ^ 0x7FFFFFFF)
        key_ref[i] = jnp.where(key == -1, 0, key)

    def count(pred):
        hit = jnp.where(pred(key_ref[...]), 1.0, 0.0)
        return jnp.sum(jnp.sum(hit, axis=0), axis=1, keepdims=True)

    c0 = count(lambda kk: kk >= 0)
    tau = jnp.where(c0 >= top, 0, INT32_MIN).astype(jnp.int32)

    def bit_step(i, tau):
        cand = tau | jnp.left_shift(jnp.int32(1), 30 - i)
        return jnp.where(count(lambda kk: kk >= cand) >= top, cand, tau)

    tau = lax.fori_loop(0, 31, bit_step, tau)
    need = top - count(lambda kk: kk > tau)
    ri = lax.broadcasted_iota(jnp.int32, (PAGE_SIZE, PAGE_SIZE), 0)
    ci = lax.broadcasted_iota(jnp.int32, (PAGE_SIZE, PAGE_SIZE), 1)
    tri = jnp.where(ri <= ci, 1.0, 0.0).astype(MXU_DTYPE)
    seen = jnp.zeros((8, 1), jnp.float32)
    for i in range(n_pages + 1):
        kk = key_ref[i]
        tie = kk == tau
        tie_f = jnp.where(tie, 1.0, 0.0)
        rank = seen + jnp.dot(tie_f.astype(MXU_DTYPE), tri, preferred_element_type=jnp.float32)
        sel = (kk > tau) | (tie & (rank <= need))
        if i == n_pages:
            sel = sel & new_ok
        o_ref[0, :, i * PAGE_SIZE:(i + 1) * PAGE_SIZE] = jnp.where(sel, 0.0, NEG_INF)
        seen = seen + jnp.sum(tie_f, axis=1, keepdims=True)


def dsa_sample_select(q_idx, k_idx_new, w_idx, pool_idx, layer, page_table, top):
    n, t, hh, di = q_idx.shape
    n_pages = page_table.shape[1]
    qx = jnp.pad(jnp.transpose(q_idx, (0, 2, 1, 3)), ((0, 0), (0, 0), (0, 8 - t), (0, 0)))
    qx = qx.reshape(n, hh * 8, di).astype(MXU_DTYPE)
    w8 = jnp.pad(w_idx, ((0, 0), (0, 8 - t), (0, 0)))
    kn = _new_rows_page(k_idx_new, di)
    per_seq = lambda a, pt: (a, 0, 0)
    total = (n_pages + 1) * PAGE_SIZE
    grid_spec = pltpu.PrefetchScalarGridSpec(
        num_scalar_prefetch=1,
        grid=(n,),
        in_specs=[pl.BlockSpec((1, hh * 8, di), per_seq),
                  pl.BlockSpec((1, 8, hh), per_seq),
                  pl.BlockSpec((1, PAGE_SIZE, di), per_seq)]
                 + [pl.BlockSpec((None, None, di, PAGE_SIZE),
                                 functools.partial(lambda a, pt, i: (layer, pt[a * n_pages + i], 0, 0), i=i))
                    for i in range(n_pages)],
        out_specs=pl.BlockSpec((1, 8, total), per_seq),
        scratch_shapes=[pltpu.VMEM((n_pages + 1, 8, PAGE_SIZE), jnp.int32)],
    )
    return pl.pallas_call(
        functools.partial(_dsa_sample_select_body, n_pages=n_pages, t_new=t, top=top),
        grid_spec=grid_spec,
        out_shape=jax.ShapeDtypeStruct((n, 8, total), jnp.float32),
        compiler_params=pltpu.CompilerParams(
            dimension_semantics=("arbitrary",), vmem_limit_bytes=VMEM_LIMIT),
        name="dsa_sample_select",
    )(page_table.reshape(-1), qx, w8, kn, *([_feature_major_pages(pool_idx)] * n_pages))


def dsa_sample_p(z, pool_kv, pool_idx, layer, page_table, past):
    n, t, _ = z.shape
    g, r = DSA_KV_HEADS, N_HEADS // DSA_KV_HEADS
    pos = past + jnp.arange(t)
    q, kv, q_idx, k_idx, w_idx = _dsa_project(z, pos)
    total = past + t
    bias8 = dsa_sample_select(q_idx, k_idx, w_idx, pool_idx, layer, page_table, min(DSA_TOPK, total // 4))
    allow = (bias8[:, :t] == 0.0)[:, :, None, :]
    lw = g * HEAD_DIM
    pool_flat = _feature_major_pages(pool_kv)
    o = paged_attn(_group_lane_queries(q), pool_flat, layer, page_table, 0,
                   _rows_bias(allow[..., :past], g, r),
                   _new_rows_page(kv[:, :, 0].reshape(n, t, lw), lw),
                   _new_rows_page(kv[:, :, 1].reshape(n, t, lw), lw),
                   _rows_bias(allow[..., past:], g, r))
    return _ungroup_lanes(o, t, g, r, HEAD_DIM).reshape(n, t, Q_DIM), kv, k_idx


def dil_sample(z, bufs, past):
    n, t, _ = z.shape
    pos = past + jnp.arange(t)
    q, kv = _dil_project(z, pos)
    outs, lses, new_bufs = [], [], []
    for gi, ((win, dil), buf) in enumerate(zip(DIL_GROUPS, bufs)):
        keys = jnp.concatenate([buf, kv[:, :, gi]], axis=1)
        base = past - buf.shape[1]
        local = pos[:, None] - (jnp.arange(win // dil + 1) * dil)[None, :] - base
        lc = jnp.maximum(local, 0)
        o, lse = _attend_gathered(q, keys[:, lc, 0], keys[:, lc, 1], (local >= 0)[None, :, :, None])
        outs.append(o)
        lses.append(lse)
        new_bufs.append(_tail(keys, win))
    return _dil_mix(outs, lses).astype(z.dtype).reshape(n, t, Q_DIM), new_bufs


def swa_sample(z, buf, sink, past):
    n, t, _ = z.shape
    pos = past + jnp.arange(t)
    q, kv = _swa_project(z, pos)
    keys = jnp.concatenate([buf, kv], axis=1)
    kp = past - buf.shape[1] + jnp.arange(keys.shape[1])
    diff = pos[:, None] - kp[None, :]
    o, _ = _attend_dense(q, keys[:, :, 0], keys[:, :, 1], ((diff >= 0) & (diff < SWA_WINDOW))[None],
                         sink.reshape(SWA_KV_HEADS, -1))
    return o.reshape(n, t, Q_DIM), _tail(keys, SWA_WINDOW)


def _pad_cols(w):
    n_out = w.shape[1]
    return jnp.pad(w, ((0, 0), (0, -n_out % LANE))).astype(MXU_DTYPE)


def _project(h, g, w_in):
    n, t, _ = h.shape
    z = norm_proj(h.reshape(n * t, D_MODEL), g, _pad_cols(w_in))
    return z[:, :w_in.shape[1]].reshape(n, t, w_in.shape[1])


def _add_out_proj(h, o, w_out):
    n, t, _ = h.shape
    return proj_residual(o.reshape(n * t, Q_DIM), w_out.astype(MXU_DTYPE), h.reshape(n * t, D_MODEL)).reshape(h.shape)


def _add_ffn(h, g, w_in_b, w_out_b):
    return ffn_residual(h.reshape(-1, D_MODEL), g, w_in_b, w_out_b).reshape(h.shape)


def kernel(x_prompt, x_sample, cache_nsa_kv, cache_nsa_win, cache_dil1, cache_dil2, cache_dil3,
           cache_dsa_kv, cache_dsa_idx, cache_swa, page_table,
           norm_mix, norm_ffn, norm_final, ffn_in, ffn_out,
           nsa_w_in, nsa_w_cmp_k, nsa_w_cmp_v, nsa_w_out,
           dil_w_in, dil_w_out, dsa_w_in, dsa_w_out,
           swa_w_in, swa_sink, swa_w_out):
    past = page_table.shape[1] * PAGE_SIZE
    hp, hs = x_prompt, x_sample
    ffn_in_b = ffn_in.astype(MXU_DTYPE)
    ffn_out_b = ffn_out.astype(MXU_DTYPE)
    st = {name: [] for name in ("nsa_kv_p", "nsa_kv_s", "nsa_win_p", "nsa_win_s",
                                "dil1_p", "dil1_s", "dil2_p", "dil2_s", "dil3_p", "dil3_s",
                                "dsa_kv_p", "dsa_kv_s", "dsa_idx_p", "dsa_idx_s", "swa_p", "swa_s")}
    for i in range(DEPTH):
        kind, j = i % N_MIXERS, i // N_MIXERS
        w_in, w_out = ((nsa_w_in, nsa_w_out), (dil_w_in, dil_w_out), (dsa_w_in, dsa_w_out), (swa_w_in, swa_w_out))[kind]
        zp, zs = _project(hp, norm_mix[i], w_in[j]), _project(hs, norm_mix[i], w_in[j])
        if kind == 0:
            op, kv_p, win_p = nsa_prompt_p(zp, nsa_w_cmp_k[j], nsa_w_cmp_v[j])
            os_, kv_s, win_s = nsa_sample_p(zs, cache_nsa_kv, j, cache_nsa_win[j], page_table,
                                            nsa_w_cmp_k[j], nsa_w_cmp_v[j], past)
            st["nsa_kv_p"].append(kv_p)
            st["nsa_kv_s"].append(kv_s)
            st["nsa_win_p"].append(win_p)
            st["nsa_win_s"].append(win_s)
        elif kind == 1:
            op, bufs_p = dil_prompt_p(zp)
            os_, bufs_s = dil_sample(zs, [cache_dil1[j], cache_dil2[j], cache_dil3[j]], past)
            for gi in range(len(DIL_GROUPS)):
                st["dil%d_p" % (gi + 1)].append(bufs_p[gi])
                st["dil%d_s" % (gi + 1)].append(bufs_s[gi])
        elif kind == 2:
            op, kv_p, idx_p = dsa_prompt_p(zp)
            os_, kv_s, idx_s = dsa_sample_p(zs, cache_dsa_kv, cache_dsa_idx, j, page_table, past)
            st["dsa_kv_p"].append(kv_p)
            st["dsa_kv_s"].append(kv_s)
            st["dsa_idx_p"].append(idx_p)
            st["dsa_idx_s"].append(idx_s)
        else:
            op, buf_p = swa_prompt_p(zp, swa_sink[j])
            os_, buf_s = swa_sample(zs, cache_swa[j], swa_sink[j], past)
            st["swa_p"].append(buf_p)
            st["swa_s"].append(buf_s)
        hp = _add_out_proj(hp, op, w_out[j])
        hs = _add_out_proj(hs, os_, w_out[j])
        hp = _add_ffn(hp, norm_ffn[i], ffn_in_b[i], ffn_out_b[i])
        hs = _add_ffn(hs, norm_ffn[i], ffn_in_b[i], ffn_out_b[i])
    y_prompt = rms_norm(hp, norm_final)
    y_sample = rms_norm(hs, norm_final)
    return (y_prompt, y_sample,
            jnp.stack(st["nsa_kv_p"]), jnp.stack(st["nsa_kv_s"]),
            jnp.stack(st["nsa_win_p"]), jnp.stack(st["nsa_win_s"]),
            jnp.stack(st["dil1_p"]), jnp.stack(st["dil1_s"]),
            jnp.stack(st["dil2_p"]), jnp.stack(st["dil2_s"]),
            jnp.stack(st["dil3_p"]), jnp.stack(st["dil3_s"]),
            jnp.stack(st["dsa_kv_p"]), jnp.stack(st["dsa_kv_s"]),
            jnp.stack(st["dsa_idx_p"]), jnp.stack(st["dsa_idx_s"]),
            jnp.stack(st["swa_p"]), jnp.stack(st["swa_s"]))
```

```python
import functools

import jax, jax.numpy as jnp
from jax import lax
import numpy as np
from jax.experimental import pallas as pl
from jax.experimental.pallas import tpu as pltpu

D_MODEL = 1024
BATCH = 2
SEQ = 8192
DEPTH = 4
DEC_BATCH = 128
DEC_SEQ = 4
PAST_LEN = 8192
PAGE_SIZE = 128

HEAD_DIM = 64
N_HEADS = D_MODEL // HEAD_DIM
Q_DIM = N_HEADS * HEAD_DIM
ROPE_THETA = 10000.0
NORM_EPS = 1e-6
N_MIXERS = 4
D_FF = -(-(8 * D_MODEL) // (3 * 256)) * 256
NEG_INF = -1e30

NSA_KV_HEADS = 2
NSA_CMP_BLOCK = 32
NSA_SEL_BLOCK = 64
NSA_N_SEL = 16
NSA_WINDOW = 512
NSA_FORCE_BONUS = 1e4
NSA_IN = Q_DIM + 6 * NSA_KV_HEADS * HEAD_DIM + 3 * N_HEADS

DIL_KV_HEADS = 4
DIL_GROUPS = ((128, 1), (512, 4), (2048, 16))
DIL_IN = Q_DIM + len(DIL_GROUPS) * 2 * DIL_KV_HEADS * HEAD_DIM

DSA_KV_HEADS = 4
IDX_HEADS = 8
IDX_DIM = 64
DSA_TOPK = 256
IDX_SCALE = (IDX_DIM * IDX_HEADS) ** -0.5
DSA_IN = Q_DIM + 2 * DSA_KV_HEADS * HEAD_DIM + IDX_HEADS * IDX_DIM + IDX_DIM + IDX_HEADS

SWA_KV_HEADS = 2
SWA_WINDOW = 128
SWA_IN = Q_DIM + 2 * SWA_KV_HEADS * HEAD_DIM


def rms_norm(x, g):
    xf = x.astype(jnp.float32)
    y = xf * lax.rsqrt(jnp.mean(xf * xf, axis=-1, keepdims=True) + NORM_EPS)
    return (y * g.astype(jnp.float32)).astype(x.dtype)


def rope(x, pos):
    half = x.shape[-1] // 2
    inv = ROPE_THETA ** (-jnp.arange(half, dtype=jnp.float32) / half)
    ang = pos.astype(jnp.float32)[:, None] * inv[None, :]
    shape = (pos.shape[0],) + (1,) * (x.ndim - 3) + (half,)
    cos, sin = jnp.cos(ang).reshape(shape), jnp.sin(ang).reshape(shape)
    xf = x.astype(jnp.float32)
    x1, x2 = xf[..., :half], xf[..., half:]
    return jnp.concatenate([x1 * cos - x2 * sin, x2 * cos + x1 * sin], axis=-1).astype(x.dtype)


def _tail(rows, window):
    n = rows.shape[1]
    return rows[:, n - min(window, n):]


def _masked_softmax(s, mask, sink=None):
    s = jnp.where(mask, s, NEG_INF)
    m = jnp.max(s, axis=-1, keepdims=True)
    if sink is not None:
        m = jnp.maximum(m, sink)
    e = jnp.where(mask, jnp.exp(s - m), 0.0)
    den = jnp.sum(e, axis=-1, keepdims=True)
    if sink is not None:
        den = den + jnp.exp(sink - m)
    den = jnp.maximum(den, 1e-30)
    return e / den, (m + jnp.log(den))[..., 0]


def _attend_dense(q, k, v, mask, sink=None):
    s = jnp.einsum('nqgrd,nkgd->ngrqk', q, k, preferred_element_type=jnp.float32) * (q.shape[-1] ** -0.5)
    sk = None if sink is None else sink.astype(jnp.float32)[None, :, :, None, None]
    p, lse = _masked_softmax(s, mask[:, None, None], sk)
    o = jnp.einsum('ngrqk,nkgd->nqgrd', p.astype(v.dtype), v)
    return o, jnp.transpose(lse, (0, 3, 1, 2))


def _attend_gathered(q, k, v, mask):
    s = jnp.einsum('nqgrd,nqkgd->nqgrk', q, k, preferred_element_type=jnp.float32) * (q.shape[-1] ** -0.5)
    p, lse = _masked_softmax(s, jnp.swapaxes(mask, 2, 3)[:, :, :, None, :])
    o = jnp.einsum('nqgrk,nqkgd->nqgrd', p.astype(v.dtype), v)
    return o, lse


def _nsa_project(z, pos):
    n, t, _ = z.shape
    g, r = NSA_KV_HEADS, N_HEADS // NSA_KV_HEADS
    kv_end = Q_DIM + 6 * g * HEAD_DIM
    q = rope(z[..., :Q_DIM].reshape(n, t, g, r, HEAD_DIM), pos)
    kv = z[..., Q_DIM:kv_end].reshape(n, t, 3, 2, g, HEAD_DIM)
    kv = jnp.stack([rope(kv[:, :, :, 0], pos), kv[:, :, :, 1]], axis=3).reshape(n, t, 6, g, HEAD_DIM)
    gates = jax.nn.sigmoid(z[..., kv_end:].astype(jnp.float32)).reshape(n, t, g, r, 3)
    return q, kv, gates


def _nsa_compress(rows, w_ck, w_cv):
    n, length, _, g, dh = rows.shape
    nb = length // NSA_CMP_BLOCK
    r = rows[:, :nb * NSA_CMP_BLOCK].reshape(n, nb, NSA_CMP_BLOCK, 2, g, dh)
    ck = jnp.einsum('nbcgd,gcde->nbge', r[:, :, :, 0], w_ck)
    cv = jnp.einsum('nbcgd,gcde->nbge', r[:, :, :, 1], w_cv)
    return ck, cv


def _nsa_cmp_attend(q, ck, cv, qpos):
    blk_end = (jnp.arange(ck.shape[1]) + 1) * NSA_CMP_BLOCK - 1
    mask = (blk_end[None, :] <= qpos[:, None])[None, :, None, None, :]
    s = jnp.einsum('nqgrd,ncgd->nqgrc', q, ck, preferred_element_type=jnp.float32) * (q.shape[-1] ** -0.5)
    p, _ = _masked_softmax(s, mask)
    o = jnp.einsum('nqgrc,ncgd->nqgrd', p.astype(cv.dtype), cv)
    return o, p


def _nsa_gate(gates, o_cmp, o_slc, o_win):
    f = jnp.float32
    o = (gates[..., 0:1] * o_cmp.astype(f) + gates[..., 1:2] * o_slc.astype(f)
         + gates[..., 2:3] * o_win.astype(f))
    n, t = o.shape[:2]
    return o.reshape(n, t, Q_DIM)


def _dil_project(z, pos):
    n, t, _ = z.shape
    g = DIL_KV_HEADS
    q = rope(z[..., :Q_DIM].reshape(n, t, g, N_HEADS // g, HEAD_DIM), pos)
    kv = z[..., Q_DIM:].reshape(n, t, len(DIL_GROUPS), 2, g, HEAD_DIM)
    return q, jnp.stack([rope(kv[:, :, :, 0], pos), kv[:, :, :, 1]], axis=3)


def _dil_mix(outs, lses):
    w = jax.nn.softmax(jnp.stack(lses), axis=0)
    return jnp.sum(w[..., None] * jnp.stack(outs).astype(jnp.float32), axis=0)


def _dsa_project(z, pos):
    n, t, _ = z.shape
    g = DSA_KV_HEADS
    o1 = Q_DIM
    o2 = o1 + 2 * g * HEAD_DIM
    o3 = o2 + IDX_HEADS * IDX_DIM
    o4 = o3 + IDX_DIM
    q = rope(z[..., :o1].reshape(n, t, g, N_HEADS // g, HEAD_DIM), pos)
    kv = z[..., o1:o2].reshape(n, t, 2, g, HEAD_DIM)
    kv = jnp.stack([rope(kv[:, :, 0], pos), kv[:, :, 1]], axis=2)
    q_idx = rope(z[..., o2:o3].reshape(n, t, IDX_HEADS, IDX_DIM), pos)
    k_idx = rope(z[..., o3:o4], pos)
    w_idx = z[..., o4:].astype(jnp.float32) * IDX_SCALE
    return q, kv, q_idx, k_idx, w_idx


def _swa_project(z, pos):
    n, t, _ = z.shape
    g = SWA_KV_HEADS
    q = rope(z[..., :Q_DIM].reshape(n, t, g, N_HEADS // g, HEAD_DIM), pos)
    kv = z[..., Q_DIM:].reshape(n, t, 2, g, HEAD_DIM)
    return q, jnp.stack([rope(kv[:, :, 0], pos), kv[:, :, 1]], axis=2)


MXU_DTYPE = jnp.bfloat16
BIAS_DTYPE = jnp.bfloat16
ROW_TILE = 512
FF_CHUNK = 256
VMEM_LIMIT = 56 * 1024 * 1024
LANE = 128
PAGES_PER_CHUNK = 32
Q_TILE = 128
K_TILE = 512
FLASH_ROWS = 2048
TOPK_ROWS = 256
REMOVED = -3e38
INT32_MIN = -2 ** 31
_AT_BT = (((1,), (1,)), ((), ()))


def _ffn_body(x_ref, g_ref, win_ref, wout_ref, o_ref):
    x = x_ref[...]
    xn = x * lax.rsqrt(jnp.mean(x * x, axis=-1, keepdims=True) + NORM_EPS) * g_ref[...]
    xb = xn.astype(MXU_DTYPE)
    acc = x
    for j in range(D_FF // FF_CHUNK):
        lo = j * FF_CHUNK
        gate = jnp.dot(xb, win_ref[:, lo:lo + FF_CHUNK], preferred_element_type=jnp.float32)
        up = jnp.dot(xb, win_ref[:, D_FF + lo:D_FF + lo + FF_CHUNK], preferred_element_type=jnp.float32)
        a = (gate * jax.nn.sigmoid(gate) * up).astype(MXU_DTYPE)
        acc = acc + jnp.dot(a, wout_ref[lo:lo + FF_CHUNK, :], preferred_element_type=jnp.float32)
    o_ref[...] = acc


def ffn_residual(x, g, w_in_bf16, w_out_bf16):
    rows = x.shape[0]
    return pl.pallas_call(
        _ffn_body,
        grid=(rows // ROW_TILE,),
        in_specs=[
            pl.BlockSpec((ROW_TILE, D_MODEL), lambda i: (i, 0)),
            pl.BlockSpec((1, D_MODEL), lambda i: (0, 0)),
            pl.BlockSpec((D_MODEL, 2 * D_FF), lambda i: (0, 0)),
            pl.BlockSpec((D_FF, D_MODEL), lambda i: (0, 0)),
        ],
        out_specs=pl.BlockSpec((ROW_TILE, D_MODEL), lambda i: (i, 0)),
        out_shape=jax.ShapeDtypeStruct((rows, D_MODEL), jnp.float32),
        compiler_params=pltpu.CompilerParams(
            dimension_semantics=("arbitrary",), vmem_limit_bytes=VMEM_LIMIT),
        name="ffn_residual",
    )(x, g.reshape(1, D_MODEL), w_in_bf16, w_out_bf16)


def _norm_proj_body(x_ref, g_ref, w_ref, o_ref):
    x = x_ref[...]
    xn = x * lax.rsqrt(jnp.mean(x * x, axis=-1, keepdims=True) + NORM_EPS) * g_ref[...]
    o_ref[...] = jnp.dot(xn.astype(MXU_DTYPE), w_ref[...], preferred_element_type=jnp.float32)


def norm_proj(x, g, w_b):
    rows, n_out = x.shape[0], w_b.shape[1]
    return pl.pallas_call(
        _norm_proj_body,
        grid=(rows // ROW_TILE,),
        in_specs=[pl.BlockSpec((ROW_TILE, D_MODEL), lambda i: (i, 0)),
                  pl.BlockSpec((1, D_MODEL), lambda i: (0, 0)),
                  pl.BlockSpec((D_MODEL, n_out), lambda i: (0, 0))],
        out_specs=pl.BlockSpec((ROW_TILE, n_out), lambda i: (i, 0)),
        out_shape=jax.ShapeDtypeStruct((rows, n_out), jnp.float32),
        compiler_params=pltpu.CompilerParams(dimension_semantics=("arbitrary",), vmem_limit_bytes=VMEM_LIMIT),
        name="norm_proj",
    )(x, g.reshape(1, D_MODEL), w_b)


def _proj_residual_body(a_ref, w_ref, r_ref, o_ref):
    o_ref[...] = r_ref[...] + jnp.dot(a_ref[...].astype(MXU_DTYPE), w_ref[...], preferred_element_type=jnp.float32)


def proj_residual(a, w_b, res):
    rows = a.shape[0]
    return pl.pallas_call(
        _proj_residual_body,
        grid=(rows // ROW_TILE,),
        in_specs=[pl.BlockSpec((ROW_TILE, Q_DIM), lambda i: (i, 0)),
                  pl.BlockSpec((Q_DIM, D_MODEL), lambda i: (0, 0)),
                  pl.BlockSpec((ROW_TILE, D_MODEL), lambda i: (i, 0))],
        out_specs=pl.BlockSpec((ROW_TILE, D_MODEL), lambda i: (i, 0)),
        out_shape=jax.ShapeDtypeStruct((rows, D_MODEL), jnp.float32),
        compiler_params=pltpu.CompilerParams(dimension_semantics=("arbitrary",), vmem_limit_bytes=VMEM_LIMIT),
        name="proj_residual",
    )(a, w_b, res)


def _causal_tiles(qi, tq, tk):
    return (qi * tq + tq + tk - 1) // tk


def _mflash_body(qt_ref, k_ref, vt_ref, bt_ref, o_ref, m_ref, l_ref, acc_ref, *, r, tq, tk, blk):
    qi = pl.program_id(2)
    qt = qt_ref[0, 0, 0]
    m_ref[...] = jnp.full(m_ref.shape, NEG_INF, jnp.float32)
    l_ref[...] = jnp.zeros(l_ref.shape, jnp.float32)
    acc_ref[...] = jnp.zeros(acc_ref.shape, jnp.float32)

    def step(kt, carry):
        ks = pl.multiple_of(kt * tk, tk)
        k = k_ref[0, 0, pl.ds(ks, tk), :]
        vt = vt_ref[0, 0, :, pl.ds(ks, tk)]
        if blk:
            per_tile = tk // blk
            rows = bt_ref[0, 0, pl.ds(pl.multiple_of(kt * per_tile, per_tile), per_tile), :]
            chosen = jnp.concatenate([jnp.broadcast_to(rows[b:b + 1, :], (blk, tq)) for b in range(per_tile)], axis=0)
            key_pos = ks + lax.broadcasted_iota(jnp.int32, (tk, 1), 0)
            q_pos = qi * tq + lax.broadcasted_iota(jnp.int32, (1, tq), 1)
            bias = jnp.where((chosen > 0.5) & (key_pos <= q_pos), 0.0, NEG_INF)
        else:
            bias = bt_ref[0, 0, kt, 0].astype(jnp.float32)
        s = jnp.dot(k, qt, preferred_element_type=jnp.float32) + jnp.concatenate([bias] * r, axis=1)
        m_old = m_ref[...]
        m_new = jnp.maximum(m_old, jnp.max(s, axis=0, keepdims=True))
        p = jnp.exp(s - m_new)
        alpha = jnp.exp(m_old - m_new)
        l_ref[...] = alpha * l_ref[...] + jnp.sum(p, axis=0, keepdims=True)
        acc_ref[...] = alpha * acc_ref[...] + jnp.dot(vt, p.astype(MXU_DTYPE), preferred_element_type=jnp.float32)
        m_ref[...] = m_new
        return carry

    lax.fori_loop(0, _causal_tiles(qi, tq, tk), step, 0)
    o_ref[0, 0, 0] = acc_ref[...] / jnp.maximum(l_ref[...], 1e-30)


def masked_flash(q, k, v, allow_t=None, blk=0, bias_tiles=None):
    n, t, g, r, dh = q.shape
    gb = (allow_t if bias_tiles is None else bias_tiles).shape[1]
    tq, tk = FLASH_ROWS // r, K_TILE
    nq, nkt = t // tq, t // tk
    qs = (q * (dh ** -0.5)).reshape(n, nq, tq, g, r, dh)
    qt = jnp.transpose(qs, (0, 3, 1, 5, 4, 2)).reshape(n, g, nq, dh, r * tq).astype(MXU_DTYPE)
    kb = jnp.transpose(k, (0, 2, 1, 3)).astype(MXU_DTYPE)
    vt = jnp.transpose(v, (0, 2, 3, 1)).astype(MXU_DTYPE)
    if bias_tiles is not None:
        bt = bias_tiles
        bmap = (lambda a, b, c: (a, b, 0, c, 0, 0)) if gb == g else (lambda a, b, c: (a, 0, 0, c, 0, 0))
        bspec = pl.BlockSpec((1, 1, nkt, 1, tk, tq), bmap)
    elif blk:
        bt = allow_t.astype(jnp.float32)
        bspec = pl.BlockSpec((1, 1, t // blk, tq), lambda a, b, c: (a, b, 0, c))
    else:
        bt = jnp.where(allow_t, 0.0, NEG_INF).astype(BIAS_DTYPE).reshape(n, gb, nkt, tk, nq, tq)
        bt = jnp.transpose(bt, (0, 1, 2, 4, 3, 5))
        bmap = (lambda a, b, c: (a, b, 0, c, 0, 0)) if gb == g else (lambda a, b, c: (a, 0, 0, c, 0, 0))
        bspec = pl.BlockSpec((1, 1, nkt, 1, tk, tq), bmap)
    o = pl.pallas_call(
        functools.partial(_mflash_body, r=r, tq=tq, tk=tk, blk=blk),
        grid=(n, g, nq),
        in_specs=[
            pl.BlockSpec((1, 1, 1, dh, r * tq), lambda a, b, c: (a, b, c, 0, 0)),
            pl.BlockSpec((1, 1, t, dh), lambda a, b, c: (a, b, 0, 0)),
            pl.BlockSpec((1, 1, dh, t), lambda a, b, c: (a, b, 0, 0)),
            bspec,
        ],
        out_specs=pl.BlockSpec((1, 1, 1, dh, r * tq), lambda a, b, c: (a, b, c, 0, 0)),
        out_shape=jax.ShapeDtypeStruct((n, g, nq, dh, r * tq), jnp.float32),
        scratch_shapes=[pltpu.VMEM((1, r * tq), jnp.float32),
                        pltpu.VMEM((1, r * tq), jnp.float32),
                        pltpu.VMEM((dh, r * tq), jnp.float32)],
        compiler_params=pltpu.CompilerParams(
            dimension_semantics=("arbitrary", "arbitrary", "arbitrary"), vmem_limit_bytes=VMEM_LIMIT),
        name="masked_flash",
    )(qt, kb, vt, bt)
    o = o.reshape(n, g, nq, dh, r, tq)
    return jnp.transpose(o, (0, 2, 5, 1, 4, 3)).reshape(n, t, g, r, dh)


def _dsa_select_body(qx_ref, kx_ref, w_ref, o_ref, key_ref, *, tq, tk, top):
    qi = pl.program_id(1)
    nkt = key_ref.shape[0]
    n_valid = _causal_tiles(qi, tq, tk)
    qx = qx_ref[0].reshape(IDX_HEADS * tq, IDX_DIM)
    w = w_ref[0]
    row = qi * tq + lax.broadcasted_iota(jnp.int32, (tq, 1), 0)

    def causal(kt):
        col = kt * tk + lax.broadcasted_iota(jnp.int32, (1, tk), 1)
        return col <= row

    def score_step(kt, carry):
        ks = pl.multiple_of(kt * tk, tk)
        d = lax.dot_general(qx, kx_ref[0, pl.ds(ks, tk), :], _AT_BT, preferred_element_type=jnp.float32)
        d = jnp.maximum(d, 0.0).reshape(IDX_HEADS, tq, tk)
        sc = w[:, 0:1] * d[0]
        for h in range(1, IDX_HEADS):
            sc = sc + w[:, h:h + 1] * d[h]
        sc = jnp.where(causal(kt), sc, NEG_INF)
        bits = pltpu.bitcast(sc, jnp.int32)
        key = jnp.where(bits >= 0, bits, bits ^ 0x7FFFFFFF)
        key_ref[kt] = jnp.where(key == -1, 0, key)
        return carry

    lax.fori_loop(0, n_valid, score_step, 0)

    def count(bound, strict):
        bb = jnp.broadcast_to(bound, (tq, LANE))

        def body(kt, c):
            for b in range(tk // LANE):
                kk = key_ref[kt, :, b * LANE:(b + 1) * LANE]
                c = c + jnp.where((kk > bb) if strict else (kk >= bb), 1.0, 0.0)
            return c
        c = lax.fori_loop(0, n_valid, body, jnp.zeros((tq, LANE), jnp.float32))
        return jnp.sum(c, axis=1, keepdims=True)

    c0 = count(jnp.zeros((tq, 1), jnp.int32), False)
    tau = jnp.where(c0 >= top, 0, INT32_MIN).astype(jnp.int32)

    def bit_step(i, tau):
        cand = tau | jnp.left_shift(jnp.int32(1), 30 - i)
        return jnp.where(count(cand, False) >= top, cand, tau)

    tau = lax.fori_loop(0, 31, bit_step, tau)
    need = top - count(tau, True)
    ri = lax.broadcasted_iota(jnp.int32, (tk, tk), 0)
    ci = lax.broadcasted_iota(jnp.int32, (tk, tk), 1)
    tri = jnp.where(ri <= ci, 1.0, 0.0).astype(MXU_DTYPE)

    def out_step(kt, seen):
        kk = key_ref[kt]
        tie = kk == tau
        tie_f = jnp.where(tie, 1.0, 0.0)
        rank = seen + jnp.dot(tie_f.astype(MXU_DTYPE), tri, preferred_element_type=jnp.float32)
        sel = (kk > tau) | (tie & (rank <= need))
        o_ref[0, 0, kt, 0] = jnp.where(sel & causal(kt), 0.0, NEG_INF).T.astype(o_ref.dtype)
        return seen + jnp.sum(tie_f, axis=1, keepdims=True)

    lax.fori_loop(0, n_valid, out_step, jnp.zeros((tq, 1), jnp.float32))

    def fill_step(kt, carry):
        o_ref[0, 0, kt, 0] = jnp.full((tk, tq), NEG_INF, o_ref.dtype)
        return carry

    lax.fori_loop(n_valid, nkt, fill_step, 0)


def dsa_select_bias(q_idx, k_idx, w_idx, top, tq_out):
    n, h, t, di = q_idx.shape
    nkt = t // K_TILE
    per_out = tq_out // Q_TILE
    return pl.pallas_call(
        functools.partial(_dsa_select_body, tq=Q_TILE, tk=K_TILE, top=top),
        grid=(n, t // Q_TILE),
        in_specs=[
            pl.BlockSpec((1, h, Q_TILE, di), lambda a, c: (a, 0, c, 0)),
            pl.BlockSpec((1, t, di), lambda a, c: (a, 0, 0)),
            pl.BlockSpec((1, Q_TILE, h), lambda a, c: (a, c, 0)),
        ],
        out_specs=pl.BlockSpec((1, 1, nkt, 1, K_TILE, Q_TILE), lambda a, c: (a, 0, 0, c // per_out, 0, c % per_out)),
        out_shape=jax.ShapeDtypeStruct((n, 1, nkt, t // tq_out, K_TILE, tq_out), BIAS_DTYPE),
        scratch_shapes=[pltpu.VMEM((nkt, Q_TILE, K_TILE), jnp.int32)],
        compiler_params=pltpu.CompilerParams(
            dimension_semantics=("arbitrary", "arbitrary"), vmem_limit_bytes=VMEM_LIMIT),
        name="dsa_select",
    )(q_idx, k_idx, w_idx)


def _topk_mask_body(s_ref, o_ref, *, k):
    s0 = s_ref[...]
    lane = lax.broadcasted_iota(jnp.int32, s0.shape, 1).astype(jnp.float32)
    width = float(s0.shape[1])

    def step(_, carry):
        s, sel = carry
        m = jnp.max(s, axis=1, keepdims=True)
        first = jnp.min(jnp.where(s == m, lane, width), axis=1, keepdims=True)
        hit = lane == first
        return jnp.where(hit, REMOVED, s), jnp.where(hit, 1.0, sel)

    _, sel = lax.fori_loop(0, k, step, (s0, jnp.zeros(s0.shape, jnp.float32)))
    o_ref[...] = sel


def topk_mask(score, k):
    lead, n = score.shape[:-1], score.shape[-1]
    rows = int(np.prod(lead))
    width = -(-n // LANE) * LANE
    s2 = jnp.pad(score.reshape(rows, n), ((0, 0), (0, width - n)), constant_values=REMOVED)
    tm = min(TOPK_ROWS, rows)
    sel = pl.pallas_call(
        functools.partial(_topk_mask_body, k=k),
        grid=(rows // tm,),
        in_specs=[pl.BlockSpec((tm, width), lambda i: (i, 0))],
        out_specs=pl.BlockSpec((tm, width), lambda i: (i, 0)),
        out_shape=jax.ShapeDtypeStruct((rows, width), jnp.float32),
        compiler_params=pltpu.CompilerParams(dimension_semantics=("arbitrary",)),
        name="topk_mask",
    )(s2)
    return (sel[:, :n] > 0.5).reshape(lead + (n,))


def _nsa_select_mask(p_cmp, qpos, n_slc):
    n, t, g = p_cmp.shape[:3]
    ratio = NSA_SEL_BLOCK // NSA_CMP_BLOCK
    imp = p_cmp.sum(axis=3)
    imp = jnp.pad(imp, ((0, 0), (0, 0), (0, 0), (0, n_slc * ratio - imp.shape[-1])))
    imp = imp.reshape(n, t, g, n_slc, ratio).sum(-1)
    j = jnp.arange(n_slc)[None, :]
    cur = (qpos // NSA_SEL_BLOCK)[:, None]
    forced = ((j == 0) | (j == cur) | (j == cur - 1))[None, :, None, :]
    future = (j > cur)[None, :, None, :]
    score = jnp.where(future, NEG_INF, jnp.where(forced, imp + NSA_FORCE_BONUS, imp))
    return topk_mask(score, min(NSA_N_SEL, n_slc))


def _slab_query(q_ref, h, r, lane_lo):
    x = q_ref[0, :, (h // 2) * LANE:(h // 2 + 1) * LANE]
    half = (h // r) % 2
    if h % 2 != half:
        x = pltpu.roll(x, HEAD_DIM, axis=1)
    keep = lane_lo if half == 0 else jnp.logical_not(lane_lo)
    return jnp.where(keep, x, 0.0).astype(MXU_DTYPE)


def _pack_head_pair(o_a, o_b, h_a, r, lane_lo):
    half = (h_a // r) % 2
    a = o_a if half == 0 else pltpu.roll(o_a, HEAD_DIM, axis=1)
    b = o_b if half == 1 else pltpu.roll(o_b, HEAD_DIM, axis=1)
    return jnp.where(lane_lo, a, b)


def _nsa_cmp_body(q_ref, ck_ref, cv_ref, o_ref, blk_ref, *, tq, r, n_sel):
    qi = pl.program_id(1)
    nb = ck_ref.shape[1]
    n_slc = nb // 2
    t = qi * tq + lax.broadcasted_iota(jnp.int32, (tq, 1), 0)
    c = lax.broadcasted_iota(jnp.int32, (1, nb), 1)
    blk_id = jnp.where(c < n_slc, 2 * c, 2 * (c - n_slc) + 1)
    visible = (blk_id + 1) * NSA_CMP_BLOCK - 1 <= t
    ck, cv = ck_ref[0], cv_ref[0]
    imps = [jnp.zeros((tq, nb), jnp.float32) for _ in range(N_HEADS // r)]
    lane_lo = lax.broadcasted_iota(jnp.int32, (tq, LANE), 1) < HEAD_DIM
    pair = []
    for h in range(N_HEADS):
        s = lax.dot_general(_slab_query(q_ref, h, r, lane_lo), ck, _AT_BT, preferred_element_type=jnp.float32)
        s = jnp.where(visible, s, NEG_INF)
        m = jnp.max(s, axis=1, keepdims=True)
        e = jnp.where(visible, jnp.exp(s - m), 0.0)
        p = e / jnp.maximum(jnp.sum(e, axis=1, keepdims=True), 1e-30)
        pair.append(jnp.dot(p.astype(MXU_DTYPE), cv, preferred_element_type=jnp.float32))
        imps[h // r] = imps[h // r] + p
        if h % 2 == 1:
            o_ref[0, :, (h // 2) * LANE:(h // 2 + 1) * LANE] = _pack_head_pair(pair[0], pair[1], h - 1, r, lane_lo)
            pair = []
    j = lax.broadcasted_iota(jnp.int32, (1, n_slc), 1)
    lane = j.astype(jnp.float32)
    cur = t // NSA_SEL_BLOCK
    forced = (j == 0) | (j == cur) | (j == cur - 1)
    future = j > cur
    scores = []
    for imp2 in imps:
        imp = imp2[:, :n_slc] + imp2[:, n_slc:]
        scores.append(jnp.where(future, NEG_INF, jnp.where(forced, imp + NSA_FORCE_BONUS, imp)))
    score = jnp.concatenate(scores, axis=0)

    def step(_, carry):
        s, sel = carry
        m = jnp.max(s, axis=1, keepdims=True)
        first = jnp.min(jnp.where(s == m, lane, float(n_slc)), axis=1, keepdims=True)
        hit = lane == first
        return jnp.where(hit, REMOVED, s), jnp.where(hit, 1.0, sel)

    _, sel = lax.fori_loop(0, n_sel, step, (score, jnp.zeros(score.shape, jnp.float32)))
    for gi in range(len(imps)):
        blk_ref[0, :, gi * n_slc:(gi + 1) * n_slc] = sel[gi * tq:(gi + 1) * tq]


def nsa_cmp_select(q, ck, cv):
    n, t, g, r, dh = q.shape
    n_slc = -(-t // NSA_SEL_BLOCK)
    n_slc_pad = -(-n_slc // LANE) * LANE
    nb = ck.shape[1]

    def arrange(x):
        x = jnp.pad(x.reshape(n, nb, g * dh), ((0, 0), (0, 2 * n_slc_pad - nb), (0, 0)))
        return jnp.concatenate([x[:, 0::2], x[:, 1::2]], axis=1).astype(MXU_DTYPE)

    qw = g * r * dh
    o, blk = pl.pallas_call(
        functools.partial(_nsa_cmp_body, tq=Q_TILE, r=r, n_sel=min(NSA_N_SEL, n_slc)),
        grid=(n, t // Q_TILE),
        in_specs=[pl.BlockSpec((1, Q_TILE, qw), lambda a, c: (a, c, 0)),
                  pl.BlockSpec((1, 2 * n_slc_pad, g * dh), lambda a, c: (a, 0, 0)),
                  pl.BlockSpec((1, 2 * n_slc_pad, g * dh), lambda a, c: (a, 0, 0))],
        out_specs=[pl.BlockSpec((1, Q_TILE, qw), lambda a, c: (a, c, 0)),
                   pl.BlockSpec((1, Q_TILE, g * n_slc_pad), lambda a, c: (a, c, 0))],
        out_shape=[jax.ShapeDtypeStruct((n, t, qw), jnp.float32),
                   jax.ShapeDtypeStruct((n, t, g * n_slc_pad), jnp.float32)],
        compiler_params=pltpu.CompilerParams(
            dimension_semantics=("arbitrary", "arbitrary"), vmem_limit_bytes=VMEM_LIMIT),
        name="nsa_cmp_select",
    )((q * (dh ** -0.5)).reshape(n, t, qw), arrange(ck), arrange(cv))
    return o.reshape(n, t, g, r, dh), blk.reshape(n, t, g, n_slc_pad)[..., :n_slc] > 0.5


def _compress_body(*refs, n_pages):
    if n_pages:
        pages, (w_ref, o_ref, x_ref) = refs[1:1 + n_pages], refs[1 + n_pages:]
        for i in range(n_pages):
            for pi in range(x_ref.shape[0]):
                x_ref[pi, i * PAGE_SIZE:(i + 1) * PAGE_SIZE, :] = pages[i][pi * LANE:(pi + 1) * LANE, :].T
    else:
        in_ref, w_ref, o_ref, x_ref = refs
        for pi in range(x_ref.shape[0]):
            x_ref[pi] = in_ref[0, :, pi * LANE:(pi + 1) * LANE]
    n_planes = x_ref.shape[0]
    nb = o_ref.shape[1]
    acc = jnp.zeros(o_ref.shape[1:], jnp.float32)
    for c in range(NSA_CMP_BLOCK):
        rows_c = jnp.concatenate([x_ref[pi, pl.ds(c, nb, stride=NSA_CMP_BLOCK), :] for pi in range(n_planes)],
                                 axis=1).astype(MXU_DTYPE)
        acc = acc + jnp.dot(rows_c, w_ref[c], preferred_element_type=jnp.float32)
    o_ref[0] = acc


def _compress_weights(w_ck, w_cv):
    w2 = jnp.stack([w_ck, w_cv])
    g = w_ck.shape[0]
    eye = jnp.eye(2 * g, dtype=w2.dtype).reshape(2, g, 2, g)
    wbd = jnp.einsum('sgcde,sgtk->csgdtke', w2, eye)
    width = 2 * g * HEAD_DIM
    return wbd.reshape(NSA_CMP_BLOCK, width, width).astype(MXU_DTYPE)


def _split_summaries(out, g):
    n, nb, _ = out.shape
    half = g * HEAD_DIM
    return out[..., :half].reshape(n, nb, g, HEAD_DIM), out[..., half:].reshape(n, nb, g, HEAD_DIM)


def nsa_compress_rows(kv_rows, w_ck, w_cv):
    n, length, _ = kv_rows.shape
    g = w_ck.shape[0]
    width = 2 * g * HEAD_DIM
    nb = length // NSA_CMP_BLOCK
    out = pl.pallas_call(
        functools.partial(_compress_body, n_pages=0),
        grid=(n,),
        in_specs=[pl.BlockSpec((1, length, width), lambda a: (a, 0, 0)),
                  pl.BlockSpec((NSA_CMP_BLOCK, width, width), lambda a: (0, 0, 0))],
        out_specs=pl.BlockSpec((1, nb, width), lambda a: (a, 0, 0)),
        out_shape=jax.ShapeDtypeStruct((n, nb, width), jnp.float32),
        scratch_shapes=[pltpu.VMEM((width // LANE, length, LANE), jnp.float32)],
        compiler_params=pltpu.CompilerParams(dimension_semantics=("arbitrary",), vmem_limit_bytes=VMEM_LIMIT),
        name="nsa_compress_rows",
    )(kv_rows, _compress_weights(w_ck, w_cv))
    return _split_summaries(out, g)


def nsa_compress_paged(pool, layer, page_table, w_ck, w_cv):
    n, n_pages = page_table.shape
    g = w_ck.shape[0]
    width = 2 * g * HEAD_DIM
    nb = n_pages * PAGE_SIZE // NSA_CMP_BLOCK
    pool_flat = _feature_major_pages(pool)
    grid_spec = pltpu.PrefetchScalarGridSpec(
        num_scalar_prefetch=1,
        grid=(n,),
        in_specs=[pl.BlockSpec((None, None, width, PAGE_SIZE),
                               functools.partial(lambda a, pt, i: (layer, pt[a * n_pages + i], 0, 0), i=i))
                  for i in range(n_pages)]
                 + [pl.BlockSpec((NSA_CMP_BLOCK, width, width), lambda a, pt: (0, 0, 0))],
        out_specs=pl.BlockSpec((1, nb, width), lambda a, pt: (a, 0, 0)),
        scratch_shapes=[pltpu.VMEM((width // LANE, n_pages * PAGE_SIZE, LANE), jnp.float32)],
    )
    out = pl.pallas_call(
        functools.partial(_compress_body, n_pages=n_pages),
        grid_spec=grid_spec,
        out_shape=jax.ShapeDtypeStruct((n, nb, width), jnp.float32),
        compiler_params=pltpu.CompilerParams(dimension_semantics=("arbitrary",), vmem_limit_bytes=VMEM_LIMIT),
        name="nsa_compress_paged",
    )(page_table.reshape(-1), *([pool_flat] * n_pages), _compress_weights(w_ck, w_cv))
    return _split_summaries(out, g)


def _band_body(*refs, tq, n_prev, span, r, has_sink):
    nk = n_prev + 1
    q_ref, k_refs, v_refs = refs[0], refs[1:1 + nk], refs[1 + nk:1 + 2 * nk]
    sink_ref = refs[1 + 2 * nk] if has_sink else None
    o_ref, lse_ref = refs[-2:]
    ui = pl.program_id(2)
    i = lax.broadcasted_iota(jnp.int32, (tq, tq), 0)
    j = lax.broadcasted_iota(jnp.int32, (tq, tq), 1)
    parts = []
    for p in range(n_prev, -1, -1):
        diff = p * tq + i - j
        parts.append((diff >= 0) & (diff < span) & (ui >= p))
    allow = jnp.concatenate(parts, axis=1)
    lane = lax.broadcasted_iota(jnp.int32, (tq, LANE), 1)
    lse_tile = jnp.zeros((tq, LANE), jnp.float32)
    lane_lo = lane < HEAD_DIM
    pair = []
    slabs = {}
    for h in range(N_HEADS):
        gp = (h // r) // 2
        if gp not in slabs:
            sl = slice(gp * LANE, (gp + 1) * LANE)
            slabs[gp] = (jnp.concatenate([kr[0, :, sl] for kr in k_refs], axis=0),
                         jnp.concatenate([vr[0, :, sl] for vr in v_refs], axis=0))
        k2, v2 = slabs[gp]
        s = lax.dot_general(_slab_query(q_ref, h, r, lane_lo), k2, _AT_BT, preferred_element_type=jnp.float32)
        s = jnp.where(allow, s, NEG_INF)
        m = jnp.max(s, axis=1, keepdims=True)
        if has_sink:
            m = jnp.maximum(m, sink_ref[h])
        e = jnp.exp(s - m)
        den = jnp.sum(e, axis=1, keepdims=True)
        if has_sink:
            den = den + jnp.exp(sink_ref[h] - m)
        den = jnp.maximum(den, 1e-30)
        pair.append(jnp.dot(e.astype(MXU_DTYPE), v2, preferred_element_type=jnp.float32) / den)
        if h % 2 == 1:
            o_ref[0, :, (h // 2) * LANE:(h // 2 + 1) * LANE] = _pack_head_pair(pair[0], pair[1], h - 1, r, lane_lo)
            pair = []
        lse_tile = jnp.where(lane == h, m + jnp.log(den), lse_tile)
    lse_ref[0] = lse_tile


def band_attn(q, k, v, dil, span, sink=None):
    n, t, g, r, dh = q.shape
    tq = Q_TILE
    n_prev = -(-(span - 1) // tq)
    assert r % 2 == 0 and g * r == N_HEADS
    qf = (q * (dh ** -0.5)).reshape(n, t, g * r * dh)
    kb = k.reshape(n, t, g * dh).astype(MXU_DTYPE)
    vb = v.reshape(n, t, g * dh).astype(MXU_DTYPE)
    qw, kw = g * r * dh, g * dh
    length = t // dil
    view = lambda x: x.reshape(n, length, dil * x.shape[-1])
    cur = lambda a, c, u: (a, u, c)
    back = [functools.partial(lambda a, c, u, p: (a, jnp.maximum(u - p, 0), c), p=p) for p in range(n_prev, -1, -1)]
    in_specs = ([pl.BlockSpec((1, tq, qw), cur)] + [pl.BlockSpec((1, tq, kw), b) for b in back] * 2)
    operands = [view(qf)] + [view(kb)] * (n_prev + 1) + [view(vb)] * (n_prev + 1)
    if sink is not None:
        in_specs.append(pl.BlockSpec(memory_space=pltpu.SMEM))
        operands.append(sink.astype(jnp.float32))
    o, lse = pl.pallas_call(
        functools.partial(_band_body, tq=tq, n_prev=n_prev, span=span, r=r, has_sink=sink is not None),
        grid=(n, dil, length // tq),
        in_specs=in_specs,
        out_specs=[pl.BlockSpec((1, tq, qw), cur), pl.BlockSpec((1, tq, LANE), cur)],
        out_shape=[jax.ShapeDtypeStruct((n, length, dil * qw), jnp.float32),
                   jax.ShapeDtypeStruct((n, length, dil * LANE), jnp.float32)],
        compiler_params=pltpu.CompilerParams(
            dimension_semantics=("arbitrary", "arbitrary", "arbitrary"), vmem_limit_bytes=VMEM_LIMIT),
        name="band_attn",
    )(*operands)
    return o.reshape(n, t, g, r, dh), lse.reshape(n, t, LANE)[:, :, :N_HEADS].reshape(n, t, g, r)


def nsa_prompt_p(z, w_ck, w_cv):
    n, s, _ = z.shape
    pos = jnp.arange(s)
    q, kv, gates = _nsa_project(z, pos)
    ck, cv = nsa_compress_rows(kv.reshape(n, s, -1), w_ck, w_cv)
    o_cmp, blk = nsa_cmp_select(q, ck, cv)
    o_slc = masked_flash(q, kv[:, :, 2], kv[:, :, 3], jnp.transpose(blk, (0, 2, 3, 1)), NSA_SEL_BLOCK)
    o_win, _ = band_attn(q, kv[:, :, 4], kv[:, :, 5], 1, NSA_WINDOW)
    return _nsa_gate(gates, o_cmp, o_slc, o_win), kv[:, :, 0:4], _tail(kv[:, :, 4:6], NSA_WINDOW)


def dsa_prompt_p(z):
    n, s, _ = z.shape
    pos = jnp.arange(s)
    q, kv, q_idx, k_idx, w_idx = _dsa_project(z, pos)
    top = min(DSA_TOPK, s // 4)
    bias = dsa_select_bias(jnp.transpose(q_idx, (0, 2, 1, 3)).astype(MXU_DTYPE),
                           k_idx.astype(MXU_DTYPE), w_idx, top, FLASH_ROWS // q.shape[3])
    o = masked_flash(q, kv[:, :, 0], kv[:, :, 1], bias_tiles=bias)
    return o.reshape(n, s, Q_DIM), kv, k_idx


def dil_prompt_p(z):
    n, s, _ = z.shape
    q, kv = _dil_project(z, jnp.arange(s))
    outs, lses, bufs = [], [], []
    for gi, (win, dil) in enumerate(DIL_GROUPS):
        o, lse = band_attn(q, kv[:, :, gi, 0], kv[:, :, gi, 1], dil, win // dil + 1)
        outs.append(o)
        lses.append(lse)
        bufs.append(_tail(kv[:, :, gi], win))
    return _dil_mix(outs, lses).astype(z.dtype).reshape(n, s, Q_DIM), bufs


def swa_prompt_p(z, sink):
    n, s, _ = z.shape
    q, kv = _swa_project(z, jnp.arange(s))
    o, _ = band_attn(q, kv[:, :, 0], kv[:, :, 1], 1, SWA_WINDOW, sink)
    return o.reshape(n, s, Q_DIM), _tail(kv, SWA_WINDOW)


def _flash_update(s, pv, m_ref, l_ref, acc_ref):
    m_old = m_ref[...]
    m_new = jnp.maximum(m_old, jnp.max(s, axis=1, keepdims=True))
    p = jnp.exp(s - m_new)
    alpha = jnp.exp(m_old - m_new)
    l_ref[...] = alpha * l_ref[...] + jnp.sum(p, axis=1, keepdims=True)
    acc_ref[...] = alpha * acc_ref[...] + pv(p.astype(MXU_DTYPE))
    m_ref[...] = m_new


def _paged_attn_body(pt_ref, q_ref, bn_ref, kn_ref, vn_ref, b_ref, *rest, ppc):
    pages = rest[:ppc]
    o_ref, m_ref, l_ref, acc_ref = rest[ppc:]
    lw = q_ref.shape[2]
    c = pl.program_id(1)
    q = q_ref[0]

    @pl.when(c == 0)
    def _():
        m_ref[...] = jnp.full(m_ref.shape, NEG_INF, jnp.float32)
        l_ref[...] = jnp.zeros(l_ref.shape, jnp.float32)
        acc_ref[...] = jnp.zeros(acc_ref.shape, jnp.float32)
        s = lax.dot_general(q, kn_ref[0], _AT_BT, preferred_element_type=jnp.float32) + bn_ref[0].astype(jnp.float32)
        _flash_update(s, lambda p: jnp.dot(p, vn_ref[0], preferred_element_type=jnp.float32), m_ref, l_ref, acc_ref)

    parts = [jnp.dot(q, pages[i][0:lw, :].astype(MXU_DTYPE), preferred_element_type=jnp.float32) for i in range(ppc)]
    s = jnp.concatenate(parts, axis=1) + b_ref[0].astype(jnp.float32)

    def pv(p):
        acc = lax.dot_general(p[:, 0:PAGE_SIZE], pages[0][lw:2 * lw, :].astype(MXU_DTYPE), _AT_BT,
                              preferred_element_type=jnp.float32)
        for i in range(1, ppc):
            acc = acc + lax.dot_general(p[:, i * PAGE_SIZE:(i + 1) * PAGE_SIZE],
                                        pages[i][lw:2 * lw, :].astype(MXU_DTYPE), _AT_BT,
                                        preferred_element_type=jnp.float32)
        return acc

    _flash_update(s, pv, m_ref, l_ref, acc_ref)

    @pl.when(c == pl.num_programs(1) - 1)
    def _():
        o_ref[0] = acc_ref[...] / jnp.maximum(l_ref[...], 1e-30)


def paged_attn(q2, pool, layer, page_table, kv_blk, bias, k_new, v_new, bias_new):
    n, rows, lw = q2.shape
    n_pages = page_table.shape[1]
    ppc = min(PAGES_PER_CHUNK, n_pages)
    n_chunks = n_pages // ppc

    def page_spec(i):
        return pl.BlockSpec((None, None, 2 * lw, PAGE_SIZE),
                            lambda a, c, pt: (layer, pt[a * n_pages + c * ppc + i], kv_blk, 0))

    per_seq = lambda a, c, pt: (a, 0, 0)
    grid_spec = pltpu.PrefetchScalarGridSpec(
        num_scalar_prefetch=1,
        grid=(n, n_chunks),
        in_specs=[pl.BlockSpec((1, rows, lw), per_seq),
                  pl.BlockSpec((1, rows, PAGE_SIZE), per_seq),
                  pl.BlockSpec((1, PAGE_SIZE, lw), per_seq),
                  pl.BlockSpec((1, PAGE_SIZE, lw), per_seq),
                  pl.BlockSpec((1, rows, ppc * PAGE_SIZE), lambda a, c, pt: (a, 0, c))]
                 + [page_spec(i) for i in range(ppc)],
        out_specs=pl.BlockSpec((1, rows, lw), per_seq),
        scratch_shapes=[pltpu.VMEM((rows, 1), jnp.float32),
                        pltpu.VMEM((rows, 1), jnp.float32),
                        pltpu.VMEM((rows, lw), jnp.float32)],
    )
    return pl.pallas_call(
        functools.partial(_paged_attn_body, ppc=ppc),
        grid_spec=grid_spec,
        out_shape=jax.ShapeDtypeStruct((n, rows, lw), jnp.float32),
        compiler_params=pltpu.CompilerParams(
            dimension_semantics=("arbitrary", "arbitrary"), vmem_limit_bytes=VMEM_LIMIT),
        name="paged_attn",
    )(page_table.reshape(-1), q2, bias_new, k_new, v_new, bias, *([pool] * ppc))


def _feature_major_pages(pool):
    nd = pool.ndim
    t = jnp.transpose(pool, (0, 1) + tuple(range(3, nd)) + (2,))
    return t.reshape(pool.shape[0], pool.shape[1], -1, pool.shape[2])


def _group_lane_queries(q):
    n, t, g, r, dh = q.shape
    qs = jnp.transpose(q * (dh ** -0.5), (0, 2, 1, 3, 4))
    onehot = jnp.eye(g, dtype=qs.dtype)
    q2 = qs[:, :, :, :, None, :] * onehot[None, :, None, None, :, None]
    return q2.reshape(n, g * t * r, g * dh).astype(MXU_DTYPE)


def _ungroup_lanes(o, t, g, r, dh):
    n = o.shape[0]
    o6 = o.reshape(n, g, t, r, g, dh)
    return jnp.stack([o6[:, gi, :, :, gi] for gi in range(g)], axis=2)


def _rows_bias(allow, g, r):
    n, t, ga, k = allow.shape
    a = jnp.broadcast_to(jnp.transpose(allow, (0, 2, 1, 3))[:, :, :, None, :], (n, ga, t, r, k))
    if ga != g:
        a = jnp.broadcast_to(a, (n, g, t, r, k))
    return jnp.where(a, 0.0, NEG_INF).astype(BIAS_DTYPE).reshape(n, g * t * r, k)


def _new_rows_page(x, lw):
    n, t, _ = x.shape
    return jnp.pad(x, ((0, 0), (0, PAGE_SIZE - t), (0, 0))).astype(MXU_DTYPE)


def nsa_sample_p(z, pool, layer, win_buf, page_table, w_ck, w_cv, past):
    n, t, _ = z.shape
    g, r = NSA_KV_HEADS, N_HEADS // NSA_KV_HEADS
    pos = past + jnp.arange(t)
    q, kv, gates = _nsa_project(z, pos)
    ck_p, cv_p = nsa_compress_paged(pool, layer, page_table, w_ck, w_cv)
    ck_n, cv_n = _nsa_compress(kv[:, :, 0:2], w_ck, w_cv)
    o_cmp, p_cmp = _nsa_cmp_attend(q, jnp.concatenate([ck_p, ck_n], 1),
                                   jnp.concatenate([cv_p, cv_n], 1), pos)
    n_slc = -(-(past + t) // NSA_SEL_BLOCK)
    blk = _nsa_select_mask(p_cmp, pos, n_slc)
    n_past_blk = past // NSA_SEL_BLOCK
    allow_past = jnp.repeat(blk[..., :n_past_blk], NSA_SEL_BLOCK, axis=-1)
    j = jnp.arange(PAGE_SIZE)
    new_blk = (past + j) // NSA_SEL_BLOCK
    allow_new = (jnp.take(blk, jnp.minimum(new_blk, n_slc - 1), axis=-1)
                 & (j[None, :] <= jnp.arange(t)[:, None])[None, :, None, :] & (j < t))
    lw = g * HEAD_DIM
    pool_flat = _feature_major_pages(pool)
    o = paged_attn(_group_lane_queries(q), pool_flat, layer, page_table, 1,
                   _rows_bias(allow_past, g, r),
                   _new_rows_page(kv[:, :, 2].reshape(n, t, lw), lw),
                   _new_rows_page(kv[:, :, 3].reshape(n, t, lw), lw),
                   _rows_bias(allow_new, g, r))
    o_slc = _ungroup_lanes(o, t, g, r, HEAD_DIM)
    keys = jnp.concatenate([win_buf, kv[:, :, 4:6]], axis=1)
    kp = past - win_buf.shape[1] + jnp.arange(keys.shape[1])
    diff = pos[:, None] - kp[None, :]
    o_win, _ = _attend_dense(q, keys[:, :, 0], keys[:, :, 1], ((diff >= 0) & (diff < NSA_WINDOW))[None])
    return _nsa_gate(gates, o_cmp, o_slc, o_win), kv[:, :, 0:4], _tail(keys, NSA_WINDOW)


def _dsa_sample_select_body(pt_ref, qx_ref, w_ref, kn_ref, *rest, n_pages, t_new, top):
    pages = rest[:n_pages]
    o_ref, key_ref = rest[n_pages:]
    qx = qx_ref[0]
    w = w_ref[0]
    row = lax.broadcasted_iota(jnp.int32, (8, 1), 0)
    col = lax.broadcasted_iota(jnp.int32, (1, PAGE_SIZE), 1)
    new_ok = (col <= row) & (col < t_new)

    for i in range(n_pages + 1):
        if i == n_pages:
            d = lax.dot_general(qx, kn_ref[0], _AT_BT, preferred_element_type=jnp.float32)
        else:
            d = jnp.dot(qx, pages[i][...].astype(MXU_DTYPE), preferred_element_type=jnp.float32)
        d = jnp.maximum(d, 0.0).reshape(IDX_HEADS, 8, PAGE_SIZE)
        sc = w[:, 0:1] * d[0]
        for hh in range(1, IDX_HEADS):
            sc = sc + w[:, hh:hh + 1] * d[hh]
        if i == n_pages:
            sc = jnp.where(new_ok, sc, NEG_INF)
        bits = pltpu.bitcast(sc, jnp.int32)
        key = jnp.where(bits >= 0, bits, bits ^ 0x7FFFFFFF)
        key_ref[i] = jnp.where(key == -1, 0, key)

    def count(pred):
        hit = jnp.where(pred(key_ref[...]), 1.0, 0.0)
        return jnp.sum(jnp.sum(hit, axis=0), axis=1, keepdims=True)

    c0 = count(lambda kk: kk >= 0)
    tau = jnp.where(c0 >= top, 0, INT32_MIN).astype(jnp.int32)

    def bit_step(i, tau):
        cand = tau | jnp.left_shift(jnp.int32(1), 30 - i)
        return jnp.where(count(lambda kk: kk >= cand) >= top, cand, tau)

    tau = lax.fori_loop(0, 31, bit_step, tau)
    need = top - count(lambda kk: kk > tau)
    ri = lax.broadcasted_iota(jnp.int32, (PAGE_SIZE, PAGE_SIZE), 0)
    ci = lax.broadcasted_iota(jnp.int32, (PAGE_SIZE, PAGE_SIZE), 1)
    tri = jnp.where(ri <= ci, 1.0, 0.0).astype(MXU_DTYPE)
    seen = jnp.zeros((8, 1), jnp.float32)
    for i in range(n_pages + 1):
        kk = key_ref[i]
        tie = kk == tau
        tie_f = jnp.where(tie, 1.0, 0.0)
        rank = seen + jnp.dot(tie_f.astype(MXU_DTYPE), tri, preferred_element_type=jnp.float32)
        sel = (kk > tau) | (tie & (rank <= need))
        if i == n_pages:
            sel = sel & new_ok
        o_ref[0, :, i * PAGE_SIZE:(i + 1) * PAGE_SIZE] = jnp.where(sel, 0.0, NEG_INF)
        seen = seen + jnp.sum(tie_f, axis=1, keepdims=True)


def dsa_sample_select(q_idx, k_idx_new, w_idx, pool_idx, layer, page_table, top):
    n, t, hh, di = q_idx.shape
    n_pages = page_table.shape[1]
    qx = jnp.pad(jnp.transpose(q_idx, (0, 2, 1, 3)), ((0, 0), (0, 0), (0, 8 - t), (0, 0)))
    qx = qx.reshape(n, hh * 8, di).astype(MXU_DTYPE)
    w8 = jnp.pad(w_idx, ((0, 0), (0, 8 - t), (0, 0)))
    kn = _new_rows_page(k_idx_new, di)
    per_seq = lambda a, pt: (a, 0, 0)
    total = (n_pages + 1) * PAGE_SIZE
    grid_spec = pltpu.PrefetchScalarGridSpec(
        num_scalar_prefetch=1,
        grid=(n,),
        in_specs=[pl.BlockSpec((1, hh * 8, di), per_seq),
                  pl.BlockSpec((1, 8, hh), per_seq),
                  pl.BlockSpec((1, PAGE_SIZE, di), per_seq)]
                 + [pl.BlockSpec((None, None, di, PAGE_SIZE),
                                 functools.partial(lambda a, pt, i: (layer, pt[a * n_pages + i], 0, 0), i=i))
                    for i in range(n_pages)],
        out_specs=pl.BlockSpec((1, 8, total), per_seq),
        scratch_shapes=[pltpu.VMEM((n_pages + 1, 8, PAGE_SIZE), jnp.int32)],
    )
    return pl.pallas_call(
        functools.partial(_dsa_sample_select_body, n_pages=n_pages, t_new=t, top=top),
        grid_spec=grid_spec,
        out_shape=jax.ShapeDtypeStruct((n, 8, total), jnp.float32),
        compiler_params=pltpu.CompilerParams(
            dimension_semantics=("arbitrary",), vmem_limit_bytes=VMEM_LIMIT),
        name="dsa_sample_select",
    )(page_table.reshape(-1), qx, w8, kn, *([_feature_major_pages(pool_idx)] * n_pages))


def dsa_sample_p(z, pool_kv, pool_idx, layer, page_table, past):
    n, t, _ = z.shape
    g, r = DSA_KV_HEADS, N_HEADS // DSA_KV_HEADS
    pos = past + jnp.arange(t)
    q, kv, q_idx, k_idx, w_idx = _dsa_project(z, pos)
    total = past + t
    bias8 = dsa_sample_select(q_idx, k_idx, w_idx, pool_idx, layer, page_table, min(DSA_TOPK, total // 4))
    allow = (bias8[:, :t] == 0.0)[:, :, None, :]
    lw = g * HEAD_DIM
    pool_flat = _feature_major_pages(pool_kv)
    o = paged_attn(_group_lane_queries(q), pool_flat, layer, page_table, 0,
                   _rows_bias(allow[..., :past], g, r),
                   _new_rows_page(kv[:, :, 0].reshape(n, t, lw), lw),
                   _new_rows_page(kv[:, :, 1].reshape(n, t, lw), lw),
                   _rows_bias(allow[..., past:], g, r))
    return _ungroup_lanes(o, t, g, r, HEAD_DIM).reshape(n, t, Q_DIM), kv, k_idx


def dil_sample(z, bufs, past):
    n, t, _ = z.shape
    pos = past + jnp.arange(t)
    q, kv = _dil_project(z, pos)
    outs, lses, new_bufs = [], [], []
    for gi, ((win, dil), buf) in enumerate(zip(DIL_GROUPS, bufs)):
        keys = jnp.concatenate([buf, kv[:, :, gi]], axis=1)
        base = past - buf.shape[1]
        local = pos[:, None] - (jnp.arange(win // dil + 1) * dil)[None, :] - base
        lc = jnp.maximum(local, 0)
        o, lse = _attend_gathered(q, keys[:, lc, 0], keys[:, lc, 1], (local >= 0)[None, :, :, None])
        outs.append(o)
        lses.append(lse)
        new_bufs.append(_tail(keys, win))
    return _dil_mix(outs, lses).astype(z.dtype).reshape(n, t, Q_DIM), new_bufs


def swa_sample(z, buf, sink, past):
    n, t, _ = z.shape
    pos = past + jnp.arange(t)
    q, kv = _swa_project(z, pos)
    keys = jnp.concatenate([buf, kv], axis=1)
    kp = past - buf.shape[1] + jnp.arange(keys.shape[1])
    diff = pos[:, None] - kp[None, :]
    o, _ = _attend_dense(q, keys[:, :, 0], keys[:, :, 1], ((diff >= 0) & (diff < SWA_WINDOW))[None],
                         sink.reshape(SWA_KV_HEADS, -1))
    return o.reshape(n, t, Q_DIM), _tail(keys, SWA_WINDOW)


def _pad_cols(w):
    n_out = w.shape[1]
    return jnp.pad(w, ((0, 0), (0, -n_out % LANE))).astype(MXU_DTYPE)


def _project(h, g, w_in):
    n, t, _ = h.shape
    z = norm_proj(h.reshape(n * t, D_MODEL), g, _pad_cols(w_in))
    return z[:, :w_in.shape[1]].reshape(n, t, w_in.shape[1])


def _add_out_proj(h, o, w_out):
    n, t, _ = h.shape
    return proj_residual(o.reshape(n * t, Q_DIM), w_out.astype(MXU_DTYPE), h.reshape(n * t, D_MODEL)).reshape(h.shape)


def _add_ffn(h, g, w_in_b, w_out_b):
    return ffn_residual(h.reshape(-1, D_MODEL), g, w_in_b, w_out_b).reshape(h.shape)


def kernel(x_prompt, x_sample, cache_nsa_kv, cache_nsa_win, cache_dil1, cache_dil2, cache_dil3,
           cache_dsa_kv, cache_dsa_idx, cache_swa, page_table,
           norm_mix, norm_ffn, norm_final, ffn_in, ffn_out,
           nsa_w_in, nsa_w_cmp_k, nsa_w_cmp_v, nsa_w_out,
           dil_w_in, dil_w_out, dsa_w_in, dsa_w_out,
           swa_w_in, swa_sink, swa_w_out):
    past = page_table.shape[1] * PAGE_SIZE
    hp, hs = x_prompt, x_sample
    ffn_in_b = ffn_in.astype(MXU_DTYPE)
    ffn_out_b = ffn_out.astype(MXU_DTYPE)
    st = {name: [] for name in ("nsa_kv_p", "nsa_kv_s", "nsa_win_p", "nsa_win_s",
                                "dil1_p", "dil1_s", "dil2_p", "dil2_s", "dil3_p", "dil3_s",
                                "dsa_kv_p", "dsa_kv_s", "dsa_idx_p", "dsa_idx_s", "swa_p", "swa_s")}
    for i in range(DEPTH):
        kind, j = i % N_MIXERS, i // N_MIXERS
        w_in, w_out = ((nsa_w_in, nsa_w_out), (dil_w_in, dil_w_out), (dsa_w_in, dsa_w_out), (swa_w_in, swa_w_out))[kind]
        zp, zs = _project(hp, norm_mix[i], w_in[j]), _project(hs, norm_mix[i], w_in[j])
        if kind == 0:
            op, kv_p, win_p = nsa_prompt_p(zp, nsa_w_cmp_k[j], nsa_w_cmp_v[j])
            os_, kv_s, win_s = nsa_sample_p(zs, cache_nsa_kv, j, cache_nsa_win[j], page_table,
                                            nsa_w_cmp_k[j], nsa_w_cmp_v[j], past)
            st["nsa_kv_p"].append(kv_p)
            st["nsa_kv_s"].append(kv_s)
            st["nsa_win_p"].append(win_p)
            st["nsa_win_s"].append(win_s)
        elif kind == 1:
            op, bufs_p = dil_prompt_p(zp)
            os_, bufs_s = dil_sample(zs, [cache_dil1[j], cache_dil2[j], cache_dil3[j]], past)
            for gi in range(len(DIL_GROUPS)):
                st["dil%d_p" % (gi + 1)].append(bufs_p[gi])
                st["dil%d_s" % (gi + 1)].append(bufs_s[gi])
        elif kind == 2:
            op, kv_p, idx_p = dsa_prompt_p(zp)
            os_, kv_s, idx_s = dsa_sample_p(zs, cache_dsa_kv, cache_dsa_idx, j, page_table, past)
            st["dsa_kv_p"].append(kv_p)
            st["dsa_kv_s"].append(kv_s)
            st["dsa_idx_p"].append(idx_p)
            st["dsa_idx_s"].append(idx_s)
        else:
            op, buf_p = swa_prompt_p(zp, swa_sink[j])
            os_, buf_s = swa_sample(zs, cache_swa[j], swa_sink[j], past)
            st["swa_p"].append(buf_p)
            st["swa_s"].append(buf_s)
        hp = _add_out_proj(hp, op, w_out[j])
        hs = _add_out_proj(hs, os_, w_out[j])
        hp = _add_ffn(hp, norm_ffn[i], ffn_in_b[i], ffn_out_b[i])
        hs = _add_ffn(hs, norm_ffn[i], ffn_in_b[i], ffn_out_b[i])
    y_prompt = rms_norm(hp, norm_final)
    y_sample = rms_norm(hs, norm_final)
    return (y_prompt, y_sample,
            jnp.stack(st["nsa_kv_p"]), jnp.stack(st["nsa_kv_s"]),
            jnp.stack(st["nsa_win_p"]), jnp.stack(st["nsa_win_s"]),
            jnp.stack(st["dil1_p"]), jnp.stack(st["dil1_s"]),
            jnp.stack(st["dil2_p"]), jnp.stack(st["dil2_s"]),
            jnp.stack(st["dil3_p"]), jnp.stack(st["dil3_s"]),
            jnp.stack(st["dsa_kv_p"]), jnp.stack(st["dsa_kv_s"]),
            jnp.stack(st["dsa_idx_p"]), jnp.stack(st["dsa_idx_s"]),
            jnp.stack(st["swa_p"]), jnp.stack(st["swa_s"]))
```

```python
import functools

import jax, jax.numpy as jnp
from jax import lax
import numpy as np
from jax.experimental import pallas as pl
from jax.experimental.pallas import tpu as pltpu

D_MODEL = 1024
BATCH = 2
SEQ = 8192
DEPTH = 4
DEC_BATCH = 128
DEC_SEQ = 4
PAST_LEN = 8192
PAGE_SIZE = 128

HEAD_DIM = 64
N_HEADS = D_MODEL // HEAD_DIM
Q_DIM = N_HEADS * HEAD_DIM
ROPE_THETA = 10000.0
NORM_EPS = 1e-6
N_MIXERS = 4
D_FF = -(-(8 * D_MODEL) // (3 * 256)) * 256
NEG_INF = -1e30

NSA_KV_HEADS = 2
NSA_CMP_BLOCK = 32
NSA_SEL_BLOCK = 64
NSA_N_SEL = 16
NSA_WINDOW = 512
NSA_FORCE_BONUS = 1e4
NSA_IN = Q_DIM + 6 * NSA_KV_HEADS * HEAD_DIM + 3 * N_HEADS

DIL_KV_HEADS = 4
DIL_GROUPS = ((128, 1), (512, 4), (2048, 16))
DIL_IN = Q_DIM + len(DIL_GROUPS) * 2 * DIL_KV_HEADS * HEAD_DIM

DSA_KV_HEADS = 4
IDX_HEADS = 8
IDX_DIM = 64
DSA_TOPK = 256
IDX_SCALE = (IDX_DIM * IDX_HEADS) ** -0.5
DSA_IN = Q_DIM + 2 * DSA_KV_HEADS * HEAD_DIM + IDX_HEADS * IDX_DIM + IDX_DIM + IDX_HEADS

SWA_KV_HEADS = 2
SWA_WINDOW = 128
SWA_IN = Q_DIM + 2 * SWA_KV_HEADS * HEAD_DIM


def rms_norm(x, g):
    xf = x.astype(jnp.float32)
    y = xf * lax.rsqrt(jnp.mean(xf * xf, axis=-1, keepdims=True) + NORM_EPS)
    return (y * g.astype(jnp.float32)).astype(x.dtype)


def rope(x, pos):
    half = x.shape[-1] // 2
    inv = ROPE_THETA ** (-jnp.arange(half, dtype=jnp.float32) / half)
    ang = pos.astype(jnp.float32)[:, None] * inv[None, :]
    shape = (pos.shape[0],) + (1,) * (x.ndim - 3) + (half,)
    cos, sin = jnp.cos(ang).reshape(shape), jnp.sin(ang).reshape(shape)
    xf = x.astype(jnp.float32)
    x1, x2 = xf[..., :half], xf[..., half:]
    return jnp.concatenate([x1 * cos - x2 * sin, x2 * cos + x1 * sin], axis=-1).astype(x.dtype)


def _tail(rows, window):
    n = rows.shape[1]
    return rows[:, n - min(window, n):]


def _masked_softmax(s, mask, sink=None):
    s = jnp.where(mask, s, NEG_INF)
    m = jnp.max(s, axis=-1, keepdims=True)
    if sink is not None:
        m = jnp.maximum(m, sink)
    e = jnp.where(mask, jnp.exp(s - m), 0.0)
    den = jnp.sum(e, axis=-1, keepdims=True)
    if sink is not None:
        den = den + jnp.exp(sink - m)
    den = jnp.maximum(den, 1e-30)
    return e / den, (m + jnp.log(den))[..., 0]


def _attend_dense(q, k, v, mask, sink=None):
    s = jnp.einsum('nqgrd,nkgd->ngrqk', q, k, preferred_element_type=jnp.float32) * (q.shape[-1] ** -0.5)
    sk = None if sink is None else sink.astype(jnp.float32)[None, :, :, None, None]
    p, lse = _masked_softmax(s, mask[:, None, None], sk)
    o = jnp.einsum('ngrqk,nkgd->nqgrd', p.astype(v.dtype), v)
    return o, jnp.transpose(lse, (0, 3, 1, 2))


def _attend_gathered(q, k, v, mask):
    s = jnp.einsum('nqgrd,nqkgd->nqgrk', q, k, preferred_element_type=jnp.float32) * (q.shape[-1] ** -0.5)
    p, lse = _masked_softmax(s, jnp.swapaxes(mask, 2, 3)[:, :, :, None, :])
    o = jnp.einsum('nqgrk,nqkgd->nqgrd', p.astype(v.dtype), v)
    return o, lse


def _nsa_project(z, pos):
    n, t, _ = z.shape
    g, r = NSA_KV_HEADS, N_HEADS // NSA_KV_HEADS
    kv_end = Q_DIM + 6 * g * HEAD_DIM
    q = rope(z[..., :Q_DIM].reshape(n, t, g, r, HEAD_DIM), pos)
    kv = z[..., Q_DIM:kv_end].reshape(n, t, 3, 2, g, HEAD_DIM)
    kv = jnp.stack([rope(kv[:, :, :, 0], pos), kv[:, :, :, 1]], axis=3).reshape(n, t, 6, g, HEAD_DIM)
    gates = jax.nn.sigmoid(z[..., kv_end:].astype(jnp.float32)).reshape(n, t, g, r, 3)
    return q, kv, gates


def _nsa_compress(rows, w_ck, w_cv):
    n, length, _, g, dh = rows.shape
    nb = length // NSA_CMP_BLOCK
    r = rows[:, :nb * NSA_CMP_BLOCK].reshape(n, nb, NSA_CMP_BLOCK, 2, g, dh)
    ck = jnp.einsum('nbcgd,gcde->nbge', r[:, :, :, 0], w_ck)
    cv = jnp.einsum('nbcgd,gcde->nbge', r[:, :, :, 1], w_cv)
    return ck, cv


def _nsa_cmp_attend(q, ck, cv, qpos):
    blk_end = (jnp.arange(ck.shape[1]) + 1) * NSA_CMP_BLOCK - 1
    mask = (blk_end[None, :] <= qpos[:, None])[None, :, None, None, :]
    s = jnp.einsum('nqgrd,ncgd->nqgrc', q, ck, preferred_element_type=jnp.float32) * (q.shape[-1] ** -0.5)
    p, _ = _masked_softmax(s, mask)
    o = jnp.einsum('nqgrc,ncgd->nqgrd', p.astype(cv.dtype), cv)
    return o, p


def _nsa_gate(gates, o_cmp, o_slc, o_win):
    f = jnp.float32
    o = (gates[..., 0:1] * o_cmp.astype(f) + gates[..., 1:2] * o_slc.astype(f)
         + gates[..., 2:3] * o_win.astype(f))
    n, t = o.shape[:2]
    return o.reshape(n, t, Q_DIM)


def _dil_project(z, pos):
    n, t, _ = z.shape
    g = DIL_KV_HEADS
    q = rope(z[..., :Q_DIM].reshape(n, t, g, N_HEADS // g, HEAD_DIM), pos)
    kv = z[..., Q_DIM:].reshape(n, t, len(DIL_GROUPS), 2, g, HEAD_DIM)
    return q, jnp.stack([rope(kv[:, :, :, 0], pos), kv[:, :, :, 1]], axis=3)


def _dil_mix(outs, lses):
    w = jax.nn.softmax(jnp.stack(lses), axis=0)
    return jnp.sum(w[..., None] * jnp.stack(outs).astype(jnp.float32), axis=0)


def _dsa_project(z, pos):
    n, t, _ = z.shape
    g = DSA_KV_HEADS
    o1 = Q_DIM
    o2 = o1 + 2 * g * HEAD_DIM
    o3 = o2 + IDX_HEADS * IDX_DIM
    o4 = o3 + IDX_DIM
    q = rope(z[..., :o1].reshape(n, t, g, N_HEADS // g, HEAD_DIM), pos)
    kv = z[..., o1:o2].reshape(n, t, 2, g, HEAD_DIM)
    kv = jnp.stack([rope(kv[:, :, 0], pos), kv[:, :, 1]], axis=2)
    q_idx = rope(z[..., o2:o3].reshape(n, t, IDX_HEADS, IDX_DIM), pos)
    k_idx = rope(z[..., o3:o4], pos)
    w_idx = z[..., o4:].astype(jnp.float32) * IDX_SCALE
    return q, kv, q_idx, k_idx, w_idx


def _swa_project(z, pos):
    n, t, _ = z.shape
    g = SWA_KV_HEADS
    q = rope(z[..., :Q_DIM].reshape(n, t, g, N_HEADS // g, HEAD_DIM), pos)
    kv = z[..., Q_DIM:].reshape(n, t, 2, g, HEAD_DIM)
    return q, jnp.stack([rope(kv[:, :, 0], pos), kv[:, :, 1]], axis=2)


MXU_DTYPE = jnp.bfloat16
BIAS_DTYPE = jnp.bfloat16
ROW_TILE = 512
FF_CHUNK = 256
VMEM_LIMIT = 56 * 1024 * 1024
LANE = 128
PAGES_PER_CHUNK = 64
Q_TILE = 128
K_TILE = 512
FLASH_ROWS = 2048
TOPK_ROWS = 256
REMOVED = -3e38
INT32_MIN = -2 ** 31
_AT_BT = (((1,), (1,)), ((), ()))


def _ffn_body(x_ref, g_ref, win_ref, wout_ref, o_ref):
    x = x_ref[...]
    xn = x * lax.rsqrt(jnp.mean(x * x, axis=-1, keepdims=True) + NORM_EPS) * g_ref[...]
    xb = xn.astype(MXU_DTYPE)
    acc = x
    for j in range(D_FF // FF_CHUNK):
        lo = j * FF_CHUNK
        gate = jnp.dot(xb, win_ref[:, lo:lo + FF_CHUNK], preferred_element_type=jnp.float32)
        up = jnp.dot(xb, win_ref[:, D_FF + lo:D_FF + lo + FF_CHUNK], preferred_element_type=jnp.float32)
        a = (gate * jax.nn.sigmoid(gate) * up).astype(MXU_DTYPE)
        acc = acc + jnp.dot(a, wout_ref[lo:lo + FF_CHUNK, :], preferred_element_type=jnp.float32)
    o_ref[...] = acc


def ffn_residual(x, g, w_in_bf16, w_out_bf16):
    rows = x.shape[0]
    return pl.pallas_call(
        _ffn_body,
        grid=(rows // ROW_TILE,),
        in_specs=[
            pl.BlockSpec((ROW_TILE, D_MODEL), lambda i: (i, 0)),
            pl.BlockSpec((1, D_MODEL), lambda i: (0, 0)),
            pl.BlockSpec((D_MODEL, 2 * D_FF), lambda i: (0, 0)),
            pl.BlockSpec((D_FF, D_MODEL), lambda i: (0, 0)),
        ],
        out_specs=pl.BlockSpec((ROW_TILE, D_MODEL), lambda i: (i, 0)),
        out_shape=jax.ShapeDtypeStruct((rows, D_MODEL), jnp.float32),
        compiler_params=pltpu.CompilerParams(
            dimension_semantics=("arbitrary",), vmem_limit_bytes=VMEM_LIMIT),
        name="ffn_residual",
    )(x, g.reshape(1, D_MODEL), w_in_bf16, w_out_bf16)


def _norm_proj_body(x_ref, g_ref, w_ref, o_ref):
    x = x_ref[...]
    xn = x * lax.rsqrt(jnp.mean(x * x, axis=-1, keepdims=True) + NORM_EPS) * g_ref[...]
    o_ref[...] = jnp.dot(xn.astype(MXU_DTYPE), w_ref[...], preferred_element_type=jnp.float32)


def norm_proj(x, g, w_b):
    rows, n_out = x.shape[0], w_b.shape[1]
    return pl.pallas_call(
        _norm_proj_body,
        grid=(rows // ROW_TILE,),
        in_specs=[pl.BlockSpec((ROW_TILE, D_MODEL), lambda i: (i, 0)),
                  pl.BlockSpec((1, D_MODEL), lambda i: (0, 0)),
                  pl.BlockSpec((D_MODEL, n_out), lambda i: (0, 0))],
        out_specs=pl.BlockSpec((ROW_TILE, n_out), lambda i: (i, 0)),
        out_shape=jax.ShapeDtypeStruct((rows, n_out), jnp.float32),
        compiler_params=pltpu.CompilerParams(dimension_semantics=("arbitrary",), vmem_limit_bytes=VMEM_LIMIT),
        name="norm_proj",
    )(x, g.reshape(1, D_MODEL), w_b)


def _proj_residual_body(a_ref, w_ref, r_ref, o_ref):
    o_ref[...] = r_ref[...] + jnp.dot(a_ref[...].astype(MXU_DTYPE), w_ref[...], preferred_element_type=jnp.float32)


def proj_residual(a, w_b, res):
    rows = a.shape[0]
    return pl.pallas_call(
        _proj_residual_body,
        grid=(rows // ROW_TILE,),
        in_specs=[pl.BlockSpec((ROW_TILE, Q_DIM), lambda i: (i, 0)),
                  pl.BlockSpec((Q_DIM, D_MODEL), lambda i: (0, 0)),
                  pl.BlockSpec((ROW_TILE, D_MODEL), lambda i: (i, 0))],
        out_specs=pl.BlockSpec((ROW_TILE, D_MODEL), lambda i: (i, 0)),
        out_shape=jax.ShapeDtypeStruct((rows, D_MODEL), jnp.float32),
        compiler_params=pltpu.CompilerParams(dimension_semantics=("arbitrary",), vmem_limit_bytes=VMEM_LIMIT),
        name="proj_residual",
    )(a, w_b, res)


def _causal_tiles(qi, tq, tk):
    return (qi * tq + tq + tk - 1) // tk


def _mflash_body(qt_ref, k_ref, vt_ref, bt_ref, o_ref, m_ref, l_ref, acc_ref, *, r, tq, tk, blk):
    qi = pl.program_id(2)
    qt = qt_ref[0, 0, 0]
    m_ref[...] = jnp.full(m_ref.shape, NEG_INF, jnp.float32)
    l_ref[...] = jnp.zeros(l_ref.shape, jnp.float32)
    acc_ref[...] = jnp.zeros(acc_ref.shape, jnp.float32)

    def step(kt, carry):
        ks = pl.multiple_of(kt * tk, tk)
        k = k_ref[0, 0, pl.ds(ks, tk), :]
        vt = vt_ref[0, 0, :, pl.ds(ks, tk)]
        if blk:
            per_tile = tk // blk
            rows = bt_ref[0, 0, pl.ds(pl.multiple_of(kt * per_tile, per_tile), per_tile), :]
            chosen = jnp.concatenate([jnp.broadcast_to(rows[b:b + 1, :], (blk, tq)) for b in range(per_tile)], axis=0)
            key_pos = ks + lax.broadcasted_iota(jnp.int32, (tk, 1), 0)
            q_pos = qi * tq + lax.broadcasted_iota(jnp.int32, (1, tq), 1)
            bias = jnp.where((chosen > 0.5) & (key_pos <= q_pos), 0.0, NEG_INF)
        else:
            bias = bt_ref[0, 0, kt, 0].astype(jnp.float32)
        s = jnp.dot(k, qt, preferred_element_type=jnp.float32) + jnp.concatenate([bias] * r, axis=1)
        m_old = m_ref[...]
        m_new = jnp.maximum(m_old, jnp.max(s, axis=0, keepdims=True))
        p = jnp.exp(s - m_new)
        alpha = jnp.exp(m_old - m_new)
        l_ref[...] = alpha * l_ref[...] + jnp.sum(p, axis=0, keepdims=True)
        acc_ref[...] = alpha * acc_ref[...] + jnp.dot(vt, p.astype(MXU_DTYPE), preferred_element_type=jnp.float32)
        m_ref[...] = m_new
        return carry

    lax.fori_loop(0, _causal_tiles(qi, tq, tk), step, 0)
    o_ref[0, 0, 0] = acc_ref[...] / jnp.maximum(l_ref[...], 1e-30)


def masked_flash(q, k, v, allow_t=None, blk=0, bias_tiles=None):
    n, t, g, r, dh = q.shape
    gb = (allow_t if bias_tiles is None else bias_tiles).shape[1]
    tq, tk = FLASH_ROWS // r, K_TILE
    nq, nkt = t // tq, t // tk
    qs = (q * (dh ** -0.5)).reshape(n, nq, tq, g, r, dh)
    qt = jnp.transpose(qs, (0, 3, 1, 5, 4, 2)).reshape(n, g, nq, dh, r * tq).astype(MXU_DTYPE)
    kb = jnp.transpose(k, (0, 2, 1, 3)).astype(MXU_DTYPE)
    vt = jnp.transpose(v, (0, 2, 3, 1)).astype(MXU_DTYPE)
    if bias_tiles is not None:
        bt = bias_tiles
        bmap = (lambda a, b, c: (a, b, 0, c, 0, 0)) if gb == g else (lambda a, b, c: (a, 0, 0, c, 0, 0))
        bspec = pl.BlockSpec((1, 1, nkt, 1, tk, tq), bmap)
    elif blk:
        bt = allow_t.astype(jnp.float32)
        bspec = pl.BlockSpec((1, 1, t // blk, tq), lambda a, b, c: (a, b, 0, c))
    else:
        bt = jnp.where(allow_t, 0.0, NEG_INF).astype(BIAS_DTYPE).reshape(n, gb, nkt, tk, nq, tq)
        bt = jnp.transpose(bt, (0, 1, 2, 4, 3, 5))
        bmap = (lambda a, b, c: (a, b, 0, c, 0, 0)) if gb == g else (lambda a, b, c: (a, 0, 0, c, 0, 0))
        bspec = pl.BlockSpec((1, 1, nkt, 1, tk, tq), bmap)
    o = pl.pallas_call(
        functools.partial(_mflash_body, r=r, tq=tq, tk=tk, blk=blk),
        grid=(n, g, nq),
        in_specs=[
            pl.BlockSpec((1, 1, 1, dh, r * tq), lambda a, b, c: (a, b, c, 0, 0)),
            pl.BlockSpec((1, 1, t, dh), lambda a, b, c: (a, b, 0, 0)),
            pl.BlockSpec((1, 1, dh, t), lambda a, b, c: (a, b, 0, 0)),
            bspec,
        ],
        out_specs=pl.BlockSpec((1, 1, 1, dh, r * tq), lambda a, b, c: (a, b, c, 0, 0)),
        out_shape=jax.ShapeDtypeStruct((n, g, nq, dh, r * tq), jnp.float32),
        scratch_shapes=[pltpu.VMEM((1, r * tq), jnp.float32),
                        pltpu.VMEM((1, r * tq), jnp.float32),
                        pltpu.VMEM((dh, r * tq), jnp.float32)],
        compiler_params=pltpu.CompilerParams(
            dimension_semantics=("arbitrary", "arbitrary", "arbitrary"), vmem_limit_bytes=VMEM_LIMIT),
        name="masked_flash",
    )(qt, kb, vt, bt)
    o = o.reshape(n, g, nq, dh, r, tq)
    return jnp.transpose(o, (0, 2, 5, 1, 4, 3)).reshape(n, t, g, r, dh)


def _dsa_select_body(qx_ref, kx_ref, w_ref, o_ref, key_ref, *, tq, tk, top):
    qi = pl.program_id(1)
    nkt = key_ref.shape[0]
    n_valid = _causal_tiles(qi, tq, tk)
    qx = qx_ref[0].reshape(IDX_HEADS * tq, IDX_DIM)
    w = w_ref[0]
    row = qi * tq + lax.broadcasted_iota(jnp.int32, (tq, 1), 0)

    def causal(kt):
        col = kt * tk + lax.broadcasted_iota(jnp.int32, (1, tk), 1)
        return col <= row

    def score_step(kt, carry):
        ks = pl.multiple_of(kt * tk, tk)
        d = lax.dot_general(qx, kx_ref[0, pl.ds(ks, tk), :], _AT_BT, preferred_element_type=jnp.float32)
        d = jnp.maximum(d, 0.0).reshape(IDX_HEADS, tq, tk)
        sc = w[:, 0:1] * d[0]
        for h in range(1, IDX_HEADS):
            sc = sc + w[:, h:h + 1] * d[h]
        sc = jnp.where(causal(kt), sc, NEG_INF)
        bits = pltpu.bitcast(sc, jnp.int32)
        key = jnp.where(bits >= 0, bits, bits ^ 0x7FFFFFFF)
        key_ref[kt] = jnp.where(key == -1, 0, key)
        return carry

    lax.fori_loop(0, n_valid, score_step, 0)

    def count(bound, strict):
        bb = jnp.broadcast_to(bound, (tq, LANE))

        def body(kt, c):
            for b in range(tk // LANE):
                kk = key_ref[kt, :, b * LANE:(b + 1) * LANE]
                c = c + jnp.where((kk > bb) if strict else (kk >= bb), 1.0, 0.0)
            return c
        c = lax.fori_loop(0, n_valid, body, jnp.zeros((tq, LANE), jnp.float32))
        return jnp.sum(c, axis=1, keepdims=True)

    c0 = count(jnp.zeros((tq, 1), jnp.int32), False)
    tau = jnp.where(c0 >= top, 0, INT32_MIN).astype(jnp.int32)

    def bit_step(i, tau):
        cand = tau | jnp.left_shift(jnp.int32(1), 30 - i)
        return jnp.where(count(cand, False) >= top, cand, tau)

    tau = lax.fori_loop(0, 31, bit_step, tau)
    need = top - count(tau, True)
    ri = lax.broadcasted_iota(jnp.int32, (tk, tk), 0)
    ci = lax.broadcasted_iota(jnp.int32, (tk, tk), 1)
    tri = jnp.where(ri <= ci, 1.0, 0.0).astype(MXU_DTYPE)

    def out_step(kt, seen):
        kk = key_ref[kt]
        tie = kk == tau
        tie_f = jnp.where(tie, 1.0, 0.0)
        rank = seen + jnp.dot(tie_f.astype(MXU_DTYPE), tri, preferred_element_type=jnp.float32)
        sel = (kk > tau) | (tie & (rank <= need))
        o_ref[0, 0, kt, 0] = jnp.where(sel & causal(kt), 0.0, NEG_INF).T.astype(o_ref.dtype)
        return seen + jnp.sum(tie_f, axis=1, keepdims=True)

    lax.fori_loop(0, n_valid, out_step, jnp.zeros((tq, 1), jnp.float32))

    def fill_step(kt, carry):
        o_ref[0, 0, kt, 0] = jnp.full((tk, tq), NEG_INF, o_ref.dtype)
        return carry

    lax.fori_loop(n_valid, nkt, fill_step, 0)


def dsa_select_bias(q_idx, k_idx, w_idx, top, tq_out):
    n, h, t, di = q_idx.shape
    nkt = t // K_TILE
    per_out = tq_out // Q_TILE
    return pl.pallas_call(
        functools.partial(_dsa_select_body, tq=Q_TILE, tk=K_TILE, top=top),
        grid=(n, t // Q_TILE),
        in_specs=[
            pl.BlockSpec((1, h, Q_TILE, di), lambda a, c: (a, 0, c, 0)),
            pl.BlockSpec((1, t, di), lambda a, c: (a, 0, 0)),
            pl.BlockSpec((1, Q_TILE, h), lambda a, c: (a, c, 0)),
        ],
        out_specs=pl.BlockSpec((1, 1, nkt, 1, K_TILE, Q_TILE), lambda a, c: (a, 0, 0, c // per_out, 0, c % per_out)),
        out_shape=jax.ShapeDtypeStruct((n, 1, nkt, t // tq_out, K_TILE, tq_out), BIAS_DTYPE),
        scratch_shapes=[pltpu.VMEM((nkt, Q_TILE, K_TILE), jnp.int32)],
        compiler_params=pltpu.CompilerParams(
            dimension_semantics=("arbitrary", "arbitrary"), vmem_limit_bytes=VMEM_LIMIT),
        name="dsa_select",
    )(q_idx, k_idx, w_idx)


def _topk_mask_body(s_ref, o_ref, *, k):
    s0 = s_ref[...]
    lane = lax.broadcasted_iota(jnp.int32, s0.shape, 1).astype(jnp.float32)
    width = float(s0.shape[1])

    def step(_, carry):
        s, sel = carry
        m = jnp.max(s, axis=1, keepdims=True)
        first = jnp.min(jnp.where(s == m, lane, width), axis=1, keepdims=True)
        hit = lane == first
        return jnp.where(hit, REMOVED, s), jnp.where(hit, 1.0, sel)

    _, sel = lax.fori_loop(0, k, step, (s0, jnp.zeros(s0.shape, jnp.float32)))
    o_ref[...] = sel


def topk_mask(score, k):
    lead, n = score.shape[:-1], score.shape[-1]
    rows = int(np.prod(lead))
    width = -(-n // LANE) * LANE
    s2 = jnp.pad(score.reshape(rows, n), ((0, 0), (0, width - n)), constant_values=REMOVED)
    tm = min(TOPK_ROWS, rows)
    sel = pl.pallas_call(
        functools.partial(_topk_mask_body, k=k),
        grid=(rows // tm,),
        in_specs=[pl.BlockSpec((tm, width), lambda i: (i, 0))],
        out_specs=pl.BlockSpec((tm, width), lambda i: (i, 0)),
        out_shape=jax.ShapeDtypeStruct((rows, width), jnp.float32),
        compiler_params=pltpu.CompilerParams(dimension_semantics=("arbitrary",)),
        name="topk_mask",
    )(s2)
    return (sel[:, :n] > 0.5).reshape(lead + (n,))


def _nsa_select_mask(p_cmp, qpos, n_slc):
    n, t, g = p_cmp.shape[:3]
    ratio = NSA_SEL_BLOCK // NSA_CMP_BLOCK
    imp = p_cmp.sum(axis=3)
    imp = jnp.pad(imp, ((0, 0), (0, 0), (0, 0), (0, n_slc * ratio - imp.shape[-1])))
    imp = imp.reshape(n, t, g, n_slc, ratio).sum(-1)
    j = jnp.arange(n_slc)[None, :]
    cur = (qpos // NSA_SEL_BLOCK)[:, None]
    forced = ((j == 0) | (j == cur) | (j == cur - 1))[None, :, None, :]
    future = (j > cur)[None, :, None, :]
    score = jnp.where(future, NEG_INF, jnp.where(forced, imp + NSA_FORCE_BONUS, imp))
    return topk_mask(score, min(NSA_N_SEL, n_slc))


def _slab_query(q_ref, h, r, lane_lo):
    x = q_ref[0, :, (h // 2) * LANE:(h // 2 + 1) * LANE]
    half = (h // r) % 2
    if h % 2 != half:
        x = pltpu.roll(x, HEAD_DIM, axis=1)
    keep = lane_lo if half == 0 else jnp.logical_not(lane_lo)
    return jnp.where(keep, x, 0.0).astype(MXU_DTYPE)


def _pack_head_pair(o_a, o_b, h_a, r, lane_lo):
    half = (h_a // r) % 2
    a = o_a if half == 0 else pltpu.roll(o_a, HEAD_DIM, axis=1)
    b = o_b if half == 1 else pltpu.roll(o_b, HEAD_DIM, axis=1)
    return jnp.where(lane_lo, a, b)


def _nsa_cmp_body(q_ref, ck_ref, cv_ref, o_ref, blk_ref, *, tq, r, n_sel):
    qi = pl.program_id(1)
    nb = ck_ref.shape[1]
    n_slc = nb // 2
    t = qi * tq + lax.broadcasted_iota(jnp.int32, (tq, 1), 0)
    c = lax.broadcasted_iota(jnp.int32, (1, nb), 1)
    blk_id = jnp.where(c < n_slc, 2 * c, 2 * (c - n_slc) + 1)
    visible = (blk_id + 1) * NSA_CMP_BLOCK - 1 <= t
    ck, cv = ck_ref[0], cv_ref[0]
    imps = [jnp.zeros((tq, nb), jnp.float32) for _ in range(N_HEADS // r)]
    lane_lo = lax.broadcasted_iota(jnp.int32, (tq, LANE), 1) < HEAD_DIM
    pair = []
    for h in range(N_HEADS):
        s = lax.dot_general(_slab_query(q_ref, h, r, lane_lo), ck, _AT_BT, preferred_element_type=jnp.float32)
        s = jnp.where(visible, s, NEG_INF)
        m = jnp.max(s, axis=1, keepdims=True)
        e = jnp.where(visible, jnp.exp(s - m), 0.0)
        p = e / jnp.maximum(jnp.sum(e, axis=1, keepdims=True), 1e-30)
        pair.append(jnp.dot(p.astype(MXU_DTYPE), cv, preferred_element_type=jnp.float32))
        imps[h // r] = imps[h // r] + p
        if h % 2 == 1:
            o_ref[0, :, (h // 2) * LANE:(h // 2 + 1) * LANE] = _pack_head_pair(pair[0], pair[1], h - 1, r, lane_lo)
            pair = []
    j = lax.broadcasted_iota(jnp.int32, (1, n_slc), 1)
    lane = j.astype(jnp.float32)
    cur = t // NSA_SEL_BLOCK
    forced = (j == 0) | (j == cur) | (j == cur - 1)
    future = j > cur
    scores = []
    for imp2 in imps:
        imp = imp2[:, :n_slc] + imp2[:, n_slc:]
        scores.append(jnp.where(future, NEG_INF, jnp.where(forced, imp + NSA_FORCE_BONUS, imp)))
    score = jnp.concatenate(scores, axis=0)

    def step(_, carry):
        s, sel = carry
        m = jnp.max(s, axis=1, keepdims=True)
        first = jnp.min(jnp.where(s == m, lane, float(n_slc)), axis=1, keepdims=True)
        hit = lane == first
        return jnp.where(hit, REMOVED, s), jnp.where(hit, 1.0, sel)

    _, sel = lax.fori_loop(0, n_sel, step, (score, jnp.zeros(score.shape, jnp.float32)))
    for gi in range(len(imps)):
        blk_ref[0, :, gi * n_slc:(gi + 1) * n_slc] = sel[gi * tq:(gi + 1) * tq]


def nsa_cmp_select(q, ck, cv):
    n, t, g, r, dh = q.shape
    n_slc = -(-t // NSA_SEL_BLOCK)
    n_slc_pad = -(-n_slc // LANE) * LANE
    nb = ck.shape[1]

    def arrange(x):
        x = jnp.pad(x.reshape(n, nb, g * dh), ((0, 0), (0, 2 * n_slc_pad - nb), (0, 0)))
        return jnp.concatenate([x[:, 0::2], x[:, 1::2]], axis=1).astype(MXU_DTYPE)

    qw = g * r * dh
    o, blk = pl.pallas_call(
        functools.partial(_nsa_cmp_body, tq=Q_TILE, r=r, n_sel=min(NSA_N_SEL, n_slc)),
        grid=(n, t // Q_TILE),
        in_specs=[pl.BlockSpec((1, Q_TILE, qw), lambda a, c: (a, c, 0)),
                  pl.BlockSpec((1, 2 * n_slc_pad, g * dh), lambda a, c: (a, 0, 0)),
                  pl.BlockSpec((1, 2 * n_slc_pad, g * dh), lambda a, c: (a, 0, 0))],
        out_specs=[pl.BlockSpec((1, Q_TILE, qw), lambda a, c: (a, c, 0)),
                   pl.BlockSpec((1, Q_TILE, g * n_slc_pad), lambda a, c: (a, c, 0))],
        out_shape=[jax.ShapeDtypeStruct((n, t, qw), jnp.float32),
                   jax.ShapeDtypeStruct((n, t, g * n_slc_pad), jnp.float32)],
        compiler_params=pltpu.CompilerParams(
            dimension_semantics=("arbitrary", "arbitrary"), vmem_limit_bytes=VMEM_LIMIT),
        name="nsa_cmp_select",
    )((q * (dh ** -0.5)).reshape(n, t, qw), arrange(ck), arrange(cv))
    return o.reshape(n, t, g, r, dh), blk.reshape(n, t, g, n_slc_pad)[..., :n_slc] > 0.5


def _compress_body(*refs, n_pages):
    if n_pages:
        pages, (w_ref, o_ref, x_ref) = refs[1:1 + n_pages], refs[1 + n_pages:]
        for i in range(n_pages):
            for pi in range(x_ref.shape[0]):
                x_ref[pi, i * PAGE_SIZE:(i + 1) * PAGE_SIZE, :] = pages[i][pi * LANE:(pi + 1) * LANE, :].T
    else:
        in_ref, w_ref, o_ref, x_ref = refs
        for pi in range(x_ref.shape[0]):
            x_ref[pi] = in_ref[0, :, pi * LANE:(pi + 1) * LANE]
    n_planes = x_ref.shape[0]
    nb = o_ref.shape[1]
    acc = jnp.zeros(o_ref.shape[1:], jnp.float32)
    for c in range(NSA_CMP_BLOCK):
        rows_c = jnp.concatenate([x_ref[pi, pl.ds(c, nb, stride=NSA_CMP_BLOCK), :] for pi in range(n_planes)],
                                 axis=1).astype(MXU_DTYPE)
        acc = acc + jnp.dot(rows_c, w_ref[c], preferred_element_type=jnp.float32)
    o_ref[0] = acc


def _compress_weights(w_ck, w_cv):
    w2 = jnp.stack([w_ck, w_cv])
    g = w_ck.shape[0]
    eye = jnp.eye(2 * g, dtype=w2.dtype).reshape(2, g, 2, g)
    wbd = jnp.einsum('sgcde,sgtk->csgdtke', w2, eye)
    width = 2 * g * HEAD_DIM
    return wbd.reshape(NSA_CMP_BLOCK, width, width).astype(MXU_DTYPE)


def _split_summaries(out, g):
    n, nb, _ = out.shape
    half = g * HEAD_DIM
    return out[..., :half].reshape(n, nb, g, HEAD_DIM), out[..., half:].reshape(n, nb, g, HEAD_DIM)


def nsa_compress_rows(kv_rows, w_ck, w_cv):
    n, length, _ = kv_rows.shape
    g = w_ck.shape[0]
    width = 2 * g * HEAD_DIM
    nb = length // NSA_CMP_BLOCK
    out = pl.pallas_call(
        functools.partial(_compress_body, n_pages=0),
        grid=(n,),
        in_specs=[pl.BlockSpec((1, length, width), lambda a: (a, 0, 0)),
                  pl.BlockSpec((NSA_CMP_BLOCK, width, width), lambda a: (0, 0, 0))],
        out_specs=pl.BlockSpec((1, nb, width), lambda a: (a, 0, 0)),
        out_shape=jax.ShapeDtypeStruct((n, nb, width), jnp.float32),
        scratch_shapes=[pltpu.VMEM((width // LANE, length, LANE), jnp.float32)],
        compiler_params=pltpu.CompilerParams(dimension_semantics=("arbitrary",), vmem_limit_bytes=VMEM_LIMIT),
        name="nsa_compress_rows",
    )(kv_rows, _compress_weights(w_ck, w_cv))
    return _split_summaries(out, g)


def nsa_compress_paged(pool, layer, page_table, w_ck, w_cv):
    n, n_pages = page_table.shape
    g = w_ck.shape[0]
    width = 2 * g * HEAD_DIM
    nb = n_pages * PAGE_SIZE // NSA_CMP_BLOCK
    pool_flat = _feature_major_pages(pool)
    grid_spec = pltpu.PrefetchScalarGridSpec(
        num_scalar_prefetch=1,
        grid=(n,),
        in_specs=[pl.BlockSpec((None, None, width, PAGE_SIZE),
                               functools.partial(lambda a, pt, i: (layer, pt[a * n_pages + i], 0, 0), i=i))
                  for i in range(n_pages)]
                 + [pl.BlockSpec((NSA_CMP_BLOCK, width, width), lambda a, pt: (0, 0, 0))],
        out_specs=pl.BlockSpec((1, nb, width), lambda a, pt: (a, 0, 0)),
        scratch_shapes=[pltpu.VMEM((width // LANE, n_pages * PAGE_SIZE, LANE), jnp.float32)],
    )
    out = pl.pallas_call(
        functools.partial(_compress_body, n_pages=n_pages),
        grid_spec=grid_spec,
        out_shape=jax.ShapeDtypeStruct((n, nb, width), jnp.float32),
        compiler_params=pltpu.CompilerParams(dimension_semantics=("arbitrary",), vmem_limit_bytes=VMEM_LIMIT),
        name="nsa_compress_paged",
    )(page_table.reshape(-1), *([pool_flat] * n_pages), _compress_weights(w_ck, w_cv))
    return _split_summaries(out, g)


def _band_body(*refs, tq, n_prev, span, r, has_sink):
    nk = n_prev + 1
    q_ref, k_refs, v_refs = refs[0], refs[1:1 + nk], refs[1 + nk:1 + 2 * nk]
    sink_ref = refs[1 + 2 * nk] if has_sink else None
    o_ref, lse_ref = refs[-2:]
    ui = pl.program_id(2)
    i = lax.broadcasted_iota(jnp.int32, (tq, tq), 0)
    j = lax.broadcasted_iota(jnp.int32, (tq, tq), 1)
    parts = []
    for p in range(n_prev, -1, -1):
        diff = p * tq + i - j
        parts.append((diff >= 0) & (diff < span) & (ui >= p))
    allow = jnp.concatenate(parts, axis=1)
    lane = lax.broadcasted_iota(jnp.int32, (tq, LANE), 1)
    lse_tile = jnp.zeros((tq, LANE), jnp.float32)
    lane_lo = lane < HEAD_DIM
    pair = []
    slabs = {}
    for h in range(N_HEADS):
        gp = (h // r) // 2
        if gp not in slabs:
            sl = slice(gp * LANE, (gp + 1) * LANE)
            slabs[gp] = (jnp.concatenate([kr[0, :, sl] for kr in k_refs], axis=0),
                         jnp.concatenate([vr[0, :, sl] for vr in v_refs], axis=0))
        k2, v2 = slabs[gp]
        s = lax.dot_general(_slab_query(q_ref, h, r, lane_lo), k2, _AT_BT, preferred_element_type=jnp.float32)
        s = jnp.where(allow, s, NEG_INF)
        m = jnp.max(s, axis=1, keepdims=True)
        if has_sink:
            m = jnp.maximum(m, sink_ref[h])
        e = jnp.exp(s - m)
        den = jnp.sum(e, axis=1, keepdims=True)
        if has_sink:
            den = den + jnp.exp(sink_ref[h] - m)
        den = jnp.maximum(den, 1e-30)
        pair.append(jnp.dot(e.astype(MXU_DTYPE), v2, preferred_element_type=jnp.float32) / den)
        if h % 2 == 1:
            o_ref[0, :, (h // 2) * LANE:(h // 2 + 1) * LANE] = _pack_head_pair(pair[0], pair[1], h - 1, r, lane_lo)
            pair = []
        lse_tile = jnp.where(lane == h, m + jnp.log(den), lse_tile)
    lse_ref[0] = lse_tile


def band_attn(q, k, v, dil, span, sink=None):
    n, t, g, r, dh = q.shape
    tq = Q_TILE
    n_prev = -(-(span - 1) // tq)
    assert r % 2 == 0 and g * r == N_HEADS
    qf = (q * (dh ** -0.5)).reshape(n, t, g * r * dh)
    kb = k.reshape(n, t, g * dh).astype(MXU_DTYPE)
    vb = v.reshape(n, t, g * dh).astype(MXU_DTYPE)
    qw, kw = g * r * dh, g * dh
    length = t // dil
    view = lambda x: x.reshape(n, length, dil * x.shape[-1])
    cur = lambda a, c, u: (a, u, c)
    back = [functools.partial(lambda a, c, u, p: (a, jnp.maximum(u - p, 0), c), p=p) for p in range(n_prev, -1, -1)]
    in_specs = ([pl.BlockSpec((1, tq, qw), cur)] + [pl.BlockSpec((1, tq, kw), b) for b in back] * 2)
    operands = [view(qf)] + [view(kb)] * (n_prev + 1) + [view(vb)] * (n_prev + 1)
    if sink is not None:
        in_specs.append(pl.BlockSpec(memory_space=pltpu.SMEM))
        operands.append(sink.astype(jnp.float32))
    o, lse = pl.pallas_call(
        functools.partial(_band_body, tq=tq, n_prev=n_prev, span=span, r=r, has_sink=sink is not None),
        grid=(n, dil, length // tq),
        in_specs=in_specs,
        out_specs=[pl.BlockSpec((1, tq, qw), cur), pl.BlockSpec((1, tq, LANE), cur)],
        out_shape=[jax.ShapeDtypeStruct((n, length, dil * qw), jnp.float32),
                   jax.ShapeDtypeStruct((n, length, dil * LANE), jnp.float32)],
        compiler_params=pltpu.CompilerParams(
            dimension_semantics=("arbitrary", "arbitrary", "arbitrary"), vmem_limit_bytes=VMEM_LIMIT),
        name="band_attn",
    )(*operands)
    return o.reshape(n, t, g, r, dh), lse.reshape(n, t, LANE)[:, :, :N_HEADS].reshape(n, t, g, r)


def nsa_prompt_p(z, w_ck, w_cv):
    n, s, _ = z.shape
    pos = jnp.arange(s)
    q, kv, gates = _nsa_project(z, pos)
    ck, cv = nsa_compress_rows(kv.reshape(n, s, -1), w_ck, w_cv)
    o_cmp, blk = nsa_cmp_select(q, ck, cv)
    o_slc = masked_flash(q, kv[:, :, 2], kv[:, :, 3], jnp.transpose(blk, (0, 2, 3, 1)), NSA_SEL_BLOCK)
    o_win, _ = band_attn(q, kv[:, :, 4], kv[:, :, 5], 1, NSA_WINDOW)
    return _nsa_gate(gates, o_cmp, o_slc, o_win), kv[:, :, 0:4], _tail(kv[:, :, 4:6], NSA_WINDOW)


def dsa_prompt_p(z):
    n, s, _ = z.shape
    pos = jnp.arange(s)
    q, kv, q_idx, k_idx, w_idx = _dsa_project(z, pos)
    top = min(DSA_TOPK, s // 4)
    bias = dsa_select_bias(jnp.transpose(q_idx, (0, 2, 1, 3)).astype(MXU_DTYPE),
                           k_idx.astype(MXU_DTYPE), w_idx, top, FLASH_ROWS // q.shape[3])
    o = masked_flash(q, kv[:, :, 0], kv[:, :, 1], bias_tiles=bias)
    return o.reshape(n, s, Q_DIM), kv, k_idx


def dil_prompt_p(z):
    n, s, _ = z.shape
    q, kv = _dil_project(z, jnp.arange(s))
    outs, lses, bufs = [], [], []
    for gi, (win, dil) in enumerate(DIL_GROUPS):
        o, lse = band_attn(q, kv[:, :, gi, 0], kv[:, :, gi, 1], dil, win // dil + 1)
        outs.append(o)
        lses.append(lse)
        bufs.append(_tail(kv[:, :, gi], win))
    return _dil_mix(outs, lses).astype(z.dtype).reshape(n, s, Q_DIM), bufs


def swa_prompt_p(z, sink):
    n, s, _ = z.shape
    q, kv = _swa_project(z, jnp.arange(s))
    o, _ = band_attn(q, kv[:, :, 0], kv[:, :, 1], 1, SWA_WINDOW, sink)
    return o.reshape(n, s, Q_DIM), _tail(kv, SWA_WINDOW)


def _flash_update(s, pv, m_ref, l_ref, acc_ref):
    m_old = m_ref[...]
    m_new = jnp.maximum(m_old, jnp.max(s, axis=1, keepdims=True))
    p = jnp.exp(s - m_new)
    alpha = jnp.exp(m_old - m_new)
    l_ref[...] = alpha * l_ref[...] + jnp.sum(p, axis=1, keepdims=True)
    acc_ref[...] = alpha * acc_ref[...] + pv(p.astype(MXU_DTYPE))
    m_ref[...] = m_new


def _paged_attn_body(pt_ref, q_ref, bn_ref, kn_ref, vn_ref, b_ref, *rest, ppc):
    pages = rest[:ppc]
    o_ref, m_ref, l_ref, acc_ref = rest[ppc:]
    lw = q_ref.shape[2]
    c = pl.program_id(1)
    q = q_ref[0]

    @pl.when(c == 0)
    def _():
        m_ref[...] = jnp.full(m_ref.shape, NEG_INF, jnp.float32)
        l_ref[...] = jnp.zeros(l_ref.shape, jnp.float32)
        acc_ref[...] = jnp.zeros(acc_ref.shape, jnp.float32)
        s = lax.dot_general(q, kn_ref[0], _AT_BT, preferred_element_type=jnp.float32) + bn_ref[0].astype(jnp.float32)
        _flash_update(s, lambda p: jnp.dot(p, vn_ref[0], preferred_element_type=jnp.float32), m_ref, l_ref, acc_ref)

    parts = [jnp.dot(q, pages[i][0:lw, :].astype(MXU_DTYPE), preferred_element_type=jnp.float32) for i in range(ppc)]
    s = jnp.concatenate(parts, axis=1) + b_ref[0].astype(jnp.float32)

    def pv(p):
        acc = lax.dot_general(p[:, 0:PAGE_SIZE], pages[0][lw:2 * lw, :].astype(MXU_DTYPE), _AT_BT,
                              preferred_element_type=jnp.float32)
        for i in range(1, ppc):
            acc = acc + lax.dot_general(p[:, i * PAGE_SIZE:(i + 1) * PAGE_SIZE],
                                        pages[i][lw:2 * lw, :].astype(MXU_DTYPE), _AT_BT,
                                        preferred_element_type=jnp.float32)
        return acc

    _flash_update(s, pv, m_ref, l_ref, acc_ref)

    @pl.when(c == pl.num_programs(1) - 1)
    def _():
        o_ref[0] = acc_ref[...] / jnp.maximum(l_ref[...], 1e-30)


def paged_attn(q2, pool, layer, page_table, kv_blk, bias, k_new, v_new, bias_new):
    n, rows, lw = q2.shape
    n_pages = page_table.shape[1]
    ppc = min(PAGES_PER_CHUNK, n_pages)
    n_chunks = n_pages // ppc

    def page_spec(i):
        return pl.BlockSpec((None, None, 2 * lw, PAGE_SIZE),
                            lambda a, c, pt: (layer, pt[a * n_pages + c * ppc + i], kv_blk, 0))

    per_seq = lambda a, c, pt: (a, 0, 0)
    grid_spec = pltpu.PrefetchScalarGridSpec(
        num_scalar_prefetch=1,
        grid=(n, n_chunks),
        in_specs=[pl.BlockSpec((1, rows, lw), per_seq),
                  pl.BlockSpec((1, rows, PAGE_SIZE), per_seq),
                  pl.BlockSpec((1, PAGE_SIZE, lw), per_seq),
                  pl.BlockSpec((1, PAGE_SIZE, lw), per_seq),
                  pl.BlockSpec((1, rows, ppc * PAGE_SIZE), lambda a, c, pt: (a, 0, c))]
                 + [page_spec(i) for i in range(ppc)],
        out_specs=pl.BlockSpec((1, rows, lw), per_seq),
        scratch_shapes=[pltpu.VMEM((rows, 1), jnp.float32),
                        pltpu.VMEM((rows, 1), jnp.float32),
                        pltpu.VMEM((rows, lw), jnp.float32)],
    )
    return pl.pallas_call(
        functools.partial(_paged_attn_body, ppc=ppc),
        grid_spec=grid_spec,
        out_shape=jax.ShapeDtypeStruct((n, rows, lw), jnp.float32),
        compiler_params=pltpu.CompilerParams(
            dimension_semantics=("arbitrary", "arbitrary"), vmem_limit_bytes=VMEM_LIMIT),
        name="paged_attn",
    )(page_table.reshape(-1), q2, bias_new, k_new, v_new, bias, *([pool] * ppc))


def _feature_major_pages(pool):
    nd = pool.ndim
    t = jnp.transpose(pool, (0, 1) + tuple(range(3, nd)) + (2,))
    return t.reshape(pool.shape[0], pool.shape[1], -1, pool.shape[2])


def _group_lane_queries(q):
    n, t, g, r, dh = q.shape
    qs = jnp.transpose(q * (dh ** -0.5), (0, 2, 1, 3, 4))
    onehot = jnp.eye(g, dtype=qs.dtype)
    q2 = qs[:, :, :, :, None, :] * onehot[None, :, None, None, :, None]
    return q2.reshape(n, g * t * r, g * dh).astype(MXU_DTYPE)


def _ungroup_lanes(o, t, g, r, dh):
    n = o.shape[0]
    o6 = o.reshape(n, g, t, r, g, dh)
    return jnp.stack([o6[:, gi, :, :, gi] for gi in range(g)], axis=2)


def _rows_bias(allow, g, r):
    n, t, ga, k = allow.shape
    a = jnp.broadcast_to(jnp.transpose(allow, (0, 2, 1, 3))[:, :, :, None, :], (n, ga, t, r, k))
    if ga != g:
        a = jnp.broadcast_to(a, (n, g, t, r, k))
    return jnp.where(a, 0.0, NEG_INF).astype(BIAS_DTYPE).reshape(n, g * t * r, k)


def _new_rows_page(x, lw):
    n, t, _ = x.shape
    return jnp.pad(x, ((0, 0), (0, PAGE_SIZE - t), (0, 0))).astype(MXU_DTYPE)


def nsa_sample_p(z, pool, layer, win_buf, page_table, w_ck, w_cv, past):
    n, t, _ = z.shape
    g, r = NSA_KV_HEADS, N_HEADS // NSA_KV_HEADS
    pos = past + jnp.arange(t)
    q, kv, gates = _nsa_project(z, pos)
    ck_p, cv_p = nsa_compress_paged(pool, layer, page_table, w_ck, w_cv)
    ck_n, cv_n = _nsa_compress(kv[:, :, 0:2], w_ck, w_cv)
    o_cmp, p_cmp = _nsa_cmp_attend(q, jnp.concatenate([ck_p, ck_n], 1),
                                   jnp.concatenate([cv_p, cv_n], 1), pos)
    n_slc = -(-(past + t) // NSA_SEL_BLOCK)
    blk = _nsa_select_mask(p_cmp, pos, n_slc)
    n_past_blk = past // NSA_SEL_BLOCK
    allow_past = jnp.repeat(blk[..., :n_past_blk], NSA_SEL_BLOCK, axis=-1)
    j = jnp.arange(PAGE_SIZE)
    new_blk = (past + j) // NSA_SEL_BLOCK
    allow_new = (jnp.take(blk, jnp.minimum(new_blk, n_slc - 1), axis=-1)
                 & (j[None, :] <= jnp.arange(t)[:, None])[None, :, None, :] & (j < t))
    lw = g * HEAD_DIM
    pool_flat = _feature_major_pages(pool)
    o = paged_attn(_group_lane_queries(q), pool_flat, layer, page_table, 1,
                   _rows_bias(allow_past, g, r),
                   _new_rows_page(kv[:, :, 2].reshape(n, t, lw), lw),
                   _new_rows_page(kv[:, :, 3].reshape(n, t, lw), lw),
                   _rows_bias(allow_new, g, r))
    o_slc = _ungroup_lanes(o, t, g, r, HEAD_DIM)
    keys = jnp.concatenate([win_buf, kv[:, :, 4:6]], axis=1)
    kp = past - win_buf.shape[1] + jnp.arange(keys.shape[1])
    diff = pos[:, None] - kp[None, :]
    o_win, _ = _attend_dense(q, keys[:, :, 0], keys[:, :, 1], ((diff >= 0) & (diff < NSA_WINDOW))[None])
    return _nsa_gate(gates, o_cmp, o_slc, o_win), kv[:, :, 0:4], _tail(keys, NSA_WINDOW)


def _dsa_sample_select_body(pt_ref, qx_ref, w_ref, kn_ref, *rest, n_pages, t_new, top):
    pages = rest[:n_pages]
    o_ref, key_ref = rest[n_pages:]
    qx = qx_ref[0]
    w = w_ref[0]
    row = lax.broadcasted_iota(jnp.int32, (8, 1), 0)
    col = lax.broadcasted_iota(jnp.int32, (1, PAGE_SIZE), 1)
    new_ok = (col <= row) & (col < t_new)

    for i in range(n_pages + 1):
        if i == n_pages:
            d = lax.dot_general(qx, kn_ref[0], _AT_BT, preferred_element_type=jnp.float32)
        else:
            d = jnp.dot(qx, pages[i][...].astype(MXU_DTYPE), preferred_element_type=jnp.float32)
        d = jnp.maximum(d, 0.0).reshape(IDX_HEADS, 8, PAGE_SIZE)
        sc = w[:, 0:1] * d[0]
        for hh in range(1, IDX_HEADS):
            sc = sc + w[:, hh:hh + 1] * d[hh]
        if i == n_pages:
            sc = jnp.where(new_ok, sc, NEG_INF)
        bits = pltpu.bitcast(sc, jnp.int32)
        key = jnp.where(bits >= 0, bits, bits ^ 0x7FFFFFFF)
        key_ref[i] = jnp.where(key == -1, 0, key)

    def count(pred):
        hit = jnp.where(pred(key_ref[...]), 1.0, 0.0)
        return jnp.sum(jnp.sum(hit, axis=0), axis=1, keepdims=True)

    c0 = count(lambda kk: kk >= 0)
    tau = jnp.where(c0 >= top, 0, INT32_MIN).astype(jnp.int32)

    def bit_step(i, tau):
        cand = tau | jnp.left_shift(jnp.int32(1), 30 - i)
        return jnp.where(count(lambda kk: kk >= cand) >= top, cand, tau)

    tau = lax.fori_loop(0, 31, bit_step, tau)
    need = top - count(lambda kk: kk > tau)
    ri = lax.broadcasted_iota(jnp.int32, (PAGE_SIZE, PAGE_SIZE), 0)
    ci = lax.broadcasted_iota(jnp.int32, (PAGE_SIZE, PAGE_SIZE), 1)
    tri = jnp.where(ri <= ci, 1.0, 0.0).astype(MXU_DTYPE)
    seen = jnp.zeros((8, 1), jnp.float32)
    for i in range(n_pages + 1):
        kk = key_ref[i]
        tie = kk == tau
        tie_f = jnp.where(tie, 1.0, 0.0)
        rank = seen + jnp.dot(tie_f.astype(MXU_DTYPE), tri, preferred_element_type=jnp.float32)
        sel = (kk > tau) | (tie & (rank <= need))
        if i == n_pages:
            sel = sel & new_ok
        o_ref[0, :, i * PAGE_SIZE:(i + 1) * PAGE_SIZE] = jnp.where(sel, 0.0, NEG_INF)
        seen = seen + jnp.sum(tie_f, axis=1, keepdims=True)


def dsa_sample_select(q_idx, k_idx_new, w_idx, pool_idx, layer, page_table, top):
    n, t, hh, di = q_idx.shape
    n_pages = page_table.shape[1]
    qx = jnp.pad(jnp.transpose(q_idx, (0, 2, 1, 3)), ((0, 0), (0, 0), (0, 8 - t), (0, 0)))
    qx = qx.reshape(n, hh * 8, di).astype(MXU_DTYPE)
    w8 = jnp.pad(w_idx, ((0, 0), (0, 8 - t), (0, 0)))
    kn = _new_rows_page(k_idx_new, di)
    per_seq = lambda a, pt: (a, 0, 0)
    total = (n_pages + 1) * PAGE_SIZE
    grid_spec = pltpu.PrefetchScalarGridSpec(
        num_scalar_prefetch=1,
        grid=(n,),
        in_specs=[pl.BlockSpec((1, hh * 8, di), per_seq),
                  pl.BlockSpec((1, 8, hh), per_seq),
                  pl.BlockSpec((1, PAGE_SIZE, di), per_seq)]
                 + [pl.BlockSpec((None, None, di, PAGE_SIZE),
                                 functools.partial(lambda a, pt, i: (layer, pt[a * n_pages + i], 0, 0), i=i))
                    for i in range(n_pages)],
        out_specs=pl.BlockSpec((1, 8, total), per_seq),
        scratch_shapes=[pltpu.VMEM((n_pages + 1, 8, PAGE_SIZE), jnp.int32)],
    )
    return pl.pallas_call(
        functools.partial(_dsa_sample_select_body, n_pages=n_pages, t_new=t, top=top),
        grid_spec=grid_spec,
        out_shape=jax.ShapeDtypeStruct((n, 8, total), jnp.float32),
        compiler_params=pltpu.CompilerParams(
            dimension_semantics=("arbitrary",), vmem_limit_bytes=VMEM_LIMIT),
        name="dsa_sample_select",
    )(page_table.reshape(-1), qx, w8, kn, *([_feature_major_pages(pool_idx)] * n_pages))


def dsa_sample_p(z, pool_kv, pool_idx, layer, page_table, past):
    n, t, _ = z.shape
    g, r = DSA_KV_HEADS, N_HEADS // DSA_KV_HEADS
    pos = past + jnp.arange(t)
    q, kv, q_idx, k_idx, w_idx = _dsa_project(z, pos)
    total = past + t
    bias8 = dsa_sample_select(q_idx, k_idx, w_idx, pool_idx, layer, page_table, min(DSA_TOPK, total // 4))
    allow = (bias8[:, :t] == 0.0)[:, :, None, :]
    lw = g * HEAD_DIM
    pool_flat = _feature_major_pages(pool_kv)
    o = paged_attn(_group_lane_queries(q), pool_flat, layer, page_table, 0,
                   _rows_bias(allow[..., :past], g, r),
                   _new_rows_page(kv[:, :, 0].reshape(n, t, lw), lw),
                   _new_rows_page(kv[:, :, 1].reshape(n, t, lw), lw),
                   _rows_bias(allow[..., past:], g, r))
    return _ungroup_lanes(o, t, g, r, HEAD_DIM).reshape(n, t, Q_DIM), kv, k_idx


def dil_sample(z, bufs, past):
    n, t, _ = z.shape
    pos = past + jnp.arange(t)
    q, kv = _dil_project(z, pos)
    outs, lses, new_bufs = [], [], []
    for gi, ((win, dil), buf) in enumerate(zip(DIL_GROUPS, bufs)):
        keys = jnp.concatenate([buf, kv[:, :, gi]], axis=1)
        base = past - buf.shape[1]
        local = pos[:, None] - (jnp.arange(win // dil + 1) * dil)[None, :] - base
        lc = jnp.maximum(local, 0)
        o, lse = _attend_gathered(q, keys[:, lc, 0], keys[:, lc, 1], (local >= 0)[None, :, :, None])
        outs.append(o)
        lses.append(lse)
        new_bufs.append(_tail(keys, win))
    return _dil_mix(outs, lses).astype(z.dtype).reshape(n, t, Q_DIM), new_bufs


def swa_sample(z, buf, sink, past):
    n, t, _ = z.shape
    pos = past + jnp.arange(t)
    q, kv = _swa_project(z, pos)
    keys = jnp.concatenate([buf, kv], axis=1)
    kp = past - buf.shape[1] + jnp.arange(keys.shape[1])
    diff = pos[:, None] - kp[None, :]
    o, _ = _attend_dense(q, keys[:, :, 0], keys[:, :, 1], ((diff >= 0) & (diff < SWA_WINDOW))[None],
                         sink.reshape(SWA_KV_HEADS, -1))
    return o.reshape(n, t, Q_DIM), _tail(keys, SWA_WINDOW)


def _pad_cols(w):
    n_out = w.shape[1]
    return jnp.pad(w, ((0, 0), (0, -n_out % LANE))).astype(MXU_DTYPE)


def _project(h, g, w_in):
    n, t, _ = h.shape
    z = norm_proj(h.reshape(n * t, D_MODEL), g, _pad_cols(w_in))
    return z[:, :w_in.shape[1]].reshape(n, t, w_in.shape[1])


def _add_out_proj(h, o, w_out):
    n, t, _ = h.shape
    return proj_residual(o.reshape(n * t, Q_DIM), w_out.astype(MXU_DTYPE), h.reshape(n * t, D_MODEL)).reshape(h.shape)


def _add_ffn(h, g, w_in_b, w_out_b):
    return ffn_residual(h.reshape(-1, D_MODEL), g, w_in_b, w_out_b).reshape(h.shape)


def kernel(x_prompt, x_sample, cache_nsa_kv, cache_nsa_win, cache_dil1, cache_dil2, cache_dil3,
           cache_dsa_kv, cache_dsa_idx, cache_swa, page_table,
           norm_mix, norm_ffn, norm_final, ffn_in, ffn_out,
           nsa_w_in, nsa_w_cmp_k, nsa_w_cmp_v, nsa_w_out,
           dil_w_in, dil_w_out, dsa_w_in, dsa_w_out,
           swa_w_in, swa_sink, swa_w_out):
    past = page_table.shape[1] * PAGE_SIZE
    hp, hs = x_prompt, x_sample
    ffn_in_b = ffn_in.astype(MXU_DTYPE)
    ffn_out_b = ffn_out.astype(MXU_DTYPE)
    st = {name: [] for name in ("nsa_kv_p", "nsa_kv_s", "nsa_win_p", "nsa_win_s",
                                "dil1_p", "dil1_s", "dil2_p", "dil2_s", "dil3_p", "dil3_s",
                                "dsa_kv_p", "dsa_kv_s", "dsa_idx_p", "dsa_idx_s", "swa_p", "swa_s")}
    for i in range(DEPTH):
        kind, j = i % N_MIXERS, i // N_MIXERS
        w_in, w_out = ((nsa_w_in, nsa_w_out), (dil_w_in, dil_w_out), (dsa_w_in, dsa_w_out), (swa_w_in, swa_w_out))[kind]
        zp, zs = _project(hp, norm_mix[i], w_in[j]), _project(hs, norm_mix[i], w_in[j])
        if kind == 0:
            op, kv_p, win_p = nsa_prompt_p(zp, nsa_w_cmp_k[j], nsa_w_cmp_v[j])
            os_, kv_s, win_s = nsa_sample_p(zs, cache_nsa_kv, j, cache_nsa_win[j], page_table,
                                            nsa_w_cmp_k[j], nsa_w_cmp_v[j], past)
            st["nsa_kv_p"].append(kv_p)
            st["nsa_kv_s"].append(kv_s)
            st["nsa_win_p"].append(win_p)
            st["nsa_win_s"].append(win_s)
        elif kind == 1:
            op, bufs_p = dil_prompt_p(zp)
            os_, bufs_s = dil_sample(zs, [cache_dil1[j], cache_dil2[j], cache_dil3[j]], past)
            for gi in range(len(DIL_GROUPS)):
                st["dil%d_p" % (gi + 1)].append(bufs_p[gi])
                st["dil%d_s" % (gi + 1)].append(bufs_s[gi])
        elif kind == 2:
            op, kv_p, idx_p = dsa_prompt_p(zp)
            os_, kv_s, idx_s = dsa_sample_p(zs, cache_dsa_kv, cache_dsa_idx, j, page_table, past)
            st["dsa_kv_p"].append(kv_p)
            st["dsa_kv_s"].append(kv_s)
            st["dsa_idx_p"].append(idx_p)
            st["dsa_idx_s"].append(idx_s)
        else:
            op, buf_p = swa_prompt_p(zp, swa_sink[j])
            os_, buf_s = swa_sample(zs, cache_swa[j], swa_sink[j], past)
            st["swa_p"].append(buf_p)
            st["swa_s"].append(buf_s)
        hp = _add_out_proj(hp, op, w_out[j])
        hs = _add_out_proj(hs, os_, w_out[j])
        hp = _add_ffn(hp, norm_ffn[i], ffn_in_b[i], ffn_out_b[i])
        hs = _add_ffn(hs, norm_ffn[i], ffn_in_b[i], ffn_out_b[i])
    y_prompt = rms_norm(hp, norm_final)
    y_sample = rms_norm(hs, norm_final)
    return (y_prompt, y_sample,
            jnp.stack(st["nsa_kv_p"]), jnp.stack(st["nsa_kv_s"]),
            jnp.stack(st["nsa_win_p"]), jnp.stack(st["nsa_win_s"]),
            jnp.stack(st["dil1_p"]), jnp.stack(st["dil1_s"]),
            jnp.stack(st["dil2_p"]), jnp.stack(st["dil2_s"]),
            jnp.stack(st["dil3_p"]), jnp.stack(st["dil3_s"]),
            jnp.stack(st["dsa_kv_p"]), jnp.stack(st["dsa_kv_s"]),
            jnp.stack(st["dsa_idx_p"]), jnp.stack(st["dsa_idx_s"]),
            jnp.stack(st["swa_p"]), jnp.stack(st["swa_s"]))
```
